```python
import jax, jax.numpy as jnp
from jax import lax
import numpy as np

D_MODEL = 1024
BATCH = 32
SEQ = 256
DEPTH = 4
DEC_BATCH = 4
DEC_SEQ = 4096
PAST_LEN = 512

GRID_W = 64
ROPE_BASE = 10000.0
EPS = 1e-6
QBLK = 128

MLA_HEADS = 8
MLA_NOPE = 64
MLA_ROPE = 32
MLA_QK = MLA_NOPE + MLA_ROPE
MLA_V = 64
Q_LORA = 384
KV_LORA = 256
MLA_WIDTH = MLA_HEADS * MLA_V

RET_HEADS = 4
RET_DK = 64
RET_DV = 128
RET_CHUNK = 128
RET_WIDTH = RET_HEADS * RET_DV

WIN_HEADS = 16
WIN_KV_HEADS = 4
WIN_HEAD_DIM = 64
WINDOW = 128
WIN_WIDTH = WIN_HEADS * WIN_HEAD_DIM

AB_SIZES = (Q_LORA, KV_LORA, MLA_ROPE, MLA_WIDTH, RET_HEADS * RET_DK, RET_HEADS * RET_DK, RET_WIDTH, RET_WIDTH)
AB_IN = sum(AB_SIZES)
WIN_SIZES = (WIN_WIDTH, WIN_KV_HEADS * WIN_HEAD_DIM, WIN_KV_HEADS * WIN_HEAD_DIM, WIN_WIDTH)
WIN_IN = sum(WIN_SIZES)
N_EVEN = (DEPTH + 1) // 2
N_ODD = DEPTH // 2

kernel_name = 'hybrid_mla_retention_window_dit_step'

F32 = jnp.float32


def split_cols(y, sizes):
    idx = np.cumsum(sizes)[:-1].tolist()
    return jnp.split(y, idx, axis=-1)


def rms_norm(x, g):
    xf = x.astype(F32)
    y = xf * lax.rsqrt(jnp.mean(xf * xf, axis=-1, keepdims=True) + EPS)
    return (y * g.astype(F32)).astype(x.dtype)


def grid_positions(n_tokens):
    rows = n_tokens // GRID_W
    t = jnp.arange(rows * GRID_W)
    return (t // GRID_W).astype(F32), (t % GRID_W).astype(F32)


def rope_1d(x, pos):
    half = x.shape[-1] // 2
    inv = jnp.power(jnp.float32(ROPE_BASE), -jnp.arange(half, dtype=F32) / half)
    ang = pos[:, None] * inv[None, :]
    cos = jnp.cos(ang)[:, None, :]
    sin = jnp.sin(ang)[:, None, :]
    xf = x.astype(F32)
    x1, x2 = xf[..., :half], xf[..., half:]
    return jnp.concatenate([x1 * cos - x2 * sin, x1 * sin + x2 * cos], axis=-1).astype(x.dtype)


def axial_rope(x, pos):
    rows, cols = pos
    d = x.shape[-1]
    return jnp.concatenate([rope_1d(x[..., : d // 2], rows), rope_1d(x[..., d // 2:], cols)], axis=-1)


def softmax_sink(s, sink):
    if sink is None:
        return jax.nn.softmax(s, axis=-1)
    sk = sink.astype(F32).reshape(1, s.shape[1], s.shape[2], 1, 1)
    m = jnp.maximum(s.max(axis=-1, keepdims=True), sk)
    e = jnp.exp(s - m)
    return e / (e.sum(axis=-1, keepdims=True) + jnp.exp(sk - m))


def block_attention(q, k, v, sink):
    B, Lq, H, D = q.shape
    KVH = k.shape[2]
    G = H // KVH
    E = v.shape[-1]
    NB = Lq // QBLK
    scale = D ** -0.5
    qb = q.reshape(B, NB, QBLK, KVH, G, D).transpose(1, 0, 2, 3, 4, 5)

    def one(q_blk):
        s = jnp.einsum('bqhgd,bkhd->bhgqk', q_blk, k).astype(F32) * scale
        pr = softmax_sink(s, sink).astype(v.dtype)
        return jnp.einsum('bhgqk,bkhe->bqhge', pr, v)

    o = lax.map(one, qb)
    return o.transpose(1, 0, 2, 3, 4, 5).reshape(B, Lq, H, E)


def window_attention(q, k, v, kc, vc, sink):
    B, L, H, D = q.shape
    KVH = k.shape[2]
    G = H // KVH
    NB = L // QBLK
    W3 = 3 * QBLK
    P = kc.shape[1]
    scale = D ** -0.5
    qb = q.reshape(B, NB, QBLK, KVH, G, D).transpose(1, 0, 2, 3, 4, 5)

    def band(x):
        xp = jnp.pad(x, ((0, 0), (QBLK, QBLK), (0, 0), (0, 0))).reshape(B, NB + 2, QBLK, KVH, x.shape[-1])
        xb = jnp.concatenate([xp[:, :-2], xp[:, 1:-1], xp[:, 2:]], axis=2)
        return xb.transpose(1, 0, 2, 3, 4)

    kb, vb = band(k), band(v)
    q_idx = jnp.arange(QBLK)[:, None]
    k_idx = jnp.arange(W3)[None, :]
    in_window = jnp.abs(k_idx - QBLK - q_idx) <= WINDOW
    kpos = (jnp.arange(NB)[:, None] - 1) * QBLK + jnp.arange(W3)[None, :]
    in_range = (kpos >= 0) & (kpos < L)
    mask = in_window[None, :, :] & in_range[:, None, :]

    def one(args):
        q_blk, k_blk, v_blk, m_blk = args
        s_ctx = jnp.einsum('bqhgd,bkhd->bhgqk', q_blk, kc).astype(F32) * scale
        s_loc = jnp.einsum('bqhgd,bkhd->bhgqk', q_blk, k_blk).astype(F32) * scale
        s_loc = jnp.where(m_blk, s_loc, -jnp.inf)
        pr = softmax_sink(jnp.concatenate([s_ctx, s_loc], axis=-1), sink).astype(v.dtype)
        return (jnp.einsum('bhgqk,bkhe->bqhge', pr[..., :P], vc)
                + jnp.einsum('bhgqk,bkhe->bqhge', pr[..., P:], v_blk))

    o = lax.map(one, (qb, kb, vb, mask))
    return o.transpose(1, 0, 2, 3, 4, 5).reshape(B, L, H, v.shape[-1])


def retention_scan(q, k, v, log_g, state0):
    B, L, H, DK = q.shape
    DV = v.shape[-1]
    C = RET_CHUNK
    N = L // C
    qc = q.reshape(B, N, C, H, DK)
    kc = k.reshape(B, N, C, H, DK)
    vc = v.reshape(B, N, C, H, DV)
    pos = jnp.arange(C, dtype=F32)
    rel = pos[:, None] - pos[None, :]
    causal = rel >= 0
    dmask = jnp.where(causal[None], jnp.exp(jnp.where(causal, rel, 0.0)[None] * log_g[:, None, None]), 0.0)
    s = jnp.einsum('bnqhd,bnkhd->bnhqk', qc, kc) * dmask.astype(q.dtype)
    intra = jnp.einsum('bnhqk,bnkhe->bnqhe', s, vc)
    k_w = jnp.exp((C - 1 - pos)[:, None] * log_g[None, :]).astype(k.dtype)
    U = jnp.einsum('bnkhd,kh,bnkhe->nbhde', kc, k_w, vc).astype(F32)
    g_chunk = jnp.exp(C * log_g)[None, :, None, None]

    def step(R, u):
        return g_chunk * R + u, R

    R_fin, R_prev = lax.scan(step, state0.astype(F32), U)
    q_w = jnp.exp((pos + 1.0)[:, None] * log_g[None, :]).astype(q.dtype)
    inter = jnp.einsum('bnqhd,qh,nbhde->bnqhe', qc, q_w, R_prev.astype(q.dtype))
    return (intra + inter).reshape(B, L, H, DV), R_fin.astype(v.dtype)


def bidir_retention(q, k, v, log_g2, sf, sb):
    of, rf = retention_scan(q, k, v, log_g2[0], sf)
    ob, rb = retention_scan(q[:, ::-1], k[:, ::-1], v[:, ::-1], log_g2[1], sb)
    return of + ob[:, ::-1], rf, rb


def head_group_norm(y, g):
    yf = y.astype(F32)
    mu = yf.mean(axis=-1, keepdims=True)
    var = jnp.mean(jnp.square(yf - mu), axis=-1, keepdims=True)
    out = (yf - mu) * lax.rsqrt(var + EPS) * g.astype(F32).reshape(y.shape[2], y.shape[3])
    return out.astype(y.dtype)


def mla_kv(ckv, krope, w_ukv, k_head_g):
    B, L, _ = ckv.shape
    kv = (ckv @ w_ukv).reshape(B, L, MLA_HEADS, MLA_NOPE + MLA_V)
    k_nope, v = kv[..., :MLA_NOPE], kv[..., MLA_NOPE:]
    k = jnp.concatenate([k_nope, jnp.broadcast_to(krope[:, :, None, :], (B, L, MLA_HEADS, MLA_ROPE))], axis=-1)
    return rms_norm(k, k_head_g), v


def rope_tail(x, pos):
    return jnp.concatenate([x[..., :MLA_NOPE], axial_rope(x[..., MLA_NOPE:], pos)], axis=-1)


def mixer_ab(h, p, ctx, pos):
    B, L, _ = h.shape
    cq, ckv, krope, g_a, rq, rk, rv, g_b = split_cols(h @ p['w_in'], AB_SIZES)
    ckv = rms_norm(ckv, p['kv_norm_g'])
    q = (rms_norm(cq, p['q_norm_g']) @ p['w_uq']).reshape(B, L, MLA_HEADS, MLA_QK)
    q = rms_norm(q, p['q_head_g'])
    k, v = mla_kv(ckv, krope, p['w_ukv'], p['k_head_g'])
    rq = rq.reshape(B, L, RET_HEADS, RET_DK) * (RET_DK ** -0.5)
    rk = rk.reshape(B, L, RET_HEADS, RET_DK)
    rv = rv.reshape(B, L, RET_HEADS, RET_DV)
    log_g = -jnp.exp(p['decay'].astype(F32))
    if ctx is None:
        a = block_attention(q, k, v, None)
        sf = jnp.zeros((B, RET_HEADS, RET_DK, RET_DV), F32)
        sb = sf
    else:
        ckv_c, krope_c, sf, sb = ctx
        kc, vc = mla_kv(ckv_c, krope_c, p['w_ukv'], p['k_head_g'])
        a = block_attention(rope_tail(q, pos), jnp.concatenate([kc, rope_tail(k, pos)], axis=1),
                            jnp.concatenate([vc, v], axis=1), None)
        rq = axial_rope(rq, pos)
        rk = axial_rope(rk, pos)
    r, rf, rb = bidir_retention(rq, rk, rv, log_g, sf, sb)
    r = head_group_norm(r, p['ret_norm_g'])
    mixed = jnp.concatenate([a.reshape(B, L, MLA_WIDTH) * jax.nn.silu(g_a),
                             r.reshape(B, L, RET_WIDTH) * jax.nn.silu(g_b)], axis=-1)
    return mixed @ p['w_out'], (ckv, krope, rf, rb)


def mixer_win(h, p, ctx, pos):
    B, L, _ = h.shape
    q, k, v, g = split_cols(h @ p['w_in'], WIN_SIZES)
    q = rms_norm(q.reshape(B, L, WIN_HEADS, WIN_HEAD_DIM), p['q_head_g'])
    k = rms_norm(k.reshape(B, L, WIN_KV_HEADS, WIN_HEAD_DIM), p['k_head_g'])
    v = v.reshape(B, L, WIN_KV_HEADS, WIN_HEAD_DIM)
    if ctx is None:
        o = block_attention(q, k, v, p['sink'])
    else:
        kc, vc = ctx
        o = window_attention(axial_rope(q, pos), axial_rope(k, pos), v, kc, vc, p['sink'])
    return (o.reshape(B, L, WIN_WIDTH) * jax.nn.silu(g)) @ p['w_out'], (k, v)


def ada_modulate(x, cond, w, b, g):
    mod = jax.nn.silu(cond) @ w + b
    shift, scale, gate = jnp.split(mod, 3, axis=-1)
    h = rms_norm(x, g) * (1.0 + scale[:, None, :]) + shift[:, None, :]
    return h, gate[:, None, :]


def setup_inputs(seed: int = 0) -> dict:
    key = jax.random.key(seed)
    ks = iter(jax.random.split(key, 48))

    def nrm(shape, scale):
        return scale * jax.random.normal(next(ks), shape, jnp.float32)

    D = D_MODEL
    inp = {}
    inp['x_prompt'] = nrm((BATCH, SEQ, D), 1.0)
    inp['x_sample'] = nrm((DEC_BATCH, DEC_SEQ, D), 1.0)
    for l in range(DEPTH):
        if l % 2 == 0:
            inp['cache_l%d_mla_ckv' % l] = nrm((DEC_BATCH, PAST_LEN, KV_LORA), 1.0)
            inp['cache_l%d_mla_krope' % l] = nrm((DEC_BATCH, PAST_LEN, MLA_ROPE), 1.0)
            inp['state_l%d_ret_fwd' % l] = nrm((DEC_BATCH, RET_HEADS, RET_DK, RET_DV), 1.0)
            inp['state_l%d_ret_bwd' % l] = nrm((DEC_BATCH, RET_HEADS, RET_DK, RET_DV), 1.0)
        else:
            inp['cache_l%d_win_k' % l] = nrm((DEC_BATCH, PAST_LEN, WIN_KV_HEADS, WIN_HEAD_DIM), 1.0)
            inp['cache_l%d_win_v' % l] = nrm((DEC_BATCH, PAST_LEN, WIN_KV_HEADS, WIN_HEAD_DIM), 1.0)
    inp['c'] = nrm((DEC_BATCH, D), 1.0)
    inp['c_ctx'] = nrm((D,), 1.0)
    inp['ada_w'] = nrm((DEPTH, D, 3 * D), 0.5 * D ** -0.5)
    inp['ada_b'] = nrm((DEPTH, 3 * D), 0.02)
    inp['norm_g'] = 1.0 + nrm((DEPTH, D), 0.02)
    inp['ab_w_in'] = nrm((N_EVEN, D, AB_IN), D ** -0.5)
    inp['mla_q_norm_g'] = 1.0 + nrm((N_EVEN, Q_LORA), 0.02)
    inp['mla_w_uq'] = nrm((N_EVEN, Q_LORA, MLA_HEADS * MLA_QK), Q_LORA ** -0.5)
    inp['mla_kv_norm_g'] = 1.0 + nrm((N_EVEN, KV_LORA), 0.02)
    inp['mla_w_ukv'] = nrm((N_EVEN, KV_LORA, MLA_HEADS * (MLA_NOPE + MLA_V)), KV_LORA ** -0.5)
    inp['mla_q_head_g'] = 1.0 + nrm((N_EVEN, MLA_QK), 0.02)
    inp['mla_k_head_g'] = 1.0 + nrm((N_EVEN, MLA_QK), 0.02)
    hh = np.arange(RET_HEADS)
    base = np.stack([np.log(-np.log(1.0 - 2.0 ** (-5.0 - hh))),
                     np.log(-np.log(1.0 - 2.0 ** (-5.5 - hh)))])
    inp['ret_decay'] = jnp.asarray(base, jnp.float32)[None] + nrm((N_EVEN, 2, RET_HEADS), 0.05)
    inp['ret_norm_g'] = 1.0 + nrm((N_EVEN, RET_WIDTH), 0.02)
    inp['ab_w_out'] = nrm((N_EVEN, MLA_WIDTH + RET_WIDTH, D), (MLA_WIDTH + RET_WIDTH) ** -0.5)
    inp['win_w_in'] = nrm((N_ODD, D, WIN_IN), D ** -0.5)
    inp['win_q_head_g'] = 1.0 + nrm((N_ODD, WIN_HEAD_DIM), 0.02)
    inp['win_k_head_g'] = 1.0 + nrm((N_ODD, WIN_HEAD_DIM), 0.02)
    inp['win_sink'] = nrm((N_ODD, WIN_HEADS), 0.5)
    inp['win_w_out'] = nrm((N_ODD, WIN_WIDTH, D), WIN_WIDTH ** -0.5)
    return inp


def reference(x_prompt, x_sample,
              cache_l0_mla_ckv, cache_l0_mla_krope, state_l0_ret_fwd, state_l0_ret_bwd,
              cache_l1_win_k, cache_l1_win_v,
              cache_l2_mla_ckv, cache_l2_mla_krope, state_l2_ret_fwd, state_l2_ret_bwd,
              cache_l3_win_k, cache_l3_win_v,
              c, c_ctx, ada_w, ada_b, norm_g,
              ab_w_in, mla_q_norm_g, mla_w_uq, mla_kv_norm_g, mla_w_ukv, mla_q_head_g, mla_k_head_g,
              ret_decay, ret_norm_g, ab_w_out,
              win_w_in, win_q_head_g, win_k_head_g, win_sink, win_w_out):
    pos = grid_positions(x_sample.shape[1])
    ctx_caches = ((cache_l0_mla_ckv, cache_l0_mla_krope, state_l0_ret_fwd, state_l0_ret_bwd),
                  (cache_l1_win_k, cache_l1_win_v),
                  (cache_l2_mla_ckv, cache_l2_mla_krope, state_l2_ret_fwd, state_l2_ret_bwd),
                  (cache_l3_win_k, cache_l3_win_v))
    cond_ctx = c_ctx[None, :]
    y_p, y_s = x_prompt, x_sample
    new_state = []
    for l in range(DEPTH):
        i = l // 2
        if l % 2 == 0:
            p = {'w_in': ab_w_in[i], 'q_norm_g': mla_q_norm_g[i], 'w_uq': mla_w_uq[i],
                 'kv_norm_g': mla_kv_norm_g[i], 'w_ukv': mla_w_ukv[i], 'q_head_g': mla_q_head_g[i],
                 'k_head_g': mla_k_head_g[i], 'decay': ret_decay[i], 'ret_norm_g': ret_norm_g[i],
                 'w_out': ab_w_out[i]}
            mixer = mixer_ab
        else:
            p = {'w_in': win_w_in[i], 'q_head_g': win_q_head_g[i], 'k_head_g': win_k_head_g[i],
                 'sink': win_sink[i], 'w_out': win_w_out[i]}
            mixer = mixer_win
        h_p, gate_p = ada_modulate(y_p, cond_ctx, ada_w[l], ada_b[l], norm_g[l])
        out_p, st = mixer(h_p, p, None, None)
        h_s, gate_s = ada_modulate(y_s, c, ada_w[l], ada_b[l], norm_g[l])
        out_s, _ = mixer(h_s, p, ctx_caches[l], pos)
        y_p = y_p + gate_p * out_p
        y_s = y_s + gate_s * out_s
        new_state.append(st)
    (l0_ckv, l0_krope, l0_rf, l0_rb), (l1_k, l1_v), (l2_ckv, l2_krope, l2_rf, l2_rb), (l3_k, l3_v) = new_state
    return (y_p, y_s, l0_ckv, l0_krope, l0_rf, l0_rb, l1_k, l1_v, l2_ckv, l2_krope, l2_rf, l2_rb, l3_k, l3_v)
```

```python
import functools

import numpy as np
import jax
import jax.numpy as jnp
from jax import lax
from jax.experimental import pallas as pl
from jax.experimental.pallas import tpu as pltpu

D_MODEL = 1024
DEPTH = 4
GRID_W = 64
ROPE_BASE = 10000.0
EPS = 1e-6

MLA_HEADS = 8
MLA_NOPE = 64
MLA_ROPE = 32
MLA_QK = MLA_NOPE + MLA_ROPE
MLA_V = 64
Q_LORA = 384
KV_LORA = 256
MLA_WIDTH = MLA_HEADS * MLA_V

RET_HEADS = 4
RET_DK = 64
RET_DV = 128
RET_CHUNK = 128
RET_WIDTH = RET_HEADS * RET_DV

WIN_HEADS = 16
WIN_KV_HEADS = 4
WIN_HEAD_DIM = 64
WINDOW = 128
WIN_WIDTH = WIN_HEADS * WIN_HEAD_DIM

LANES = 128
F32 = jnp.float32
BF16 = jnp.bfloat16
NEG = -1e30
VMEM_LIMIT = 52 * 1024 * 1024

AB_OFF = dict(cq=0, ckv=384, krope=640, ga=672, rq=1184, rk=1440, rv=1696, gb=2208)
WIN_OFF = dict(q=0, k=1024, v=1280, g=1536)


def _params(n_axes):
    return pltpu.CompilerParams(dimension_semantics=("arbitrary",) * n_axes, vmem_limit_bytes=VMEM_LIMIT)


def _const_spec(shape):
    nd = len(shape)
    return pl.BlockSpec(shape, lambda *_: (0,) * nd)


def _swap_idx(dim):
    d2 = dim // 2
    half = d2 // 2
    one = np.concatenate([np.arange(half, d2), np.arange(0, half)])
    return np.concatenate([one, d2 + one])


def _take_cols(w, cols):
    cols = np.asarray(cols, np.int32)
    picked = jnp.take(w, jnp.asarray(np.maximum(cols, 0)), axis=1)
    return jnp.where(jnp.asarray(cols >= 0)[None, :], picked, 0.0)


def _rope_tables(n_tokens, dim):
    t = jnp.arange(n_tokens)
    rows = (t // GRID_W).astype(F32)
    cols = (t % GRID_W).astype(F32)
    half = dim // 4
    inv = jnp.power(jnp.float32(ROPE_BASE), -jnp.arange(half, dtype=F32) / half)

    def one(pos):
        ang = pos[:, None] * inv[None, :]
        c, s = jnp.cos(ang), jnp.sin(ang)
        return jnp.concatenate([c, c], -1), jnp.concatenate([-s, s], -1)

    cr, sr = one(rows)
    cc, sc = one(cols)
    return jnp.concatenate([cr, cc], -1), jnp.concatenate([sr, sc], -1)


def _ada_kernel(cond_ref, w_ref, b_ref, o_ref):
    c = cond_ref[...]
    sc = (c * jax.nn.sigmoid(c)).astype(BF16)
    o_ref[0] = jnp.dot(sc, w_ref[0].astype(BF16), preferred_element_type=F32) + b_ref[0]


def _ada_mod(cond, ada_w, ada_b):
    tn = 768
    n3 = 3 * D_MODEL
    return pl.pallas_call(
        _ada_kernel,
        grid=(DEPTH, n3 // tn),
        in_specs=[pl.BlockSpec((8, D_MODEL), lambda l, j: (0, 0)),
                  pl.BlockSpec((1, D_MODEL, tn), lambda l, j: (l, 0, j)),
                  pl.BlockSpec((1, 1, tn), lambda l, j: (l, 0, j))],
        out_specs=pl.BlockSpec((1, 8, tn), lambda l, j: (l, 0, j)),
        out_shape=jax.ShapeDtypeStruct((DEPTH, 8, n3), F32),
        compiler_params=_params(2),
        name="ada_mod",
    )(cond, ada_w, ada_b.reshape(DEPTH, 1, n3))


def _rms(x, g):
    return x * lax.rsqrt(jnp.mean(x * x, axis=-1, keepdims=True) + EPS) * g


def _modulated(x_ref, mod_ref, ng_ref):
    x = x_ref[...]
    shift = mod_ref[0, :, 0:D_MODEL]
    scale = mod_ref[0, :, D_MODEL:2 * D_MODEL]
    return (_rms(x, ng_ref[...]) * (1.0 + scale) + shift).astype(BF16)


def _seg_rsqrt(x, seg_ref, n_real):
    ssq = jnp.dot((x * x).astype(BF16), seg_ref[...], preferred_element_type=F32)
    return lax.rsqrt(ssq * (1.0 / n_real) + EPS)


def _silu(g):
    return g * jax.nn.sigmoid(g)


def _prep_ab_kernel(rope, *refs):
    if rope:
        (x_ref, mod_ref, ng_ref, win_ref, qng_ref, kvng_ref, wq_ref, wqs_ref, wk_ref, wv_ref, seg_ref,
         qg_ref, qgs_ref, kg_ref, kgs_ref, cm_ref, sm_ref, cr_ref, sr_ref,
         q_out, k_out, v_out, gate_out, rq_out, rk_out, rv_out) = refs
        o_kr, o_g, o_rq, o_rk, o_rv = 640, 896, 1920, 2176, 2944
    else:
        (x_ref, mod_ref, ng_ref, win_ref, qng_ref, kvng_ref, wq_ref, wk_ref, wv_ref, seg_ref,
         qg_ref, kg_ref,
         q_out, k_out, v_out, gate_out, rq_out, rk_out, rv_out, ckv_out, kr_out) = refs
        o_kr, o_g, o_rq, o_rk, o_rv = 640, 768, 1792, 2048, 2304

    h = _modulated(x_ref, mod_ref, ng_ref)
    proj = jnp.dot(h, win_ref[...], preferred_element_type=F32)

    cqn = _rms(proj[:, 0:Q_LORA], qng_ref[...]).astype(BF16)
    ckvn_f = _rms(proj[:, Q_LORA:Q_LORA + KV_LORA], kvng_ref[...])
    ckvn = ckvn_f.astype(BF16)
    kr = proj[:, o_kr:o_kr + LANES]

    gate_out[...] = _silu(proj[:, o_g:o_g + 2 * MLA_WIDTH]).astype(BF16)
    rv_out[...] = proj[:, o_rv:o_rv + RET_WIDTH].astype(BF16)

    qm = jnp.dot(cqn, wq_ref[...], preferred_element_type=F32)
    kn = jnp.dot(ckvn, wk_ref[...], preferred_element_type=F32)
    v_out[...] = jnp.dot(ckvn, wv_ref[...], preferred_element_type=F32).astype(BF16)

    qscale = MLA_QK ** -0.5
    if rope:
        qs = jnp.dot(cqn, wqs_ref[...], preferred_element_type=F32)
        krs = proj[:, 768:768 + LANES]
        cm, sm = cm_ref[...], sm_ref[...]
        q_c = cm * qg_ref[...]
        q_s = sm * qgs_ref[...]
        kr_rot = kr * (cm * kg_ref[...]) + krs * (sm * kgs_ref[...])
        cr, sr = cr_ref[...], sr_ref[...]
        rq = proj[:, o_rq:o_rq + 256]
        rk = proj[:, o_rk:o_rk + 256]
        rqs = proj[:, 2432:2432 + 256]
        rks = proj[:, 2688:2688 + 256]
        for c in range(2):
            sl = slice(LANES * c, LANES * (c + 1))
            rq_out[:, sl] = ((rq[:, sl] * cr + rqs[:, sl] * sr) * (RET_DK ** -0.5)).astype(BF16)
            rk_out[:, sl] = (rk[:, sl] * cr + rks[:, sl] * sr).astype(BF16)
    else:
        q_c = jnp.broadcast_to(qg_ref[...], (qm.shape[0], LANES))
        kr_rot = kr * kg_ref[...]
        rq_out[...] = (proj[:, o_rq:o_rq + 256] * (RET_DK ** -0.5)).astype(BF16)
        rk_out[...] = proj[:, o_rk:o_rk + 256].astype(BF16)
        ckv_out[...] = ckvn_f
        kr_out[...] = kr

    kg = kg_ref[...]
    q_c2 = jnp.concatenate([q_c, q_c], axis=-1)
    kr2 = jnp.concatenate([kr, kr], axis=-1)
    kr_rot2 = jnp.concatenate([kr_rot, kr_rot], axis=-1)
    kg2 = jnp.concatenate([kg, kg], axis=-1)
    if rope:
        q_s2 = jnp.concatenate([q_s, q_s], axis=-1)
    for g in range(MLA_HEADS // 2):
        sl = slice(2 * LANES * g, 2 * LANES * (g + 1))
        qg = qm[:, sl]
        rq_n = _seg_rsqrt(qg, seg_ref, MLA_QK)
        qv = qg * q_c2
        if rope:
            qv = qv + qs[:, sl] * q_s2
        q_out[:, sl] = (qv * (rq_n * qscale)).astype(BF16)
        kgp = kn[:, sl]
        rk_n = _seg_rsqrt(kgp + kr2, seg_ref, MLA_QK)
        k_out[:, sl] = ((kgp * kg2 + kr_rot2) * rk_n).astype(BF16)


def _ab_weights(w_in, w_uq, w_ukv, q_head_g, k_head_g, rope):
    sw32 = _swap_idx(MLA_ROPE)
    sw64 = _swap_idx(RET_DK)
    z = lambda n: [-1] * n
    o = AB_OFF
    cols = list(range(o['cq'], o['cq'] + Q_LORA)) + list(range(o['ckv'], o['ckv'] + KV_LORA))
    cols += z(64) + list(range(o['krope'], o['krope'] + MLA_ROPE)) + z(32)
    if rope:
        cols += z(64) + list(o['krope'] + sw32) + z(32)
    cols += list(range(o['ga'], o['ga'] + MLA_WIDTH)) + list(range(o['gb'], o['gb'] + RET_WIDTH))
    cols += list(range(o['rq'], o['rq'] + 256)) + list(range(o['rk'], o['rk'] + 256))
    if rope:
        for base in (o['rq'], o['rk']):
            for hh in range(RET_HEADS):
                cols += list(base + RET_DK * hh + sw64)
    cols += list(range(o['rv'], o['rv'] + RET_WIDTH))
    win = _take_cols(w_in, cols).astype(BF16)

    qc, qsc, kc, vc = [], [], [], []
    for hh in range(MLA_HEADS):
        qb = MLA_QK * hh
        qc += list(range(qb, qb + MLA_QK)) + z(32)
        qsc += z(64) + list(qb + MLA_NOPE + sw32) + z(32)
        kb = (MLA_NOPE + MLA_V) * hh
        kc += list(range(kb, kb + MLA_NOPE)) + z(64)
        vcols = list(range(kb + MLA_NOPE, kb + MLA_NOPE + MLA_V))
        vc += (vcols + z(64)) if hh % 2 == 0 else (z(64) + vcols)
    wq = _take_cols(w_uq, qc).astype(BF16)
    wqs = _take_cols(w_uq, qsc).astype(BF16) if rope else None
    wk = _take_cols(w_ukv, kc).astype(BF16)
    wv = _take_cols(w_ukv, vc).astype(BF16)

    pad = lambda g: jnp.concatenate([g, jnp.zeros((LANES - MLA_QK,), F32)])[None, :]
    gsw = lambda g: jnp.concatenate([jnp.zeros((MLA_NOPE,), F32), g[MLA_NOPE + sw32],
                                     jnp.zeros((LANES - MLA_QK,), F32)])[None, :]
    qg, kg = pad(q_head_g), pad(k_head_g)
    qgs, kgs = (gsw(q_head_g), gsw(k_head_g)) if rope else (None, None)
    return win, wq, wqs, wk, wv, qg, qgs, kg, kgs


def _seg_matrix(width):
    idx = np.arange(2 * LANES) // width
    return jnp.asarray((idx[:, None] == idx[None, :]).astype(np.float32), BF16)


def _prep_ab(x, mod, norm_g, p, rope, tokens_per_batch, mod_row, tables, tm):
    T = x.shape[0]
    win, wq, wqs, wk, wv, qg, qgs, kg, kgs = _ab_weights(
        p['w_in'], p['w_uq'], p['w_ukv'], p['q_head_g'], p['k_head_g'], rope)
    seg = _seg_matrix(LANES)
    nw = win.shape[1]
    blocks_per_batch = tokens_per_batch // tm
    tok = lambda w: pl.BlockSpec((tm, w), lambda i: (i, 0))
    if mod_row is None:
        mod_spec = pl.BlockSpec((1, 1, 3 * D_MODEL), lambda i: (i // blocks_per_batch, 0, 0))
    else:
        mod_spec = pl.BlockSpec((1, 1, 3 * D_MODEL), lambda i: (mod_row, 0, 0))
    vec = lambda a: (a, _const_spec(a.shape))
    ins = [(x, tok(D_MODEL)), (mod, mod_spec), vec(norm_g[None, :]), vec(win),
           vec(p['q_norm_g'][None, :]), vec(p['kv_norm_g'][None, :]), vec(wq)]
    if rope:
        ins.append(vec(wqs))
    ins += [vec(wk), vec(wv), vec(seg), vec(qg)]
    if rope:
        ins.append(vec(qgs))
    ins.append(vec(kg))
    if rope:
        ins.append(vec(kgs))
        nblk = tokens_per_batch // tm
        tab = lambda a: (a, pl.BlockSpec((tm, LANES), lambda i: (i % nblk, 0)))
        ins += [tab(tables['cm']), tab(tables['sm']), tab(tables['c64']), tab(tables['s64'])]
    outs = [(1024, BF16), (1024, BF16), (1024, BF16), (1024, BF16), (256, BF16), (256, BF16), (512, BF16)]
    if not rope:
        outs += [(KV_LORA, F32), (LANES, F32)]
    return pl.pallas_call(
        functools.partial(_prep_ab_kernel, rope),
        grid=(T // tm,),
        in_specs=[s for _, s in ins],
        out_specs=[tok(w) for w, _ in outs],
        out_shape=[jax.ShapeDtypeStruct((T, w), dt) for w, dt in outs],
        compiler_params=_params(1),
        name="prep_ab_rope" if rope else "prep_ab",
    )(*[a for a, _ in ins])


def _prep_ctx_kernel(ckv_ref, kr_ref, wk_ref, wv_ref, seg_ref, kg_ref, k_out, v_out):
    ckv = ckv_ref[...].astype(BF16)
    kn = jnp.dot(ckv, wk_ref[...], preferred_element_type=F32)
    v_out[...] = jnp.dot(ckv, wv_ref[...], preferred_element_type=F32).astype(BF16)
    kr = kr_ref[...]
    kg = kg_ref[...]
    kr2 = jnp.concatenate([kr, kr], axis=-1)
    kg2 = jnp.concatenate([kg, kg], axis=-1)
    for g in range(MLA_HEADS // 2):
        sl = slice(2 * LANES * g, 2 * LANES * (g + 1))
        kraw = kn[:, sl] + kr2
        k_out[:, sl] = (kraw * kg2 * _seg_rsqrt(kraw, seg_ref, MLA_QK)).astype(BF16)


def _prep_ctx(ckv_c, kr_c, wk, wv, kg, tm):
    T = ckv_c.shape[0]
    seg = _seg_matrix(LANES)
    tok = lambda w: pl.BlockSpec((tm, w), lambda i: (i, 0))
    return pl.pallas_call(
        _prep_ctx_kernel,
        grid=(T // tm,),
        in_specs=[tok(KV_LORA), tok(LANES), _const_spec(wk.shape), _const_spec(wv.shape),
                  _const_spec(seg.shape), _const_spec(kg.shape)],
        out_specs=[tok(1024), tok(1024)],
        out_shape=[jax.ShapeDtypeStruct((T, 1024), BF16)] * 2,
        compiler_params=_params(1),
        name="prep_ctx",
    )(ckv_c, kr_c, wk, wv, seg, kg)


def _mla_attn_kernel(tk, q_ref, k_ref, v_ref, g_ref, o_ref):
    tq = q_ref.shape[1]
    nk = k_ref.shape[1] // tk
    out = None
    for j in range(2):
        sl = slice(LANES * j, LANES * (j + 1))
        q = q_ref[0, :, sl]

        def body(c, carry, sl=sl, q=q):
            m, l, acc = carry
            ks = pl.multiple_of(c * tk, tk)
            k = k_ref[0, pl.ds(ks, tk), sl]
            v = v_ref[0, pl.ds(ks, tk), sl]
            s = lax.dot_general(q, k, (((1,), (1,)), ((), ())), preferred_element_type=F32)
            mn = jnp.maximum(m, jnp.max(s, axis=-1, keepdims=True))
            p = jnp.exp(s - mn)
            alpha = jnp.exp(m - mn)
            l = alpha * l + jnp.sum(p, axis=-1, keepdims=True)
            acc = alpha * acc + jnp.dot(p.astype(BF16), v, preferred_element_type=F32)
            return mn, l, acc

        init = (jnp.full((tq, 1), NEG, F32), jnp.zeros((tq, 1), F32), jnp.zeros((tq, LANES), F32))
        _, l, acc = lax.fori_loop(0, nk, body, init)
        oj = acc / l
        out = oj if out is None else out + oj
    o_ref[0] = (out * g_ref[0].astype(F32)).astype(BF16)


def _mla_attn(q, k, v, gates, tq, tk):
    B, L, _ = q.shape
    Lk = k.shape[1]
    hp = MLA_HEADS // 2
    return pl.pallas_call(
        functools.partial(_mla_attn_kernel, tk),
        grid=(B, hp, L // tq),
        in_specs=[pl.BlockSpec((1, tq, 2 * LANES), lambda b, h, i: (b, i, h)),
                  pl.BlockSpec((1, Lk, 2 * LANES), lambda b, h, i: (b, 0, h)),
                  pl.BlockSpec((1, Lk, 2 * LANES), lambda b, h, i: (b, 0, h)),
                  pl.BlockSpec((1, tq, LANES), lambda b, h, i: (b, i, h))],
        out_specs=pl.BlockSpec((1, tq, LANES), lambda b, h, i: (b, i, h)),
        out_shape=jax.ShapeDtypeStruct((B, L, MLA_WIDTH), BF16),
        compiler_params=_params(3),
        name="mla_attn",
    )(q, k, v, gates)


def _ret_kernel(n_chunks, decay_ref, q_ref, k_ref, v_ref, sf_ref, sb_ref, gn_ref, gate_ref,
                o_ref, rf_ref, rb_ref, acc_ref, st_ref):
    C = RET_CHUNK
    pair = pl.program_id(1)
    row = lax.broadcasted_iota(jnp.int32, (C, C), 0).astype(F32)
    col = lax.broadcasted_iota(jnp.int32, (C, C), 1).astype(F32)
    lane_head = lax.broadcasted_iota(jnp.int32, (C, LANES), 1) // RET_DK
    rel = row - col

    tabs = []
    for hh in range(2):
        per_dir = []
        for d in range(2):
            lg = -jnp.exp(jnp.full((C, C), decay_ref[d, 2 * pair + hh], F32))
            if d == 0:
                dmask = jnp.where(rel >= 0, jnp.exp(jnp.maximum(rel, 0.0) * lg), 0.0)
                k_w = jnp.exp((C - 1.0 - row) * lg)
                q_w = jnp.exp((row + 1.0) * lg)
            else:
                dmask = jnp.where(rel <= 0, jnp.exp(jnp.maximum(-rel, 0.0) * lg), 0.0)
                k_w = jnp.exp(row * lg)
                q_w = jnp.exp((C - row) * lg)
            per_dir.append((dmask, k_w, q_w, jnp.exp(C * lg)))
            st_ref[2 * hh + d] = jnp.zeros((C, RET_DV), F32)
        tabs.append(per_dir)
        st_ref[2 * hh + 0, RET_DK * hh:RET_DK * (hh + 1), :] = sf_ref[0, hh]
        st_ref[2 * hh + 1, RET_DK * hh:RET_DK * (hh + 1), :] = sb_ref[0, hh]
    acc_ref[...] = jnp.zeros_like(acc_ref)

    def step(n, _):
        for d in range(2):
            cidx = n if d == 0 else n_chunks - 1 - n
            r0 = pl.multiple_of(cidx * C, C)
            qc = q_ref[0, pl.ds(r0, C), :].astype(F32)
            kc = k_ref[0, pl.ds(r0, C), :]
            kf = kc.astype(F32)
            for hh in range(2):
                dmask, k_w, q_w, g_chunk = tabs[hh][d]
                vc = v_ref[0, pl.ds(r0, C), RET_DV * hh:RET_DV * (hh + 1)]
                qm = jnp.where(lane_head == hh, qc, 0.0)
                s = lax.dot_general(qm.astype(BF16), kc, (((1,), (1,)), ((), ())),
                                    preferred_element_type=F32) * dmask
                intra = jnp.dot(s.astype(BF16), vc, preferred_element_type=F32)
                state = st_ref[2 * hh + d]
                inter = jnp.dot((qm * q_w).astype(BF16), state.astype(BF16), preferred_element_type=F32)
                u = jnp.dot((kf * k_w).T.astype(BF16), vc, preferred_element_type=F32)
                st_ref[2 * hh + d] = g_chunk * state + u
                acc_ref[pl.ds(r0, C), RET_DV * hh:RET_DV * (hh + 1)] += intra + inter
        return 0

    lax.fori_loop(0, n_chunks, step, 0)

    for hh in range(2):
        rf_ref[0, hh] = st_ref[2 * hh + 0, RET_DK * hh:RET_DK * (hh + 1), :]
        rb_ref[0, hh] = st_ref[2 * hh + 1, RET_DK * hh:RET_DK * (hh + 1), :]

    def norm(n, _):
        r0 = pl.multiple_of(n * C, C)
        for hh in range(2):
            sl = slice(RET_DV * hh, RET_DV * (hh + 1))
            y = acc_ref[pl.ds(r0, C), sl]
            mu = jnp.mean(y, axis=-1, keepdims=True)
            yc = y - mu
            var = jnp.mean(yc * yc, axis=-1, keepdims=True)
            out = yc * lax.rsqrt(var + EPS) * gn_ref[:, sl]
            o_ref[0, pl.ds(r0, C), sl] = (out * gate_ref[0, pl.ds(r0, C), sl].astype(F32)).astype(BF16)
        return 0

    lax.fori_loop(0, n_chunks, norm, 0)


def _retention(rq, rk, rv, sf, sb, decay, ret_norm_g, gates):
    B, L, _ = rq.shape
    n_chunks = L // RET_CHUNK
    hp = RET_HEADS // 2
    st_spec = pl.BlockSpec((1, 2, RET_DK, RET_DV), lambda b, h: (b, h, 0, 0))
    st_shape = jax.ShapeDtypeStruct((B, RET_HEADS, RET_DK, RET_DV), F32)
    return pl.pallas_call(
        functools.partial(_ret_kernel, n_chunks),
        grid=(B, hp),
        in_specs=[pl.BlockSpec(memory_space=pltpu.SMEM),
                  pl.BlockSpec((1, L, LANES), lambda b, h: (b, 0, h)),
                  pl.BlockSpec((1, L, LANES), lambda b, h: (b, 0, h)),
                  pl.BlockSpec((1, L, 2 * RET_DV), lambda b, h: (b, 0, h)),
                  st_spec, st_spec,
                  pl.BlockSpec((1, 2 * RET_DV), lambda b, h: (0, h)),
                  pl.BlockSpec((1, L, 2 * RET_DV), lambda b, h: (b, 0, hp + h))],
        out_specs=[pl.BlockSpec((1, L, 2 * RET_DV), lambda b, h: (b, 0, h)), st_spec, st_spec],
        out_shape=[jax.ShapeDtypeStruct((B, L, RET_WIDTH), BF16), st_shape, st_shape],
        scratch_shapes=[pltpu.VMEM((L, 2 * RET_DV), F32), pltpu.VMEM((4, RET_CHUNK, RET_DV), F32)],
        compiler_params=_params(2),
        name="retention",
    )(decay, rq, rk, rv, sf, sb, ret_norm_g[None, :], gates)


def _prep_win_kernel(rope, *refs):
    if rope:
        (x_ref, mod_ref, ng_ref, win_ref, seg_ref, exp_ref, qg_ref, qgs_ref, kg_ref, kgs_ref, c_ref, s_ref,
         q_out, k_out, v_out, gate_out) = refs
    else:
        (x_ref, mod_ref, ng_ref, win_ref, seg_ref, exp_ref, qg_ref, kg_ref,
         q_out, k_out, v_out, gate_out, kst_out, vst_out) = refs
    o = WIN_OFF
    h = _modulated(x_ref, mod_ref, ng_ref)
    proj = jnp.dot(h, win_ref[...], preferred_element_type=F32)
    gate_out[...] = _silu(proj[:, o['g']:o['g'] + WIN_WIDTH]).astype(BF16)

    qg2 = jnp.concatenate([qg_ref[...]] * 2, axis=-1)
    kg2 = jnp.concatenate([kg_ref[...]] * 2, axis=-1)
    if rope:
        c2 = jnp.concatenate([c_ref[...]] * 2, axis=-1)
        s2 = jnp.concatenate([s_ref[...]] * 2, axis=-1)
        q_c, q_s = c2 * qg2, s2 * jnp.concatenate([qgs_ref[...]] * 2, axis=-1)
        k_c, k_s = c2 * kg2, s2 * jnp.concatenate([kgs_ref[...]] * 2, axis=-1)
    qscale = WIN_HEAD_DIM ** -0.5
    for g in range(WIN_WIDTH // 256):
        sl = slice(256 * g, 256 * (g + 1))
        qg = proj[:, sl]
        rn = _seg_rsqrt(qg, seg_ref, WIN_HEAD_DIM) * qscale
        if rope:
            qv = qg * q_c + proj[:, 2560 + 256 * g:2560 + 256 * (g + 1)] * q_s
        else:
            qv = qg * qg2
        q_out[:, sl] = (qv * rn).astype(BF16)
    kraw = proj[:, o['k']:o['k'] + 256]
    rn = _seg_rsqrt(kraw, seg_ref, WIN_HEAD_DIM)
    if rope:
        kn = (kraw * k_c + proj[:, 3584:3584 + 256] * k_s) * rn
    else:
        kn = kraw * kg2 * rn
    v = proj[:, o['v']:o['v'] + 256]
    if not rope:
        kst_out[...] = kn
        vst_out[...] = v
    k_out[...] = jnp.dot(kn.astype(BF16), exp_ref[...], preferred_element_type=F32).astype(BF16)
    v_out[...] = jnp.dot(v.astype(BF16), exp_ref[...], preferred_element_type=F32).astype(BF16)


def _win_expand_matrix():
    e = np.zeros((256, 1024), np.float32)
    for j in range(WIN_KV_HEADS):
        for i in range(WIN_HEAD_DIM):
            e[WIN_HEAD_DIM * j + i, 256 * j + i] = 1.0
            e[WIN_HEAD_DIM * j + i, 256 * j + 192 + i] = 1.0
    return jnp.asarray(e, BF16)


def _prep_win(x, mod, norm_g, p, rope, tokens_per_batch, mod_row, tables, tm):
    T = x.shape[0]
    o = WIN_OFF
    sw64 = _swap_idx(WIN_HEAD_DIM)
    if rope:
        cols = list(range(2560))
        for hh in range(WIN_HEADS):
            cols += list(o['q'] + WIN_HEAD_DIM * hh + sw64)
        for hh in range(WIN_KV_HEADS):
            cols += list(o['k'] + WIN_HEAD_DIM * hh + sw64)
        win = _take_cols(p['w_in'], cols).astype(BF16)
    else:
        win = p['w_in'].astype(BF16)
    seg = _seg_matrix(WIN_HEAD_DIM)
    expand = _win_expand_matrix()
    rep = lambda g: jnp.concatenate([g, g])[None, :]
    blocks_per_batch = tokens_per_batch // tm
    tok = lambda w: pl.BlockSpec((tm, w), lambda i: (i, 0))
    if mod_row is None:
        mod_spec = pl.BlockSpec((1, 1, 3 * D_MODEL), lambda i: (i // blocks_per_batch, 0, 0))
    else:
        mod_spec = pl.BlockSpec((1, 1, 3 * D_MODEL), lambda i: (mod_row, 0, 0))
    vec = lambda a: (a, _const_spec(a.shape))
    ins = [(x, tok(D_MODEL)), (mod, mod_spec), vec(norm_g[None, :]), vec(win), vec(seg), vec(expand),
           vec(rep(p['q_head_g']))]
    if rope:
        ins.append(vec(rep(p['q_head_g'][sw64])))
    ins.append(vec(rep(p['k_head_g'])))
    if rope:
        ins.append(vec(rep(p['k_head_g'][sw64])))
        tab = lambda a: (a, pl.BlockSpec((tm, LANES), lambda i: (i % blocks_per_batch, 0)))
        ins += [tab(tables['c64']), tab(tables['s64'])]
    outs = [(1024, BF16), (1024, BF16), (1024, BF16), (1024, BF16)]
    if not rope:
        outs += [(256, F32), (256, F32)]
    return pl.pallas_call(
        functools.partial(_prep_win_kernel, rope),
        grid=(T // tm,),
        in_specs=[s for _, s in ins],
        out_specs=[tok(w) for w, _ in outs],
        out_shape=[jax.ShapeDtypeStruct((T, w), dt) for w, dt in outs],
        compiler_params=_params(1),
        name="prep_win_rope" if rope else "prep_win",
    )(*[a for a, _ in ins])


def _win_attn_kernel(n_ctx, local, sink_ref, q_ref, k_ref, v_ref, g_ref, o_ref):
    tq = q_ref.shape[1]
    j = pl.program_id(1)
    nb = pl.program_id(2)
    q2 = jnp.concatenate([q_ref[0, :, 0:LANES], q_ref[0, :, LANES:2 * LANES]], axis=0)
    upper = lax.broadcasted_iota(jnp.int32, (2 * tq, 1), 0) >= tq
    if local:
        start = pl.multiple_of(n_ctx + (nb - 1) * tq, tq)
        qi = lax.broadcasted_iota(jnp.int32, (2 * tq, 3 * tq), 0)
        qi = jnp.where(qi >= tq, qi - tq, qi)
        kk = lax.broadcasted_iota(jnp.int32, (2 * tq, 3 * tq), 1)
        kpos = (nb - 1) * tq + kk
        n_local = k_ref.shape[1] - n_ctx - tq
        ok = (jnp.abs(kk - tq - qi) <= WINDOW) & (kpos >= 0) & (kpos < n_local)
    out = None
    dn = (((1,), (1,)), ((), ()))
    for half in range(2):
        sl = slice(LANES * half, LANES * (half + 1))
        sk = jnp.where(upper, sink_ref[4 * j + 2 + half], sink_ref[4 * j + half])
        s_ctx = lax.dot_general(q2, k_ref[0, 0:n_ctx, sl], dn, preferred_element_type=F32)
        m = jnp.maximum(jnp.max(s_ctx, axis=-1, keepdims=True), sk)
        if local:
            s_loc = lax.dot_general(q2, k_ref[0, pl.ds(start, 3 * tq), sl], dn, preferred_element_type=F32)
            s_loc = jnp.where(ok, s_loc, NEG)
            m = jnp.maximum(m, jnp.max(s_loc, axis=-1, keepdims=True))
        e_ctx = jnp.exp(s_ctx - m)
        den = jnp.sum(e_ctx, axis=-1, keepdims=True) + jnp.exp(sk - m)
        acc = jnp.dot(e_ctx.astype(BF16), v_ref[0, 0:n_ctx, sl], preferred_element_type=F32)
        if local:
            e_loc = jnp.exp(s_loc - m)
            den = den + jnp.sum(e_loc, axis=-1, keepdims=True)
            acc = acc + jnp.dot(e_loc.astype(BF16), v_ref[0, pl.ds(start, 3 * tq), sl],
                                preferred_element_type=F32)
        oh = acc / den
        out = oh if out is None else out + oh
    gate = g_ref[0].astype(F32)
    o_ref[0, :, 0:LANES] = (out[0:tq] * gate[:, 0:LANES]).astype(BF16)
    o_ref[0, :, LANES:2 * LANES] = (out[tq:2 * tq] * gate[:, LANES:2 * LANES]).astype(BF16)


def _win_attn(q, k, v, gates, sink, n_ctx, local, tq):
    B, L, _ = q.shape
    Lk = k.shape[1]
    qspec = pl.BlockSpec((1, tq, 2 * LANES), lambda b, h, i: (b, i, h))
    kspec = pl.BlockSpec((1, Lk, 2 * LANES), lambda b, h, i: (b, 0, h))
    return pl.pallas_call(
        functools.partial(_win_attn_kernel, n_ctx, local),
        grid=(B, WIN_KV_HEADS, L // tq),
        in_specs=[pl.BlockSpec(memory_space=pltpu.SMEM), qspec, kspec, kspec, qspec],
        out_specs=qspec,
        out_shape=jax.ShapeDtypeStruct((B, L, WIN_WIDTH), BF16),
        compiler_params=_params(3),
        name="win_attn_local" if local else "win_attn",
    )(sink, q, k, v, gates)


def _out_kernel(n_parts, *refs):
    x_ref, mod_ref = refs[0], refs[1]
    parts = refs[2:2 + 2 * n_parts]
    o_ref = refs[2 + 2 * n_parts]
    y = None
    for i in range(n_parts):
        t = jnp.dot(parts[2 * i][...], parts[2 * i + 1][...], preferred_element_type=F32)
        y = t if y is None else y + t
    gate = mod_ref[0, :, 2 * D_MODEL:3 * D_MODEL]
    o_ref[...] = x_ref[...] + gate * y


def _out_proj(x, mod, parts, tokens_per_batch, mod_row, tm):
    T = x.shape[0]
    blocks_per_batch = tokens_per_batch // tm
    tok = lambda w: pl.BlockSpec((tm, w), lambda i: (i, 0))
    if mod_row is None:
        mod_spec = pl.BlockSpec((1, 1, 3 * D_MODEL), lambda i: (i // blocks_per_batch, 0, 0))
    else:
        mod_spec = pl.BlockSpec((1, 1, 3 * D_MODEL), lambda i: (mod_row, 0, 0))
    in_specs = [tok(D_MODEL), mod_spec]
    args = [x, mod]
    for a, w in parts:
        in_specs += [tok(a.shape[1]), _const_spec(w.shape)]
        args += [a, w]
    return pl.pallas_call(
        functools.partial(_out_kernel, len(parts)),
        grid=(T // tm,),
        in_specs=in_specs,
        out_specs=tok(D_MODEL),
        out_shape=jax.ShapeDtypeStruct((T, D_MODEL), F32),
        compiler_params=_params(1),
        name="out_proj",
    )(*args)


def kernel(x_prompt, x_sample, cache_l0_mla_ckv, cache_l0_mla_krope, state_l0_ret_fwd, state_l0_ret_bwd, cache_l1_win_k, cache_l1_win_v, cache_l2_mla_ckv, cache_l2_mla_krope, state_l2_ret_fwd, state_l2_ret_bwd, cache_l3_win_k, cache_l3_win_v, c, c_ctx, ada_w, ada_b, norm_g, ab_w_in, mla_q_norm_g, mla_w_uq, mla_kv_norm_g, mla_w_ukv, mla_q_head_g, mla_k_head_g, ret_decay, ret_norm_g, ab_w_out, win_w_in, win_q_head_g, win_k_head_g, win_sink, win_w_out):
    BP, LP, D = x_prompt.shape
    BS, LS, _ = x_sample.shape
    P = cache_l0_mla_ckv.shape[1]
    ctx_caches = ((cache_l0_mla_ckv, cache_l0_mla_krope, state_l0_ret_fwd, state_l0_ret_bwd),
                  (cache_l1_win_k, cache_l1_win_v),
                  (cache_l2_mla_ckv, cache_l2_mla_krope, state_l2_ret_fwd, state_l2_ret_bwd),
                  (cache_l3_win_k, cache_l3_win_v))

    cond = jnp.concatenate([c, c_ctx[None, :], jnp.zeros((8 - BS - 1, D), F32)], axis=0)
    mod_all = _ada_mod(cond, ada_w, ada_b).reshape(DEPTH, 8, 1, 3 * D)
    ctx_row = BS

    c32, s32 = _rope_tables(LS, MLA_ROPE)
    c64, s64 = _rope_tables(LS, RET_DK)
    ones = jnp.ones((LS, MLA_NOPE), F32)
    zeros = jnp.zeros((LS, MLA_NOPE), F32)
    z32 = jnp.zeros((LS, LANES - MLA_QK), F32)
    tables = dict(cm=jnp.concatenate([ones, c32, z32], -1), sm=jnp.concatenate([zeros, s32, z32], -1),
                  c64=jnp.concatenate([c64, c64], -1), s64=jnp.concatenate([s64, s64], -1))

    y_p = x_prompt.reshape(BP * LP, D)
    y_s = x_sample.reshape(BS * LS, D)
    tm = 256
    new_state = []
    for l in range(DEPTH):
        i = l // 2
        mod = mod_all[l]
        if l % 2 == 0:
            p = {'w_in': ab_w_in[i], 'q_norm_g': mla_q_norm_g[i], 'w_uq': mla_w_uq[i],
                 'kv_norm_g': mla_kv_norm_g[i], 'w_ukv': mla_w_ukv[i], 'q_head_g': mla_q_head_g[i],
                 'k_head_g': mla_k_head_g[i]}
            w_out = ab_w_out[i].astype(BF16)
            parts_w = (w_out[:MLA_WIDTH], w_out[MLA_WIDTH:])
            ckv_c, krope_c, sf, sb = ctx_caches[l]

            q, k, v, gates, rq, rk, rv, ckv, kr = _prep_ab(y_p, mod, norm_g[l], p, False, LP, ctx_row, None, tm)
            r3 = lambda a: a.reshape(BP, LP, a.shape[-1])
            a_p = _mla_attn(r3(q), r3(k), r3(v), r3(gates), LP, LP)
            zst = jnp.zeros((BP, RET_HEADS, RET_DK, RET_DV), F32)
            r_p, rf, rb = _retention(r3(rq), r3(rk), r3(rv), zst, zst, ret_decay[i], ret_norm_g[i], r3(gates))
            new_state.append((ckv.reshape(BP, LP, KV_LORA),
                              kr[:, MLA_NOPE:MLA_QK].reshape(BP, LP, MLA_ROPE), rf, rb))
            y_p_next = _out_proj(y_p, mod, [(a_p.reshape(BP * LP, -1), parts_w[0]),
                                            (r_p.reshape(BP * LP, -1), parts_w[1])], LP, ctx_row, tm)

            q, k, v, gates, rq, rk, rv = _prep_ab(y_s, mod, norm_g[l], p, True, LS, None, tables, tm)
            _, _, _, wk, wv, _, _, kg, _ = _ab_weights(p['w_in'], p['w_uq'], p['w_ukv'],
                                                        p['q_head_g'], p['k_head_g'], False)
            kr_c = jnp.pad(krope_c.reshape(BS * P, MLA_ROPE), ((0, 0), (MLA_NOPE, LANES - MLA_QK)))
            kc, vc = _prep_ctx(ckv_c.reshape(BS * P, KV_LORA), kr_c, wk, wv, kg, P)
            r3 = lambda a: a.reshape(BS, -1, a.shape[-1])
            k_all = jnp.concatenate([r3(kc), r3(k)], axis=1)
            v_all = jnp.concatenate([r3(vc), r3(v)], axis=1)
            a_s = _mla_attn(r3(q), k_all, v_all, r3(gates), 256, 512)
            r_s, _, _ = _retention(r3(rq), r3(rk), r3(rv), sf, sb, ret_decay[i], ret_norm_g[i], r3(gates))
            y_s_next = _out_proj(y_s, mod, [(a_s.reshape(BS * LS, -1), parts_w[0]),
                                            (r_s.reshape(BS * LS, -1), parts_w[1])], LS, None, tm)
        else:
            p = {'w_in': win_w_in[i], 'q_head_g': win_q_head_g[i], 'k_head_g': win_k_head_g[i]}
            w_out = win_w_out[i].astype(BF16)
            sink = win_sink[i]
            kc, vc = ctx_caches[l]

            q, k, v, gates, kst, vst = _prep_win(y_p, mod, norm_g[l], p, False, LP, ctx_row, None, tm)
            r3 = lambda a: a.reshape(BP, LP, a.shape[-1])
            o_p = _win_attn(r3(q), r3(k), r3(v), r3(gates), sink, LP, False, LP)
            new_state.append((kst.reshape(BP, LP, WIN_KV_HEADS, WIN_HEAD_DIM),
                              vst.reshape(BP, LP, WIN_KV_HEADS, WIN_HEAD_DIM)))
            y_p_next = _out_proj(y_p, mod, [(o_p.reshape(BP * LP, -1), w_out)], LP, ctx_row, tm)

            q, k, v, gates = _prep_win(y_s, mod, norm_g[l], p, True, LS, None, tables, tm)
            r3 = lambda a: a.reshape(BS, -1, a.shape[-1])
            def slots(a):
                a = a.astype(BF16)
                z = jnp.zeros(a.shape[:-1] + (2 * WIN_HEAD_DIM,), BF16)
                return jnp.concatenate([a, z, a], axis=-1).reshape(BS, P, 4 * 2 * LANES)

            pad = jnp.zeros((BS, QBLK_WIN, 1024), BF16)
            k_all = jnp.concatenate([slots(kc), r3(k), pad], axis=1)
            v_all = jnp.concatenate([slots(vc), r3(v), pad], axis=1)
            o_s = _win_attn(r3(q), k_all, v_all, r3(gates), sink, P, True, QBLK_WIN)
            y_s_next = _out_proj(y_s, mod, [(o_s.reshape(BS * LS, -1), w_out)], LS, None, tm)
        y_p, y_s = y_p_next, y_s_next

    (l0_ckv, l0_krope, l0_rf, l0_rb), (l1_k, l1_v), (l2_ckv, l2_krope, l2_rf, l2_rb), (l3_k, l3_v) = new_state
    return (y_p.reshape(BP, LP, D), y_s.reshape(BS, LS, D), l0_ckv, l0_krope, l0_rf, l0_rb, l1_k, l1_v,
            l2_ckv, l2_krope, l2_rf, l2_rb, l3_k, l3_v)


QBLK_WIN = 128
```

```python
import functools

import numpy as np
import jax
import jax.numpy as jnp
from jax import lax
from jax.experimental import pallas as pl
from jax.experimental.pallas import tpu as pltpu

D_MODEL = 1024
DEPTH = 4
GRID_W = 64
ROPE_BASE = 10000.0
EPS = 1e-6

MLA_HEADS = 8
MLA_NOPE = 64
MLA_ROPE = 32
MLA_QK = MLA_NOPE + MLA_ROPE
MLA_V = 64
Q_LORA = 384
KV_LORA = 256
MLA_WIDTH = MLA_HEADS * MLA_V

RET_HEADS = 4
RET_DK = 64
RET_DV = 128
RET_CHUNK = 128
RET_WIDTH = RET_HEADS * RET_DV

WIN_HEADS = 16
WIN_KV_HEADS = 4
WIN_HEAD_DIM = 64
WINDOW = 128
WIN_WIDTH = WIN_HEADS * WIN_HEAD_DIM

LANES = 128
F32 = jnp.float32
BF16 = jnp.bfloat16
NEG = -1e30
LOG2E = 1.4426950408889634
VMEM_LIMIT = 52 * 1024 * 1024

AB_OFF = dict(cq=0, ckv=384, krope=640, ga=672, rq=1184, rk=1440, rv=1696, gb=2208)
WIN_OFF = dict(q=0, k=1024, v=1280, g=1536)


def _params(n_axes):
    return pltpu.CompilerParams(dimension_semantics=("arbitrary",) * n_axes, vmem_limit_bytes=VMEM_LIMIT)


def _const_spec(shape):
    nd = len(shape)
    return pl.BlockSpec(shape, lambda *_: (0,) * nd)


def _swap_idx(dim):
    d2 = dim // 2
    half = d2 // 2
    one = np.concatenate([np.arange(half, d2), np.arange(0, half)])
    return np.concatenate([one, d2 + one])


def _take_cols(w, cols):
    cols = np.asarray(cols, np.int32)
    picked = jnp.take(w, jnp.asarray(np.maximum(cols, 0)), axis=1)
    return jnp.where(jnp.asarray(cols >= 0)[None, :], picked, 0.0)


def _rope_tables(n_tokens, dim):
    t = jnp.arange(n_tokens)
    rows = (t // GRID_W).astype(F32)
    cols = (t % GRID_W).astype(F32)
    half = dim // 4
    inv = jnp.power(jnp.float32(ROPE_BASE), -jnp.arange(half, dtype=F32) / half)

    def one(pos):
        ang = pos[:, None] * inv[None, :]
        c, s = jnp.cos(ang), jnp.sin(ang)
        return jnp.concatenate([c, c], -1), jnp.concatenate([-s, s], -1)

    cr, sr = one(rows)
    cc, sc = one(cols)
    return jnp.concatenate([cr, cc], -1), jnp.concatenate([sr, sc], -1)


def _ada_kernel(cond_ref, w_ref, b_ref, o_ref):
    c = cond_ref[...]
    sc = (c * jax.nn.sigmoid(c)).astype(BF16)
    o_ref[0] = jnp.dot(sc, w_ref[0].astype(BF16), preferred_element_type=F32) + b_ref[0]


def _ada_mod(cond, ada_w, ada_b):
    tn = 768
    n3 = 3 * D_MODEL
    return pl.pallas_call(
        _ada_kernel,
        grid=(DEPTH, n3 // tn),
        in_specs=[pl.BlockSpec((8, D_MODEL), lambda l, j: (0, 0)),
                  pl.BlockSpec((1, D_MODEL, tn), lambda l, j: (l, 0, j)),
                  pl.BlockSpec((1, 1, tn), lambda l, j: (l, 0, j))],
        out_specs=pl.BlockSpec((1, 8, tn), lambda l, j: (l, 0, j)),
        out_shape=jax.ShapeDtypeStruct((DEPTH, 8, n3), F32),
        compiler_params=_params(2),
        name="ada_mod",
    )(cond, ada_w, ada_b.reshape(DEPTH, 1, n3))


def _rms(x, g):
    return x * lax.rsqrt(jnp.mean(x * x, axis=-1, keepdims=True) + EPS) * g


def _modulated(x_ref, mod_ref, ng_ref):
    x = x_ref[...]
    shift = mod_ref[0, :, 0:D_MODEL]
    scale = mod_ref[0, :, D_MODEL:2 * D_MODEL]
    return (_rms(x, ng_ref[...]) * (1.0 + scale) + shift).astype(BF16)


def _seg_rsqrt(x, seg_ref, n_real):
    ssq = jnp.dot((x * x).astype(BF16), seg_ref[...], preferred_element_type=F32)
    return lax.rsqrt(ssq * (1.0 / n_real) + EPS)


def _silu(g):
    return g * jax.nn.sigmoid(g)


def _prep_ab_kernel(rope, *refs):
    if rope:
        (x_ref, mod_ref, ng_ref, win_ref, qng_ref, kvng_ref, wq_ref, wqs_ref, wk_ref, wv_ref, seg_ref,
         qg_ref, qgs_ref, kg_ref, kgs_ref, cm_ref, sm_ref, cr_ref, sr_ref,
         q_out, k_out, v_out, gate_out, rq_out, rk_out, rv_out) = refs
        o_kr, o_g, o_rq, o_rk, o_rv = 640, 896, 1920, 2176, 2944
    else:
        (x_ref, mod_ref, ng_ref, win_ref, qng_ref, kvng_ref, wq_ref, wk_ref, wv_ref, seg_ref,
         qg_ref, kg_ref,
         q_out, k_out, v_out, gate_out, rq_out, rk_out, rv_out, ckv_out, kr_out) = refs
        o_kr, o_g, o_rq, o_rk, o_rv = 640, 768, 1792, 2048, 2304

    h = _modulated(x_ref, mod_ref, ng_ref)
    proj = jnp.dot(h, win_ref[...], preferred_element_type=F32)

    cqn = _rms(proj[:, 0:Q_LORA], qng_ref[...]).astype(BF16)
    ckvn_f = _rms(proj[:, Q_LORA:Q_LORA + KV_LORA], kvng_ref[...])
    ckvn = ckvn_f.astype(BF16)
    kr = proj[:, o_kr:o_kr + LANES]

    gate_out[...] = _silu(proj[:, o_g:o_g + 2 * MLA_WIDTH]).astype(BF16)
    rv_out[...] = proj[:, o_rv:o_rv + RET_WIDTH].astype(BF16)

    qm = jnp.dot(cqn, wq_ref[...], preferred_element_type=F32)
    kn = jnp.dot(ckvn, wk_ref[...], preferred_element_type=F32)
    v_out[...] = jnp.dot(ckvn, wv_ref[...], preferred_element_type=F32).astype(BF16)

    qscale = MLA_QK ** -0.5 * LOG2E
    if rope:
        qs = jnp.dot(cqn, wqs_ref[...], preferred_element_type=F32)
        krs = proj[:, 768:768 + LANES]
        cm, sm = cm_ref[...], sm_ref[...]
        q_c = cm * qg_ref[...]
        q_s = sm * qgs_ref[...]
        kr_rot = kr * (cm * kg_ref[...]) + krs * (sm * kgs_ref[...])
        cr, sr = cr_ref[...], sr_ref[...]
        rq = proj[:, o_rq:o_rq + 256]
        rk = proj[:, o_rk:o_rk + 256]
        rqs = proj[:, 2432:2432 + 256]
        rks = proj[:, 2688:2688 + 256]
        for c in range(2):
            sl = slice(LANES * c, LANES * (c + 1))
            rq_out[:, sl] = ((rq[:, sl] * cr + rqs[:, sl] * sr) * (RET_DK ** -0.5)).astype(BF16)
            rk_out[:, sl] = (rk[:, sl] * cr + rks[:, sl] * sr).astype(BF16)
    else:
        q_c = jnp.broadcast_to(qg_ref[...], (qm.shape[0], LANES))
        kr_rot = kr * kg_ref[...]
        rq_out[...] = (proj[:, o_rq:o_rq + 256] * (RET_DK ** -0.5)).astype(BF16)
        rk_out[...] = proj[:, o_rk:o_rk + 256].astype(BF16)
        ckv_out[...] = ckvn_f
        kr_out[...] = kr

    kg = kg_ref[...]
    q_c2 = jnp.concatenate([q_c, q_c], axis=-1)
    kr2 = jnp.concatenate([kr, kr], axis=-1)
    kr_rot2 = jnp.concatenate([kr_rot, kr_rot], axis=-1)
    kg2 = jnp.concatenate([kg, kg], axis=-1)
    if rope:
        q_s2 = jnp.concatenate([q_s, q_s], axis=-1)
    for g in range(MLA_HEADS // 2):
        sl = slice(2 * LANES * g, 2 * LANES * (g + 1))
        qg = qm[:, sl]
        rq_n = _seg_rsqrt(qg, seg_ref, MLA_QK)
        qv = qg * q_c2
        if rope:
            qv = qv + qs[:, sl] * q_s2
        q_out[:, sl] = (qv * (rq_n * qscale)).astype(BF16)
        kgp = kn[:, sl]
        rk_n = _seg_rsqrt(kgp + kr2, seg_ref, MLA_QK)
        k_out[:, sl] = ((kgp * kg2 + kr_rot2) * rk_n).astype(BF16)


def _ab_weights(w_in, w_uq, w_ukv, q_head_g, k_head_g, rope):
    sw32 = _swap_idx(MLA_ROPE)
    sw64 = _swap_idx(RET_DK)
    z = lambda n: [-1] * n
    o = AB_OFF
    cols = list(range(o['cq'], o['cq'] + Q_LORA)) + list(range(o['ckv'], o['ckv'] + KV_LORA))
    cols += z(64) + list(range(o['krope'], o['krope'] + MLA_ROPE)) + z(32)
    if rope:
        cols += z(64) + list(o['krope'] + sw32) + z(32)
    cols += list(range(o['ga'], o['ga'] + MLA_WIDTH)) + list(range(o['gb'], o['gb'] + RET_WIDTH))
    cols += list(range(o['rq'], o['rq'] + 256)) + list(range(o['rk'], o['rk'] + 256))
    if rope:
        for base in (o['rq'], o['rk']):
            for hh in range(RET_HEADS):
                cols += list(base + RET_DK * hh + sw64)
    cols += list(range(o['rv'], o['rv'] + RET_WIDTH))
    win = _take_cols(w_in, cols).astype(BF16)

    qc, qsc, kc, vc = [], [], [], []
    for hh in range(MLA_HEADS):
        qb = MLA_QK * hh
        qc += list(range(qb, qb + MLA_QK)) + z(32)
        qsc += z(64) + list(qb + MLA_NOPE + sw32) + z(32)
        kb = (MLA_NOPE + MLA_V) * hh
        kc += list(range(kb, kb + MLA_NOPE)) + z(64)
        vcols = list(range(kb + MLA_NOPE, kb + MLA_NOPE + MLA_V))
        vc += (vcols + z(64)) if hh % 2 == 0 else (z(64) + vcols)
    wq = _take_cols(w_uq, qc).astype(BF16)
    wqs = _take_cols(w_uq, qsc).astype(BF16) if rope else None
    wk = _take_cols(w_ukv, kc).astype(BF16)
    wv = _take_cols(w_ukv, vc).astype(BF16)

    pad = lambda g: jnp.concatenate([g, jnp.zeros((LANES - MLA_QK,), F32)])[None, :]
    gsw = lambda g: jnp.concatenate([jnp.zeros((MLA_NOPE,), F32), g[MLA_NOPE + sw32],
                                     jnp.zeros((LANES - MLA_QK,), F32)])[None, :]
    qg, kg = pad(q_head_g), pad(k_head_g)
    qgs, kgs = (gsw(q_head_g), gsw(k_head_g)) if rope else (None, None)
    return win, wq, wqs, wk, wv, qg, qgs, kg, kgs


def _seg_matrix(width):
    idx = np.arange(2 * LANES) // width
    return jnp.asarray((idx[:, None] == idx[None, :]).astype(np.float32), BF16)


def _prep_ab(x, mod, norm_g, p, rope, tokens_per_batch, mod_row, tables, tm):
    T = x.shape[0]
    win, wq, wqs, wk, wv, qg, qgs, kg, kgs = _ab_weights(
        p['w_in'], p['w_uq'], p['w_ukv'], p['q_head_g'], p['k_head_g'], rope)
    seg = _seg_matrix(LANES)
    nw = win.shape[1]
    blocks_per_batch = tokens_per_batch // tm
    tok = lambda w: pl.BlockSpec((tm, w), lambda i: (i, 0))
    if mod_row is None:
        mod_spec = pl.BlockSpec((1, 1, 3 * D_MODEL), lambda i: (i // blocks_per_batch, 0, 0))
    else:
        mod_spec = pl.BlockSpec((1, 1, 3 * D_MODEL), lambda i: (mod_row, 0, 0))
    vec = lambda a: (a, _const_spec(a.shape))
    ins = [(x, tok(D_MODEL)), (mod, mod_spec), vec(norm_g[None, :]), vec(win),
           vec(p['q_norm_g'][None, :]), vec(p['kv_norm_g'][None, :]), vec(wq)]
    if rope:
        ins.append(vec(wqs))
    ins += [vec(wk), vec(wv), vec(seg), vec(qg)]
    if rope:
        ins.append(vec(qgs))
    ins.append(vec(kg))
    if rope:
        ins.append(vec(kgs))
        nblk = tokens_per_batch // tm
        tab = lambda a: (a, pl.BlockSpec((tm, LANES), lambda i: (i % nblk, 0)))
        ins += [tab(tables['cm']), tab(tables['sm']), tab(tables['c64']), tab(tables['s64'])]
    outs = [(1024, BF16), (1024, BF16), (1024, BF16), (1024, BF16), (256, BF16), (256, BF16), (512, BF16)]
    if not rope:
        outs += [(KV_LORA, F32), (LANES, F32)]
    return pl.pallas_call(
        functools.partial(_prep_ab_kernel, rope),
        grid=(T // tm,),
        in_specs=[s for _, s in ins],
        out_specs=[tok(w) for w, _ in outs],
        out_shape=[jax.ShapeDtypeStruct((T, w), dt) for w, dt in outs],
        compiler_params=_params(1),
        name="prep_ab_rope" if rope else "prep_ab",
    )(*[a for a, _ in ins])


def _prep_ctx_kernel(ckv_ref, kr_ref, wk_ref, wv_ref, seg_ref, kg_ref, k_out, v_out):
    ckv = ckv_ref[...].astype(BF16)
    kn = jnp.dot(ckv, wk_ref[...], preferred_element_type=F32)
    v_out[...] = jnp.dot(ckv, wv_ref[...], preferred_element_type=F32).astype(BF16)
    kr = kr_ref[...]
    kg = kg_ref[...]
    kr2 = jnp.concatenate([kr, kr], axis=-1)
    kg2 = jnp.concatenate([kg, kg], axis=-1)
    for g in range(MLA_HEADS // 2):
        sl = slice(2 * LANES * g, 2 * LANES * (g + 1))
        kraw = kn[:, sl] + kr2
        k_out[:, sl] = (kraw * kg2 * _seg_rsqrt(kraw, seg_ref, MLA_QK)).astype(BF16)


def _prep_ctx(ckv_c, kr_c, wk, wv, kg, tm):
    T = ckv_c.shape[0]
    seg = _seg_matrix(LANES)
    tok = lambda w: pl.BlockSpec((tm, w), lambda i: (i, 0))
    return pl.pallas_call(
        _prep_ctx_kernel,
        grid=(T // tm,),
        in_specs=[tok(KV_LORA), tok(LANES), _const_spec(wk.shape), _const_spec(wv.shape),
                  _const_spec(seg.shape), _const_spec(kg.shape)],
        out_specs=[tok(1024), tok(1024)],
        out_shape=[jax.ShapeDtypeStruct((T, 1024), BF16)] * 2,
        compiler_params=_params(1),
        name="prep_ctx",
    )(ckv_c, kr_c, wk, wv, seg, kg)


def _mla_attn_kernel(q_ref, k_ref, v_ref, g_ref, o_ref):
    out = None
    for j in range(2):
        sl = slice(LANES * j, LANES * (j + 1))
        s = lax.dot_general(q_ref[0, :, sl], k_ref[0, :, sl], (((1,), (1,)), ((), ())),
                            preferred_element_type=F32)
        p = jnp.exp2(s - jnp.max(s, axis=-1, keepdims=True))
        l = jnp.sum(p, axis=-1, keepdims=True)
        oj = jnp.dot(p.astype(BF16), v_ref[0, :, sl], preferred_element_type=F32) / l
        out = oj if out is None else out + oj
    o_ref[0] = (out * g_ref[0].astype(F32)).astype(BF16)


def _mla_attn(q, k, v, gates, tq):
    B, L, _ = q.shape
    Lk = k.shape[1]
    hp = MLA_HEADS // 2
    return pl.pallas_call(
        _mla_attn_kernel,
        grid=(B, hp, L // tq),
        in_specs=[pl.BlockSpec((1, tq, 2 * LANES), lambda b, h, i: (b, i, h)),
                  pl.BlockSpec((1, Lk, 2 * LANES), lambda b, h, i: (b, 0, h)),
                  pl.BlockSpec((1, Lk, 2 * LANES), lambda b, h, i: (b, 0, h)),
                  pl.BlockSpec((1, tq, LANES), lambda b, h, i: (b, i, h))],
        out_specs=pl.BlockSpec((1, tq, LANES), lambda b, h, i: (b, i, h)),
        out_shape=jax.ShapeDtypeStruct((B, L, MLA_WIDTH), BF16),
        compiler_params=_params(3),
        name="mla_attn",
    )(q, k, v, gates)


def _ret_kernel(n_chunks, decay_ref, q_ref, k_ref, v_ref, sf_ref, sb_ref, gn_ref, gate_ref,
                o_ref, rf_ref, rb_ref, acc_ref, st_ref):
    C = RET_CHUNK
    pair = pl.program_id(1)
    row = lax.broadcasted_iota(jnp.int32, (C, C), 0).astype(F32)
    col = lax.broadcasted_iota(jnp.int32, (C, C), 1).astype(F32)
    lane_head = lax.broadcasted_iota(jnp.int32, (C, LANES), 1) // RET_DK
    rel = row - col

    tabs = []
    for hh in range(2):
        per_dir = []
        for d in range(2):
            lg = -jnp.exp(jnp.full((C, C), decay_ref[d, 2 * pair + hh], F32))
            if d == 0:
                dmask = jnp.where(rel >= 0, jnp.exp(jnp.maximum(rel, 0.0) * lg), 0.0)
                k_w = jnp.exp((C - 1.0 - row) * lg)
                q_w = jnp.exp((row + 1.0) * lg)
            else:
                dmask = jnp.where(rel <= 0, jnp.exp(jnp.maximum(-rel, 0.0) * lg), 0.0)
                k_w = jnp.exp(row * lg)
                q_w = jnp.exp((C - row) * lg)
            per_dir.append((dmask, k_w, q_w, jnp.exp(C * lg)))
            st_ref[2 * hh + d] = jnp.zeros((C, RET_DV), F32)
        tabs.append(per_dir)
        st_ref[2 * hh + 0, RET_DK * hh:RET_DK * (hh + 1), :] = sf_ref[0, hh]
        st_ref[2 * hh + 1, RET_DK * hh:RET_DK * (hh + 1), :] = sb_ref[0, hh]
    acc_ref[...] = jnp.zeros_like(acc_ref)

    def step(n, _):
        for d in range(2):
            cidx = n if d == 0 else n_chunks - 1 - n
            r0 = pl.multiple_of(cidx * C, C)
            qc = q_ref[0, pl.ds(r0, C), :].astype(F32)
            kc = k_ref[0, pl.ds(r0, C), :]
            kf = kc.astype(F32)
            for hh in range(2):
                dmask, k_w, q_w, g_chunk = tabs[hh][d]
                vc = v_ref[0, pl.ds(r0, C), RET_DV * hh:RET_DV * (hh + 1)]
                qm = jnp.where(lane_head == hh, qc, 0.0)
                s = lax.dot_general(qm.astype(BF16), kc, (((1,), (1,)), ((), ())),
                                    preferred_element_type=F32) * dmask
                intra = jnp.dot(s.astype(BF16), vc, preferred_element_type=F32)
                state = st_ref[2 * hh + d]
                inter = jnp.dot((qm * q_w).astype(BF16), state.astype(BF16), preferred_element_type=F32)
                u = jnp.dot((kf * k_w).T.astype(BF16), vc, preferred_element_type=F32)
                st_ref[2 * hh + d] = g_chunk * state + u
                acc_ref[pl.ds(r0, C), RET_DV * hh:RET_DV * (hh + 1)] += intra + inter
        return 0

    lax.fori_loop(0, n_chunks, step, 0)

    for hh in range(2):
        rf_ref[0, hh] = st_ref[2 * hh + 0, RET_DK * hh:RET_DK * (hh + 1), :]
        rb_ref[0, hh] = st_ref[2 * hh + 1, RET_DK * hh:RET_DK * (hh + 1), :]

    def norm(n, _):
        r0 = pl.multiple_of(n * C, C)
        for hh in range(2):
            sl = slice(RET_DV * hh, RET_DV * (hh + 1))
            y = acc_ref[pl.ds(r0, C), sl]
            mu = jnp.mean(y, axis=-1, keepdims=True)
            yc = y - mu
            var = jnp.mean(yc * yc, axis=-1, keepdims=True)
            out = yc * lax.rsqrt(var + EPS) * gn_ref[:, sl]
            o_ref[0, pl.ds(r0, C), sl] = (out * gate_ref[0, pl.ds(r0, C), sl].astype(F32)).astype(BF16)
        return 0

    lax.fori_loop(0, n_chunks, norm, 0)


def _retention(rq, rk, rv, sf, sb, decay, ret_norm_g, gates):
    B, L, _ = rq.shape
    n_chunks = L // RET_CHUNK
    hp = RET_HEADS // 2
    st_spec = pl.BlockSpec((1, 2, RET_DK, RET_DV), lambda b, h: (b, h, 0, 0))
    st_shape = jax.ShapeDtypeStruct((B, RET_HEADS, RET_DK, RET_DV), F32)
    return pl.pallas_call(
        functools.partial(_ret_kernel, n_chunks),
        grid=(B, hp),
        in_specs=[pl.BlockSpec(memory_space=pltpu.SMEM),
                  pl.BlockSpec((1, L, LANES), lambda b, h: (b, 0, h)),
                  pl.BlockSpec((1, L, LANES), lambda b, h: (b, 0, h)),
                  pl.BlockSpec((1, L, 2 * RET_DV), lambda b, h: (b, 0, h)),
                  st_spec, st_spec,
                  pl.BlockSpec((1, 2 * RET_DV), lambda b, h: (0, h)),
                  pl.BlockSpec((1, L, 2 * RET_DV), lambda b, h: (b, 0, hp + h))],
        out_specs=[pl.BlockSpec((1, L, 2 * RET_DV), lambda b, h: (b, 0, h)), st_spec, st_spec],
        out_shape=[jax.ShapeDtypeStruct((B, L, RET_WIDTH), BF16), st_shape, st_shape],
        scratch_shapes=[pltpu.VMEM((L, 2 * RET_DV), F32), pltpu.VMEM((4, RET_CHUNK, RET_DV), F32)],
        compiler_params=_params(2),
        name="retention",
    )(decay, rq, rk, rv, sf, sb, ret_norm_g[None, :], gates)


def _prep_win_kernel(rope, *refs):
    if rope:
        (x_ref, mod_ref, ng_ref, win_ref, seg_ref, exp_ref, qg_ref, qgs_ref, kg_ref, kgs_ref, c_ref, s_ref,
         q_out, k_out, v_out, gate_out) = refs
    else:
        (x_ref, mod_ref, ng_ref, win_ref, seg_ref, exp_ref, qg_ref, kg_ref,
         q_out, k_out, v_out, gate_out, kst_out, vst_out) = refs
    o = WIN_OFF
    h = _modulated(x_ref, mod_ref, ng_ref)
    proj = jnp.dot(h, win_ref[...], preferred_element_type=F32)
    gate_out[...] = _silu(proj[:, o['g']:o['g'] + WIN_WIDTH]).astype(BF16)

    qg2 = jnp.concatenate([qg_ref[...]] * 2, axis=-1)
    kg2 = jnp.concatenate([kg_ref[...]] * 2, axis=-1)
    if rope:
        c2 = jnp.concatenate([c_ref[...]] * 2, axis=-1)
        s2 = jnp.concatenate([s_ref[...]] * 2, axis=-1)
        q_c, q_s = c2 * qg2, s2 * jnp.concatenate([qgs_ref[...]] * 2, axis=-1)
        k_c, k_s = c2 * kg2, s2 * jnp.concatenate([kgs_ref[...]] * 2, axis=-1)
    qscale = WIN_HEAD_DIM ** -0.5 * LOG2E
    for g in range(WIN_WIDTH // 256):
        sl = slice(256 * g, 256 * (g + 1))
        qg = proj[:, sl]
        rn = _seg_rsqrt(qg, seg_ref, WIN_HEAD_DIM) * qscale
        if rope:
            qv = qg * q_c + proj[:, 2560 + 256 * g:2560 + 256 * (g + 1)] * q_s
        else:
            qv = qg * qg2
        q_out[:, sl] = (qv * rn).astype(BF16)
    kraw = proj[:, o['k']:o['k'] + 256]
    rn = _seg_rsqrt(kraw, seg_ref, WIN_HEAD_DIM)
    if rope:
        kn = (kraw * k_c + proj[:, 3584:3584 + 256] * k_s) * rn
    else:
        kn = kraw * kg2 * rn
    v = proj[:, o['v']:o['v'] + 256]
    if not rope:
        kst_out[...] = kn
        vst_out[...] = v
    k_out[...] = jnp.dot(kn.astype(BF16), exp_ref[...], preferred_element_type=F32).astype(BF16)
    v_out[...] = jnp.dot(v.astype(BF16), exp_ref[...], preferred_element_type=F32).astype(BF16)


def _win_expand_matrix():
    e = np.zeros((256, 1024), np.float32)
    for j in range(WIN_KV_HEADS):
        for i in range(WIN_HEAD_DIM):
            e[WIN_HEAD_DIM * j + i, 256 * j + i] = 1.0
            e[WIN_HEAD_DIM * j + i, 256 * j + 192 + i] = 1.0
    return jnp.asarray(e, BF16)


def _prep_win(x, mod, norm_g, p, rope, tokens_per_batch, mod_row, tables, tm):
    T = x.shape[0]
    o = WIN_OFF
    sw64 = _swap_idx(WIN_HEAD_DIM)
    if rope:
        cols = list(range(2560))
        for hh in range(WIN_HEADS):
            cols += list(o['q'] + WIN_HEAD_DIM * hh + sw64)
        for hh in range(WIN_KV_HEADS):
            cols += list(o['k'] + WIN_HEAD_DIM * hh + sw64)
        win = _take_cols(p['w_in'], cols).astype(BF16)
    else:
        win = p['w_in'].astype(BF16)
    seg = _seg_matrix(WIN_HEAD_DIM)
    expand = _win_expand_matrix()
    rep = lambda g: jnp.concatenate([g, g])[None, :]
    blocks_per_batch = tokens_per_batch // tm
    tok = lambda w: pl.BlockSpec((tm, w), lambda i: (i, 0))
    if mod_row is None:
        mod_spec = pl.BlockSpec((1, 1, 3 * D_MODEL), lambda i: (i // blocks_per_batch, 0, 0))
    else:
        mod_spec = pl.BlockSpec((1, 1, 3 * D_MODEL), lambda i: (mod_row, 0, 0))
    vec = lambda a: (a, _const_spec(a.shape))
    ins = [(x, tok(D_MODEL)), (mod, mod_spec), vec(norm_g[None, :]), vec(win), vec(seg), vec(expand),
           vec(rep(p['q_head_g']))]
    if rope:
        ins.append(vec(rep(p['q_head_g'][sw64])))
    ins.append(vec(rep(p['k_head_g'])))
    if rope:
        ins.append(vec(rep(p['k_head_g'][sw64])))
        tab = lambda a: (a, pl.BlockSpec((tm, LANES), lambda i: (i % blocks_per_batch, 0)))
        ins += [tab(tables['c64']), tab(tables['s64'])]
    outs = [(1024, BF16), (1024, BF16), (1024, BF16), (1024, BF16)]
    if not rope:
        outs += [(256, F32), (256, F32)]
    return pl.pallas_call(
        functools.partial(_prep_win_kernel, rope),
        grid=(T // tm,),
        in_specs=[s for _, s in ins],
        out_specs=[tok(w) for w, _ in outs],
        out_shape=[jax.ShapeDtypeStruct((T, w), dt) for w, dt in outs],
        compiler_params=_params(1),
        name="prep_win_rope" if rope else "prep_win",
    )(*[a for a, _ in ins])


def _win_attn_kernel(n_ctx, local, *refs):
    if local:
        sink_ref, q_ref, k_ref, v_ref, g_ref, bias_ref, o_ref = refs
    else:
        sink_ref, q_ref, k_ref, v_ref, g_ref, o_ref = refs
    tq = q_ref.shape[1]
    j = pl.program_id(1)
    q2 = jnp.concatenate([q_ref[0, :, 0:LANES], q_ref[0, :, LANES:2 * LANES]], axis=0)
    upper = lax.broadcasted_iota(jnp.int32, (2 * tq, 1), 0) >= tq
    if local:
        span = bias_ref.shape[2]
        start = pl.multiple_of(n_ctx - WINDOW + pl.program_id(2) * tq, WINDOW)
    out = None
    dn = (((1,), (1,)), ((), ()))
    for half in range(2):
        sl = slice(LANES * half, LANES * (half + 1))
        sk = jnp.where(upper, sink_ref[4 * j + 2 + half], sink_ref[4 * j + half]) * LOG2E
        s_ctx = lax.dot_general(q2, k_ref[0, 0:n_ctx, sl], dn, preferred_element_type=F32)
        m = jnp.maximum(jnp.max(s_ctx, axis=-1, keepdims=True), sk)
        if local:
            s_loc = lax.dot_general(q2, k_ref[0, pl.ds(start, span), sl], dn,
                                    preferred_element_type=F32) + bias_ref[0]
            m = jnp.maximum(m, jnp.max(s_loc, axis=-1, keepdims=True))
        e_ctx = jnp.exp2(s_ctx - m)
        den = jnp.sum(e_ctx, axis=-1, keepdims=True) + jnp.exp2(sk - m)
        acc = jnp.dot(e_ctx.astype(BF16), v_ref[0, 0:n_ctx, sl], preferred_element_type=F32)
        if local:
            e_loc = jnp.exp2(s_loc - m)
            den = den + jnp.sum(e_loc, axis=-1, keepdims=True)
            acc = acc + jnp.dot(e_loc.astype(BF16), v_ref[0, pl.ds(start, span), sl],
                                preferred_element_type=F32)
        oh = acc / den
        out = oh if out is None else out + oh
    gate = g_ref[0].astype(F32)
    o_ref[0, :, 0:LANES] = (out[0:tq] * gate[:, 0:LANES]).astype(BF16)
    o_ref[0, :, LANES:2 * LANES] = (out[tq:2 * tq] * gate[:, LANES:2 * LANES]).astype(BF16)


def _window_bias(tq):
    span = tq + 2 * WINDOW
    qi = jnp.arange(2 * tq)[:, None] % tq
    kk = jnp.arange(span)[None, :]
    in_window = jnp.abs(kk - WINDOW - qi) <= WINDOW
    cases = [in_window & (kk >= WINDOW), in_window, in_window & (kk < tq + WINDOW)]
    return jnp.where(jnp.stack(cases), 0.0, NEG).astype(F32)


def _win_attn(q, k, v, gates, sink, n_ctx, local, tq):
    B, L, _ = q.shape
    Lk = k.shape[1]
    nb = L // tq
    qspec = pl.BlockSpec((1, tq, 2 * LANES), lambda b, h, i: (b, i, h))
    kspec = pl.BlockSpec((1, Lk, 2 * LANES), lambda b, h, i: (b, 0, h))
    in_specs = [pl.BlockSpec(memory_space=pltpu.SMEM), qspec, kspec, kspec, qspec]
    args = [sink, q, k, v, gates]
    if local:
        assert nb >= 2 and Lk == n_ctx + L + WINDOW
        bias = _window_bias(tq)
        in_specs.append(pl.BlockSpec(
            (1,) + bias.shape[1:],
            lambda b, h, i: (jnp.where(i == 0, 0, jnp.where(i == nb - 1, 2, 1)), 0, 0)))
        args.append(bias)
    return pl.pallas_call(
        functools.partial(_win_attn_kernel, n_ctx, local),
        grid=(B, WIN_KV_HEADS, nb),
        in_specs=in_specs,
        out_specs=qspec,
        out_shape=jax.ShapeDtypeStruct((B, L, WIN_WIDTH), BF16),
        compiler_params=_params(3),
        name="win_attn_local" if local else "win_attn",
    )(*args)


def _out_kernel(n_parts, *refs):
    x_ref, mod_ref = refs[0], refs[1]
    parts = refs[2:2 + 2 * n_parts]
    o_ref = refs[2 + 2 * n_parts]
    y = None
    for i in range(n_parts):
        t = jnp.dot(parts[2 * i][...], parts[2 * i + 1][...], preferred_element_type=F32)
        y = t if y is None else y + t
    gate = mod_ref[0, :, 2 * D_MODEL:3 * D_MODEL]
    o_ref[...] = x_ref[...] + gate * y


def _out_proj(x, mod, parts, tokens_per_batch, mod_row, tm):
    T = x.shape[0]
    blocks_per_batch = tokens_per_batch // tm
    tok = lambda w: pl.BlockSpec((tm, w), lambda i: (i, 0))
    if mod_row is None:
        mod_spec = pl.BlockSpec((1, 1, 3 * D_MODEL), lambda i: (i // blocks_per_batch, 0, 0))
    else:
        mod_spec = pl.BlockSpec((1, 1, 3 * D_MODEL), lambda i: (mod_row, 0, 0))
    in_specs = [tok(D_MODEL), mod_spec]
    args = [x, mod]
    for a, w in parts:
        in_specs += [tok(a.shape[1]), _const_spec(w.shape)]
        args += [a, w]
    return pl.pallas_call(
        functools.partial(_out_kernel, len(parts)),
        grid=(T // tm,),
        in_specs=in_specs,
        out_specs=tok(D_MODEL),
        out_shape=jax.ShapeDtypeStruct((T, D_MODEL), F32),
        compiler_params=_params(1),
        name="out_proj",
    )(*args)


def kernel(x_prompt, x_sample, cache_l0_mla_ckv, cache_l0_mla_krope, state_l0_ret_fwd, state_l0_ret_bwd, cache_l1_win_k, cache_l1_win_v, cache_l2_mla_ckv, cache_l2_mla_krope, state_l2_ret_fwd, state_l2_ret_bwd, cache_l3_win_k, cache_l3_win_v, c, c_ctx, ada_w, ada_b, norm_g, ab_w_in, mla_q_norm_g, mla_w_uq, mla_kv_norm_g, mla_w_ukv, mla_q_head_g, mla_k_head_g, ret_decay, ret_norm_g, ab_w_out, win_w_in, win_q_head_g, win_k_head_g, win_sink, win_w_out):
    BP, LP, D = x_prompt.shape
    BS, LS, _ = x_sample.shape
    P = cache_l0_mla_ckv.shape[1]
    ctx_caches = ((cache_l0_mla_ckv, cache_l0_mla_krope, state_l0_ret_fwd, state_l0_ret_bwd),
                  (cache_l1_win_k, cache_l1_win_v),
                  (cache_l2_mla_ckv, cache_l2_mla_krope, state_l2_ret_fwd, state_l2_ret_bwd),
                  (cache_l3_win_k, cache_l3_win_v))

    cond = jnp.concatenate([c, c_ctx[None, :], jnp.zeros((8 - BS - 1, D), F32)], axis=0)
    mod_all = _ada_mod(cond, ada_w, ada_b).reshape(DEPTH, 8, 1, 3 * D)
    ctx_row = BS

    c32, s32 = _rope_tables(LS, MLA_ROPE)
    c64, s64 = _rope_tables(LS, RET_DK)
    ones = jnp.ones((LS, MLA_NOPE), F32)
    zeros = jnp.zeros((LS, MLA_NOPE), F32)
    z32 = jnp.zeros((LS, LANES - MLA_QK), F32)
    tables = dict(cm=jnp.concatenate([ones, c32, z32], -1), sm=jnp.concatenate([zeros, s32, z32], -1),
                  c64=jnp.concatenate([c64, c64], -1), s64=jnp.concatenate([s64, s64], -1))

    y_p = x_prompt.reshape(BP * LP, D)
    y_s = x_sample.reshape(BS * LS, D)
    tm = 256
    new_state = []
    for l in range(DEPTH):
        i = l // 2
        mod = mod_all[l]
        if l % 2 == 0:
            p = {'w_in': ab_w_in[i], 'q_norm_g': mla_q_norm_g[i], 'w_uq': mla_w_uq[i],
                 'kv_norm_g': mla_kv_norm_g[i], 'w_ukv': mla_w_ukv[i], 'q_head_g': mla_q_head_g[i],
                 'k_head_g': mla_k_head_g[i]}
            w_out = ab_w_out[i].astype(BF16)
            parts_w = (w_out[:MLA_WIDTH], w_out[MLA_WIDTH:])
            ckv_c, krope_c, sf, sb = ctx_caches[l]

            q, k, v, gates, rq, rk, rv, ckv, kr = _prep_ab(y_p, mod, norm_g[l], p, False, LP, ctx_row, None, tm)
            r3 = lambda a: a.reshape(BP, LP, a.shape[-1])
            a_p = _mla_attn(r3(q), r3(k), r3(v), r3(gates), LP)
            zst = jnp.zeros((BP, RET_HEADS, RET_DK, RET_DV), F32)
            r_p, rf, rb = _retention(r3(rq), r3(rk), r3(rv), zst, zst, ret_decay[i], ret_norm_g[i], r3(gates))
            new_state.append((ckv.reshape(BP, LP, KV_LORA),
                              kr[:, MLA_NOPE:MLA_QK].reshape(BP, LP, MLA_ROPE), rf, rb))
            y_p_next = _out_proj(y_p, mod, [(a_p.reshape(BP * LP, -1), parts_w[0]),
                                            (r_p.reshape(BP * LP, -1), parts_w[1])], LP, ctx_row, tm)

            q, k, v, gates, rq, rk, rv = _prep_ab(y_s, mod, norm_g[l], p, True, LS, None, tables, tm)
            _, _, _, wk, wv, _, _, kg, _ = _ab_weights(p['w_in'], p['w_uq'], p['w_ukv'],
                                                        p['q_head_g'], p['k_head_g'], False)
            kr_c = jnp.pad(krope_c.reshape(BS * P, MLA_ROPE), ((0, 0), (MLA_NOPE, LANES - MLA_QK)))
            kc, vc = _prep_ctx(ckv_c.reshape(BS * P, KV_LORA), kr_c, wk, wv, kg, P)
            r3 = lambda a: a.reshape(BS, -1, a.shape[-1])
            k_all = jnp.concatenate([r3(kc), r3(k)], axis=1)
            v_all = jnp.concatenate([r3(vc), r3(v)], axis=1)
            a_s = _mla_attn(r3(q), k_all, v_all, r3(gates), 256)
            r_s, _, _ = _retention(r3(rq), r3(rk), r3(rv), sf, sb, ret_decay[i], ret_norm_g[i], r3(gates))
            y_s_next = _out_proj(y_s, mod, [(a_s.reshape(BS * LS, -1), parts_w[0]),
                                            (r_s.reshape(BS * LS, -1), parts_w[1])], LS, None, tm)
        else:
            p = {'w_in': win_w_in[i], 'q_head_g': win_q_head_g[i], 'k_head_g': win_k_head_g[i]}
            w_out = win_w_out[i].astype(BF16)
            sink = win_sink[i]
            kc, vc = ctx_caches[l]

            q, k, v, gates, kst, vst = _prep_win(y_p, mod, norm_g[l], p, False, LP, ctx_row, None, tm)
            r3 = lambda a: a.reshape(BP, LP, a.shape[-1])
            o_p = _win_attn(r3(q), r3(k), r3(v), r3(gates), sink, LP, False, LP)
            new_state.append((kst.reshape(BP, LP, WIN_KV_HEADS, WIN_HEAD_DIM),
                              vst.reshape(BP, LP, WIN_KV_HEADS, WIN_HEAD_DIM)))
            y_p_next = _out_proj(y_p, mod, [(o_p.reshape(BP * LP, -1), w_out)], LP, ctx_row, tm)

            q, k, v, gates = _prep_win(y_s, mod, norm_g[l], p, True, LS, None, tables, tm)
            r3 = lambda a: a.reshape(BS, -1, a.shape[-1])
            def slots(a):
                a = a.astype(BF16)
                z = jnp.zeros(a.shape[:-1] + (2 * WIN_HEAD_DIM,), BF16)
                return jnp.concatenate([a, z, a], axis=-1).reshape(BS, P, 4 * 2 * LANES)

            pad = jnp.zeros((BS, WINDOW, 1024), BF16)
            k_all = jnp.concatenate([slots(kc), r3(k), pad], axis=1)
            v_all = jnp.concatenate([slots(vc), r3(v), pad], axis=1)
            o_s = _win_attn(r3(q), k_all, v_all, r3(gates), sink, P, True, 256)
            y_s_next = _out_proj(y_s, mod, [(o_s.reshape(BS * LS, -1), w_out)], LS, None, tm)
        y_p, y_s = y_p_next, y_s_next

    (l0_ckv, l0_krope, l0_rf, l0_rb), (l1_k, l1_v), (l2_ckv, l2_krope, l2_rf, l2_rb), (l3_k, l3_v) = new_state
    return (y_p.reshape(BP, LP, D), y_s.reshape(BS, LS, D), l0_ckv, l0_krope, l0_rf, l0_rb, l1_k, l1_v,
            l2_ckv, l2_krope, l2_rf, l2_rb, l3_k, l3_v)
```

```python
import functools

import numpy as np
import jax
import jax.numpy as jnp
from jax import lax
from jax.experimental import pallas as pl
from jax.experimental.pallas import tpu as pltpu

D_MODEL = 1024
DEPTH = 4
GRID_W = 64
ROPE_BASE = 10000.0
EPS = 1e-6

MLA_HEADS = 8
MLA_NOPE = 64
MLA_ROPE = 32
MLA_QK = MLA_NOPE + MLA_ROPE
MLA_V = 64
Q_LORA = 384
KV_LORA = 256
MLA_WIDTH = MLA_HEADS * MLA_V

RET_HEADS = 4
RET_DK = 64
RET_DV = 128
RET_CHUNK = 128
RET_WIDTH = RET_HEADS * RET_DV

WIN_HEADS = 16
WIN_KV_HEADS = 4
WIN_HEAD_DIM = 64
WINDOW = 128
WIN_WIDTH = WIN_HEADS * WIN_HEAD_DIM

LANES = 128
F32 = jnp.float32
BF16 = jnp.bfloat16
NEG = -1e30
LOG2E = 1.4426950408889634
VMEM_LIMIT = 52 * 1024 * 1024
PROMPT_ROWS_PER_STEP = 4

AB_OFF = dict(cq=0, ckv=384, krope=640, ga=672, rq=1184, rk=1440, rv=1696, gb=2208)
WIN_OFF = dict(q=0, k=1024, v=1280, g=1536)


def _params(n_axes):
    return pltpu.CompilerParams(dimension_semantics=("arbitrary",) * n_axes, vmem_limit_bytes=VMEM_LIMIT)


def _const_spec(shape):
    nd = len(shape)
    return pl.BlockSpec(shape, lambda *_: (0,) * nd)


def _swap_idx(dim):
    d2 = dim // 2
    half = d2 // 2
    one = np.concatenate([np.arange(half, d2), np.arange(0, half)])
    return np.concatenate([one, d2 + one])


def _take_cols(w, cols):
    cols = np.asarray(cols, np.int32)
    picked = jnp.take(w.astype(BF16), jnp.asarray(np.maximum(cols, 0)), axis=1)
    return jnp.where(jnp.asarray(cols >= 0)[None, :], picked, jnp.zeros((), BF16))


def _rope_tables(n_tokens, dim):
    t = jnp.arange(n_tokens)
    rows = (t // GRID_W).astype(F32)
    cols = (t % GRID_W).astype(F32)
    half = dim // 4
    inv = jnp.power(jnp.float32(ROPE_BASE), -jnp.arange(half, dtype=F32) / half)

    def one(pos):
        ang = pos[:, None] * inv[None, :]
        c, s = jnp.cos(ang), jnp.sin(ang)
        return jnp.concatenate([c, c], -1), jnp.concatenate([-s, s], -1)

    cr, sr = one(rows)
    cc, sc = one(cols)
    return jnp.concatenate([cr, cc], -1), jnp.concatenate([sr, sc], -1)


def _ada_kernel(cond_ref, w_ref, b_ref, o_ref):
    c = cond_ref[...]
    sc = (c * jax.nn.sigmoid(c)).astype(BF16)
    o_ref[0] = jnp.dot(sc, w_ref[0].astype(BF16), preferred_element_type=F32) + b_ref[0]


def _ada_mod(cond, ada_w, ada_b):
    tn = 768
    n3 = 3 * D_MODEL
    return pl.pallas_call(
        _ada_kernel,
        grid=(DEPTH, n3 // tn),
        in_specs=[pl.BlockSpec((8, D_MODEL), lambda l, j: (0, 0)),
                  pl.BlockSpec((1, D_MODEL, tn), lambda l, j: (l, 0, j)),
                  pl.BlockSpec((1, 1, tn), lambda l, j: (l, 0, j))],
        out_specs=pl.BlockSpec((1, 8, tn), lambda l, j: (l, 0, j)),
        out_shape=jax.ShapeDtypeStruct((DEPTH, 8, n3), F32),
        compiler_params=_params(2),
        name="ada_mod",
    )(cond, ada_w, ada_b.reshape(DEPTH, 1, n3))


def _rms(x, g):
    return x * lax.rsqrt(jnp.mean(x * x, axis=-1, keepdims=True) + EPS) * g


def _modulated(x_ref, mod_ref, ng_ref):
    x = x_ref[...]
    shift = mod_ref[0, :, 0:D_MODEL]
    scale = mod_ref[0, :, D_MODEL:2 * D_MODEL]
    return (_rms(x, ng_ref[...]) * (1.0 + scale) + shift).astype(BF16)


def _seg_rsqrt(x, seg_ref, n_real):
    ssq = jnp.dot((x * x).astype(BF16), seg_ref[...], preferred_element_type=F32)
    return lax.rsqrt(ssq * (1.0 / n_real) + EPS)


def _silu(g):
    return g * jax.nn.sigmoid(g)


def _prep_ab_kernel(rope, *refs):
    if rope:
        (x_ref, mod_ref, ng_ref, win_ref, qng_ref, kvng_ref, wq_ref, wqs_ref, wk_ref, wv_ref, seg_ref,
         qg_ref, qgs_ref, kg_ref, kgs_ref, cm_ref, sm_ref, cr_ref, sr_ref,
         q_out, k_out, v_out, gate_out, rq_out, rk_out, rv_out) = refs
        o_kr, o_g, o_rq, o_rk, o_rv = 640, 896, 1920, 2176, 2944
    else:
        (x_ref, mod_ref, ng_ref, win_ref, qng_ref, kvng_ref, wq_ref, wk_ref, wv_ref, seg_ref,
         qg_ref, kg_ref,
         q_out, k_out, v_out, gate_out, rq_out, rk_out, rv_out, ckv_out, kr_out) = refs
        o_kr, o_g, o_rq, o_rk, o_rv = 640, 768, 1792, 2048, 2304

    h = _modulated(x_ref, mod_ref, ng_ref)
    proj = jnp.dot(h, win_ref[...], preferred_element_type=F32)

    cqn = _rms(proj[:, 0:Q_LORA], qng_ref[...]).astype(BF16)
    ckvn_f = _rms(proj[:, Q_LORA:Q_LORA + KV_LORA], kvng_ref[...])
    ckvn = ckvn_f.astype(BF16)
    kr = proj[:, o_kr:o_kr + LANES]

    gate_out[...] = _silu(proj[:, o_g:o_g + 2 * MLA_WIDTH]).astype(BF16)
    rv_out[...] = proj[:, o_rv:o_rv + RET_WIDTH].astype(BF16)

    qm = jnp.dot(cqn, wq_ref[...], preferred_element_type=F32)
    kn = jnp.dot(ckvn, wk_ref[...], preferred_element_type=F32)
    v_out[...] = jnp.dot(ckvn, wv_ref[...], preferred_element_type=F32).astype(BF16)

    qscale = MLA_QK ** -0.5 * LOG2E
    if rope:
        qs = jnp.dot(cqn, wqs_ref[...], preferred_element_type=F32)
        krs = proj[:, 768:768 + LANES]
        cm, sm = cm_ref[...], sm_ref[...]
        q_c = cm * qg_ref[...]
        q_s = sm * qgs_ref[...]
        kr_rot = kr * (cm * kg_ref[...]) + krs * (sm * kgs_ref[...])
        cr, sr = cr_ref[...], sr_ref[...]
        rq = proj[:, o_rq:o_rq + 256]
        rk = proj[:, o_rk:o_rk + 256]
        rqs = proj[:, 2432:2432 + 256]
        rks = proj[:, 2688:2688 + 256]
        for c in range(2):
            sl = slice(LANES * c, LANES * (c + 1))
            rq_out[:, sl] = ((rq[:, sl] * cr + rqs[:, sl] * sr) * (RET_DK ** -0.5)).astype(BF16)
            rk_out[:, sl] = (rk[:, sl] * cr + rks[:, sl] * sr).astype(BF16)
    else:
        q_c = jnp.broadcast_to(qg_ref[...], (qm.shape[0], LANES))
        kr_rot = kr * kg_ref[...]
        rq_out[...] = (proj[:, o_rq:o_rq + 256] * (RET_DK ** -0.5)).astype(BF16)
        rk_out[...] = proj[:, o_rk:o_rk + 256].astype(BF16)
        ckv_out[...] = ckvn_f
        kr_out[...] = kr

    kg = kg_ref[...]
    q_c2 = jnp.concatenate([q_c, q_c], axis=-1)
    kr2 = jnp.concatenate([kr, kr], axis=-1)
    kr_rot2 = jnp.concatenate([kr_rot, kr_rot], axis=-1)
    kg2 = jnp.concatenate([kg, kg], axis=-1)
    if rope:
        q_s2 = jnp.concatenate([q_s, q_s], axis=-1)
    for g in range(MLA_HEADS // 2):
        sl = slice(2 * LANES * g, 2 * LANES * (g + 1))
        qg = qm[:, sl]
        rq_n = _seg_rsqrt(qg, seg_ref, MLA_QK)
        qv = qg * q_c2
        if rope:
            qv = qv + qs[:, sl] * q_s2
        q_out[:, sl] = (qv * (rq_n * qscale)).astype(BF16)
        kgp = kn[:, sl]
        rk_n = _seg_rsqrt(kgp + kr2, seg_ref, MLA_QK)
        k_out[:, sl] = ((kgp * kg2 + kr_rot2) * rk_n).astype(BF16)


def _ab_weights(w_in, w_uq, w_ukv, q_head_g, k_head_g, rope):
    sw32 = _swap_idx(MLA_ROPE)
    sw64 = _swap_idx(RET_DK)
    z = lambda n: [-1] * n
    o = AB_OFF
    cols = list(range(o['cq'], o['cq'] + Q_LORA)) + list(range(o['ckv'], o['ckv'] + KV_LORA))
    cols += z(64) + list(range(o['krope'], o['krope'] + MLA_ROPE)) + z(32)
    if rope:
        cols += z(64) + list(o['krope'] + sw32) + z(32)
    cols += list(range(o['ga'], o['ga'] + MLA_WIDTH)) + list(range(o['gb'], o['gb'] + RET_WIDTH))
    cols += list(range(o['rq'], o['rq'] + 256)) + list(range(o['rk'], o['rk'] + 256))
    if rope:
        for base in (o['rq'], o['rk']):
            for hh in range(RET_HEADS):
                cols += list(base + RET_DK * hh + sw64)
    cols += list(range(o['rv'], o['rv'] + RET_WIDTH))
    win = _take_cols(w_in, cols).astype(BF16)

    qc, qsc, kc, vc = [], [], [], []
    for hh in range(MLA_HEADS):
        qb = MLA_QK * hh
        qc += list(range(qb, qb + MLA_QK)) + z(32)
        qsc += z(64) + list(qb + MLA_NOPE + sw32) + z(32)
        kb = (MLA_NOPE + MLA_V) * hh
        kc += list(range(kb, kb + MLA_NOPE)) + z(64)
        vcols = list(range(kb + MLA_NOPE, kb + MLA_NOPE + MLA_V))
        vc += (vcols + z(64)) if hh % 2 == 0 else (z(64) + vcols)
    wq = _take_cols(w_uq, qc).astype(BF16)
    wqs = _take_cols(w_uq, qsc).astype(BF16) if rope else None
    wk = _take_cols(w_ukv, kc).astype(BF16)
    wv = _take_cols(w_ukv, vc).astype(BF16)

    pad = lambda g: jnp.concatenate([g, jnp.zeros((LANES - MLA_QK,), F32)])[None, :]
    gsw = lambda g: jnp.concatenate([jnp.zeros((MLA_NOPE,), F32), g[MLA_NOPE + sw32],
                                     jnp.zeros((LANES - MLA_QK,), F32)])[None, :]
    qg, kg = pad(q_head_g), pad(k_head_g)
    qgs, kgs = (gsw(q_head_g), gsw(k_head_g)) if rope else (None, None)
    return win, wq, wqs, wk, wv, qg, qgs, kg, kgs


def _seg_matrix(width):
    idx = np.arange(2 * LANES) // width
    return jnp.asarray((idx[:, None] == idx[None, :]).astype(np.float32), BF16)


def _prep_ab(x, mod, norm_g, p, rope, tokens_per_batch, mod_row, tables, tm, ctx_rows=0):
    T = x.shape[0]
    win, wq, wqs, wk, wv, qg, qgs, kg, kgs = _ab_weights(
        p['w_in'], p['w_uq'], p['w_ukv'], p['q_head_g'], p['k_head_g'], rope)
    seg = _seg_matrix(LANES)
    nw = win.shape[1]
    blocks_per_batch = tokens_per_batch // tm
    tok = lambda w: pl.BlockSpec((tm, w), lambda i: (i, 0))
    if mod_row is None:
        mod_spec = pl.BlockSpec((1, 1, 3 * D_MODEL), lambda i: (i // blocks_per_batch, 0, 0))
    else:
        mod_spec = pl.BlockSpec((1, 1, 3 * D_MODEL), lambda i: (mod_row, 0, 0))
    vec = lambda a: (a, _const_spec(a.shape))
    ins = [(x, tok(D_MODEL)), (mod, mod_spec), vec(norm_g[None, :]), vec(win),
           vec(p['q_norm_g'][None, :]), vec(p['kv_norm_g'][None, :]), vec(wq)]
    if rope:
        ins.append(vec(wqs))
    ins += [vec(wk), vec(wv), vec(seg), vec(qg)]
    if rope:
        ins.append(vec(qgs))
    ins.append(vec(kg))
    if rope:
        ins.append(vec(kgs))
        nblk = tokens_per_batch // tm
        tab = lambda a: (a, pl.BlockSpec((tm, LANES), lambda i: (i % nblk, 0)))
        ins += [tab(tables['cm']), tab(tables['sm']), tab(tables['c64']), tab(tables['s64'])]
    kv_rows, kv_spec = T, tok(1024)
    if ctx_rows:
        assert ctx_rows % tm == 0 and tokens_per_batch % tm == 0
        ctx_blocks = ctx_rows // tm
        kv_rows = T + (T // tokens_per_batch) * ctx_rows
        kv_spec = pl.BlockSpec((tm, 1024), lambda i: (
            (i // blocks_per_batch) * (blocks_per_batch + ctx_blocks) + ctx_blocks + i % blocks_per_batch, 0))
    outs = [(T, 1024, BF16, tok(1024)), (kv_rows, 1024, BF16, kv_spec), (kv_rows, 1024, BF16, kv_spec),
            (T, 1024, BF16, tok(1024)), (T, 256, BF16, tok(256)), (T, 256, BF16, tok(256)),
            (T, 512, BF16, tok(512))]
    if not rope:
        outs += [(T, KV_LORA, F32, tok(KV_LORA)), (T, LANES, F32, tok(LANES))]
    return pl.pallas_call(
        functools.partial(_prep_ab_kernel, rope),
        grid=(T // tm,),
        in_specs=[s for _, s in ins],
        out_specs=[o[3] for o in outs],
        out_shape=[jax.ShapeDtypeStruct(o[:2], o[2]) for o in outs],
        compiler_params=_params(1),
        name="prep_ab_rope" if rope else "prep_ab",
    )(*[a for a, _ in ins])


def _prep_ctx_kernel(ckv_ref, kr_ref, wk_ref, wv_ref, seg_ref, kg_ref, k_all_ref, v_all_ref, k_out, v_out):
    del k_all_ref, v_all_ref
    ckv = ckv_ref[...].astype(BF16)
    kn = jnp.dot(ckv, wk_ref[...], preferred_element_type=F32)
    v_out[...] = jnp.dot(ckv, wv_ref[...], preferred_element_type=F32).astype(BF16)
    kr = kr_ref[...]
    kg = kg_ref[...]
    kr2 = jnp.concatenate([kr, kr], axis=-1)
    kg2 = jnp.concatenate([kg, kg], axis=-1)
    for g in range(MLA_HEADS // 2):
        sl = slice(2 * LANES * g, 2 * LANES * (g + 1))
        kraw = kn[:, sl] + kr2
        k_out[:, sl] = (kraw * kg2 * _seg_rsqrt(kraw, seg_ref, MLA_QK)).astype(BF16)


def _prep_ctx(ckv_c, kr_c, wk, wv, kg, k_all, v_all, ctx_rows):
    T = ckv_c.shape[0]
    n_batch = T // ctx_rows
    blocks_per_batch = k_all.shape[0] // n_batch // ctx_rows
    seg = _seg_matrix(LANES)
    tok = lambda w: pl.BlockSpec((ctx_rows, w), lambda i: (i, 0))
    out_spec = pl.BlockSpec((ctx_rows, 1024), lambda i: (i * blocks_per_batch, 0))
    any_spec = pl.BlockSpec(memory_space=pl.ANY)
    return pl.pallas_call(
        _prep_ctx_kernel,
        grid=(n_batch,),
        in_specs=[tok(KV_LORA), tok(LANES), _const_spec(wk.shape), _const_spec(wv.shape),
                  _const_spec(seg.shape), _const_spec(kg.shape), any_spec, any_spec],
        out_specs=[out_spec, out_spec],
        out_shape=[jax.ShapeDtypeStruct(k_all.shape, BF16)] * 2,
        input_output_aliases={6: 0, 7: 1},
        compiler_params=_params(1),
        name="prep_ctx",
    )(ckv_c, kr_c, wk, wv, seg, kg, k_all, v_all)


def _mla_attn_kernel(q_ref, k_ref, v_ref, g_ref, o_ref):
    dn = (((1,), (1,)), ((), ()))
    for b in range(q_ref.shape[0]):
        out = None
        for j in range(2):
            sl = slice(LANES * j, LANES * (j + 1))
            s = lax.dot_general(q_ref[b, :, sl], k_ref[b, :, sl], dn, preferred_element_type=F32)
            p = jnp.exp2(s - jnp.max(s, axis=-1, keepdims=True))
            l = jnp.sum(p, axis=-1, keepdims=True)
            oj = jnp.dot(p.astype(BF16), v_ref[b, :, sl], preferred_element_type=F32) / l
            out = oj if out is None else out + oj
        o_ref[b] = (out * g_ref[b].astype(F32)).astype(BF16)


def _mla_attn(q, k, v, gates, bb, tq):
    B, L, _ = q.shape
    Lk = k.shape[1]
    hp = MLA_HEADS // 2
    qspec = pl.BlockSpec((bb, tq, 2 * LANES), lambda b, h, i: (b, i, h))
    kspec = pl.BlockSpec((bb, Lk, 2 * LANES), lambda b, h, i: (b, 0, h))
    ospec = pl.BlockSpec((bb, tq, LANES), lambda b, h, i: (b, i, h))
    return pl.pallas_call(
        _mla_attn_kernel,
        grid=(B // bb, hp, L // tq),
        in_specs=[qspec, kspec, kspec, ospec],
        out_specs=ospec,
        out_shape=jax.ShapeDtypeStruct((B, L, MLA_WIDTH), BF16),
        compiler_params=_params(3),
        name="mla_attn",
    )(q, k, v, gates)


def _ret_kernel(n_chunks, decay_ref, q_ref, k_ref, v_ref, sf_ref, sb_ref, gn_ref, gate_ref,
                o_ref, rf_ref, rb_ref, acc_ref, st_ref):
    C = RET_CHUNK
    pair = pl.program_id(1)
    row = lax.broadcasted_iota(jnp.int32, (C, C), 0).astype(F32)
    col = lax.broadcasted_iota(jnp.int32, (C, C), 1).astype(F32)
    lane_head = lax.broadcasted_iota(jnp.int32, (C, LANES), 1) // RET_DK
    rel = row - col

    tabs = []
    for hh in range(2):
        per_dir = []
        for d in range(2):
            lg = -jnp.exp(jnp.full((C, C), decay_ref[d, 2 * pair + hh], F32))
            if d == 0:
                dmask = jnp.where(rel >= 0, jnp.exp(jnp.maximum(rel, 0.0) * lg), 0.0)
                k_w = jnp.exp((C - 1.0 - row) * lg)
                q_w = jnp.exp((row + 1.0) * lg)
            else:
                dmask = jnp.where(rel <= 0, jnp.exp(jnp.maximum(-rel, 0.0) * lg), 0.0)
                k_w = jnp.exp(row * lg)
                q_w = jnp.exp((C - row) * lg)
            per_dir.append((dmask, k_w, q_w, jnp.exp(C * lg)))
            st_ref[2 * hh + d] = jnp.zeros((C, RET_DV), F32)
        tabs.append(per_dir)
        st_ref[2 * hh + 0, RET_DK * hh:RET_DK * (hh + 1), :] = sf_ref[0, hh]
        st_ref[2 * hh + 1, RET_DK * hh:RET_DK * (hh + 1), :] = sb_ref[0, hh]
    acc_ref[...] = jnp.zeros_like(acc_ref)

    def step(n, _):
        for d in range(2):
            cidx = n if d == 0 else n_chunks - 1 - n
            r0 = pl.multiple_of(cidx * C, C)
            qc = q_ref[0, pl.ds(r0, C), :].astype(F32)
            kc = k_ref[0, pl.ds(r0, C), :]
            kf = kc.astype(F32)
            for hh in range(2):
                dmask, k_w, q_w, g_chunk = tabs[hh][d]
                vc = v_ref[0, pl.ds(r0, C), RET_DV * hh:RET_DV * (hh + 1)]
                qm = jnp.where(lane_head == hh, qc, 0.0)
                s = lax.dot_general(qm.astype(BF16), kc, (((1,), (1,)), ((), ())),
                                    preferred_element_type=F32) * dmask
                intra = jnp.dot(s.astype(BF16), vc, preferred_element_type=F32)
                state = st_ref[2 * hh + d]
                inter = jnp.dot((qm * q_w).astype(BF16), state.astype(BF16), preferred_element_type=F32)
                u = jnp.dot((kf * k_w).T.astype(BF16), vc, preferred_element_type=F32)
                st_ref[2 * hh + d] = g_chunk * state + u
                acc_ref[pl.ds(r0, C), RET_DV * hh:RET_DV * (hh + 1)] += intra + inter
        return 0

    lax.fori_loop(0, n_chunks, step, 0)

    for hh in range(2):
        rf_ref[0, hh] = st_ref[2 * hh + 0, RET_DK * hh:RET_DK * (hh + 1), :]
        rb_ref[0, hh] = st_ref[2 * hh + 1, RET_DK * hh:RET_DK * (hh + 1), :]

    def norm(n, _):
        r0 = pl.multiple_of(n * C, C)
        for hh in range(2):
            sl = slice(RET_DV * hh, RET_DV * (hh + 1))
            y = acc_ref[pl.ds(r0, C), sl]
            mu = jnp.mean(y, axis=-1, keepdims=True)
            yc = y - mu
            var = jnp.mean(yc * yc, axis=-1, keepdims=True)
            out = yc * lax.rsqrt(var + EPS) * gn_ref[:, sl]
            o_ref[0, pl.ds(r0, C), sl] = (out * gate_ref[0, pl.ds(r0, C), sl].astype(F32)).astype(BF16)
        return 0

    lax.fori_loop(0, n_chunks, norm, 0)


def _retention(rq, rk, rv, sf, sb, decay, ret_norm_g, gates):
    B, L, _ = rq.shape
    n_chunks = L // RET_CHUNK
    hp = RET_HEADS // 2
    st_spec = pl.BlockSpec((1, 2, RET_DK, RET_DV), lambda b, h: (b, h, 0, 0))
    st_shape = jax.ShapeDtypeStruct((B, RET_HEADS, RET_DK, RET_DV), F32)
    return pl.pallas_call(
        functools.partial(_ret_kernel, n_chunks),
        grid=(B, hp),
        in_specs=[pl.BlockSpec(memory_space=pltpu.SMEM),
                  pl.BlockSpec((1, L, LANES), lambda b, h: (b, 0, h)),
                  pl.BlockSpec((1, L, LANES), lambda b, h: (b, 0, h)),
                  pl.BlockSpec((1, L, 2 * RET_DV), lambda b, h: (b, 0, h)),
                  st_spec, st_spec,
                  pl.BlockSpec((1, 2 * RET_DV), lambda b, h: (0, h)),
                  pl.BlockSpec((1, L, 2 * RET_DV), lambda b, h: (b, 0, hp + h))],
        out_specs=[pl.BlockSpec((1, L, 2 * RET_DV), lambda b, h: (b, 0, h)), st_spec, st_spec],
        out_shape=[jax.ShapeDtypeStruct((B, L, RET_WIDTH), BF16), st_shape, st_shape],
        scratch_shapes=[pltpu.VMEM((L, 2 * RET_DV), F32), pltpu.VMEM((4, RET_CHUNK, RET_DV), F32)],
        compiler_params=_params(2),
        name="retention",
    )(decay, rq, rk, rv, sf, sb, ret_norm_g[None, :], gates)


def _prep_win_kernel(rope, *refs):
    if rope:
        (x_ref, mod_ref, ng_ref, win_ref, seg_ref, exp_ref, qg_ref, qgs_ref, kg_ref, kgs_ref, c_ref, s_ref,
         q_out, k_out, v_out, gate_out) = refs
    else:
        (x_ref, mod_ref, ng_ref, win_ref, seg_ref, exp_ref, qg_ref, kg_ref,
         q_out, k_out, v_out, gate_out, kst_out, vst_out) = refs
    o = WIN_OFF
    h = _modulated(x_ref, mod_ref, ng_ref)
    proj = jnp.dot(h, win_ref[...], preferred_element_type=F32)
    gate_out[...] = _silu(proj[:, o['g']:o['g'] + WIN_WIDTH]).astype(BF16)

    qg2 = jnp.concatenate([qg_ref[...]] * 2, axis=-1)
    kg2 = jnp.concatenate([kg_ref[...]] * 2, axis=-1)
    if rope:
        c2 = jnp.concatenate([c_ref[...]] * 2, axis=-1)
        s2 = jnp.concatenate([s_ref[...]] * 2, axis=-1)
        q_c, q_s = c2 * qg2, s2 * jnp.concatenate([qgs_ref[...]] * 2, axis=-1)
        k_c, k_s = c2 * kg2, s2 * jnp.concatenate([kgs_ref[...]] * 2, axis=-1)
    qscale = WIN_HEAD_DIM ** -0.5 * LOG2E
    for g in range(WIN_WIDTH // 256):
        sl = slice(256 * g, 256 * (g + 1))
        qg = proj[:, sl]
        rn = _seg_rsqrt(qg, seg_ref, WIN_HEAD_DIM) * qscale
        if rope:
            qv = qg * q_c + proj[:, 2560 + 256 * g:2560 + 256 * (g + 1)] * q_s
        else:
            qv = qg * qg2
        q_out[:, sl] = (qv * rn).astype(BF16)
    kraw = proj[:, o['k']:o['k'] + 256]
    rn = _seg_rsqrt(kraw, seg_ref, WIN_HEAD_DIM)
    if rope:
        kn = (kraw * k_c + proj[:, 3584:3584 + 256] * k_s) * rn
    else:
        kn = kraw * kg2 * rn
    v = proj[:, o['v']:o['v'] + 256]
    if not rope:
        kst_out[...] = kn
        vst_out[...] = v
    k_out[...] = jnp.dot(kn.astype(BF16), exp_ref[...], preferred_element_type=F32).astype(BF16)
    v_out[...] = jnp.dot(v.astype(BF16), exp_ref[...], preferred_element_type=F32).astype(BF16)


def _win_expand_matrix():
    e = np.zeros((256, 1024), np.float32)
    for j in range(WIN_KV_HEADS):
        for i in range(WIN_HEAD_DIM):
            e[WIN_HEAD_DIM * j + i, 256 * j + i] = 1.0
            e[WIN_HEAD_DIM * j + i, 256 * j + 192 + i] = 1.0
    return jnp.asarray(e, BF16)


def _prep_win(x, mod, norm_g, p, rope, tokens_per_batch, mod_row, tables, tm):
    T = x.shape[0]
    o = WIN_OFF
    sw64 = _swap_idx(WIN_HEAD_DIM)
    if rope:
        cols = list(range(2560))
        for hh in range(WIN_HEADS):
            cols += list(o['q'] + WIN_HEAD_DIM * hh + sw64)
        for hh in range(WIN_KV_HEADS):
            cols += list(o['k'] + WIN_HEAD_DIM * hh + sw64)
        win = _take_cols(p['w_in'], cols).astype(BF16)
    else:
        win = p['w_in'].astype(BF16)
    seg = _seg_matrix(WIN_HEAD_DIM)
    expand = _win_expand_matrix()
    rep = lambda g: jnp.concatenate([g, g])[None, :]
    blocks_per_batch = tokens_per_batch // tm
    tok = lambda w: pl.BlockSpec((tm, w), lambda i: (i, 0))
    if mod_row is None:
        mod_spec = pl.BlockSpec((1, 1, 3 * D_MODEL), lambda i: (i // blocks_per_batch, 0, 0))
    else:
        mod_spec = pl.BlockSpec((1, 1, 3 * D_MODEL), lambda i: (mod_row, 0, 0))
    vec = lambda a: (a, _const_spec(a.shape))
    ins = [(x, tok(D_MODEL)), (mod, mod_spec), vec(norm_g[None, :]), vec(win), vec(seg), vec(expand),
           vec(rep(p['q_head_g']))]
    if rope:
        ins.append(vec(rep(p['q_head_g'][sw64])))
    ins.append(vec(rep(p['k_head_g'])))
    if rope:
        ins.append(vec(rep(p['k_head_g'][sw64])))
        tab = lambda a: (a, pl.BlockSpec((tm, LANES), lambda i: (i % blocks_per_batch, 0)))
        ins += [tab(tables['c64']), tab(tables['s64'])]
    outs = [(1024, BF16), (1024, BF16), (1024, BF16), (1024, BF16)]
    if not rope:
        outs += [(256, F32), (256, F32)]
    return pl.pallas_call(
        functools.partial(_prep_win_kernel, rope),
        grid=(T // tm,),
        in_specs=[s for _, s in ins],
        out_specs=[tok(w) for w, _ in outs],
        out_shape=[jax.ShapeDtypeStruct((T, w), dt) for w, dt in outs],
        compiler_params=_params(1),
        name="prep_win_rope" if rope else "prep_win",
    )(*[a for a, _ in ins])


def _win_attn_kernel(local, *refs):
    if local:
        sink_ref, q_ref, kf_ref, vf_ref, k_ref, v_ref, g_ref, bias_ref, o_ref = refs
    else:
        sink_ref, q_ref, kf_ref, vf_ref, g_ref, o_ref = refs
    tq = q_ref.shape[1]
    j = pl.program_id(1)
    upper = lax.broadcasted_iota(jnp.int32, (2 * tq, 1), 0) >= tq
    if local:
        span = bias_ref.shape[2]
        start = jnp.clip(pl.program_id(2) * tq - WINDOW, 0, k_ref.shape[1] - span)
        start = pl.multiple_of(start, WINDOW)
    dn = (((1,), (1,)), ((), ()))
    for b in range(q_ref.shape[0]):
        q2 = jnp.concatenate([q_ref[b, :, 0:LANES], q_ref[b, :, LANES:2 * LANES]], axis=0)
        out = None
        for half in range(2):
            sl = slice(LANES * half, LANES * (half + 1))
            sk = jnp.where(upper, sink_ref[4 * j + 2 + half], sink_ref[4 * j + half]) * LOG2E
            s_ctx = lax.dot_general(q2, kf_ref[b, :, sl], dn, preferred_element_type=F32)
            m = jnp.maximum(jnp.max(s_ctx, axis=-1, keepdims=True), sk)
            if local:
                s_loc = lax.dot_general(q2, k_ref[b, pl.ds(start, span), sl], dn,
                                        preferred_element_type=F32) + bias_ref[0]
                m = jnp.maximum(m, jnp.max(s_loc, axis=-1, keepdims=True))
            e_ctx = jnp.exp2(s_ctx - m)
            den = jnp.sum(e_ctx, axis=-1, keepdims=True) + jnp.exp2(sk - m)
            acc = jnp.dot(e_ctx.astype(BF16), vf_ref[b, :, sl], preferred_element_type=F32)
            if local:
                e_loc = jnp.exp2(s_loc - m)
                den = den + jnp.sum(e_loc, axis=-1, keepdims=True)
                acc = acc + jnp.dot(e_loc.astype(BF16), v_ref[b, pl.ds(start, span), sl],
                                    preferred_element_type=F32)
            oh = acc / den
            out = oh if out is None else out + oh
        gate = g_ref[b].astype(F32)
        o_ref[b, :, 0:LANES] = (out[0:tq] * gate[:, 0:LANES]).astype(BF16)
        o_ref[b, :, LANES:2 * LANES] = (out[tq:2 * tq] * gate[:, LANES:2 * LANES]).astype(BF16)


def _window_bias(tq):
    span = tq + 2 * WINDOW
    qi = jnp.arange(2 * tq)[:, None] % tq
    kk = jnp.arange(span)[None, :]
    cases = [jnp.abs(kk - off - qi) <= WINDOW for off in (0, WINDOW, 2 * WINDOW)]
    return jnp.where(jnp.stack(cases), 0.0, NEG).astype(F32)


def _win_attn(q, kf, vf, own, gates, sink, bb, tq):
    B, L, _ = q.shape
    nb = L // tq
    qspec = pl.BlockSpec((bb, tq, 2 * LANES), lambda b, h, i: (b, i, h))
    fspec = pl.BlockSpec((bb, kf.shape[1], 2 * LANES), lambda b, h, i: (b, 0, h))
    in_specs = [pl.BlockSpec(memory_space=pltpu.SMEM), qspec, fspec, fspec]
    args = [sink, q, kf, vf]
    if own is not None:
        assert nb >= 3 and bb == 1
        kspec = pl.BlockSpec((bb, L, 2 * LANES), lambda b, h, i: (b, 0, h))
        in_specs += [kspec, kspec]
        args += list(own)
    in_specs.append(qspec)
    args.append(gates)
    if own is not None:
        bias = _window_bias(tq)
        in_specs.append(pl.BlockSpec(
            (1,) + bias.shape[1:],
            lambda b, h, i: (jnp.where(i == 0, 0, jnp.where(i == nb - 1, 2, 1)), 0, 0)))
        args.append(bias)
    return pl.pallas_call(
        functools.partial(_win_attn_kernel, own is not None),
        grid=(B // bb, WIN_KV_HEADS, nb),
        in_specs=in_specs,
        out_specs=qspec,
        out_shape=jax.ShapeDtypeStruct((B, L, WIN_WIDTH), BF16),
        compiler_params=_params(3),
        name="win_attn_local" if own is not None else "win_attn",
    )(*args)


def _out_kernel(n_parts, *refs):
    x_ref, mod_ref = refs[0], refs[1]
    parts = refs[2:2 + 2 * n_parts]
    o_ref = refs[2 + 2 * n_parts]
    y = None
    for i in range(n_parts):
        t = jnp.dot(parts[2 * i][...], parts[2 * i + 1][...], preferred_element_type=F32)
        y = t if y is None else y + t
    gate = mod_ref[0, :, 2 * D_MODEL:3 * D_MODEL]
    o_ref[...] = x_ref[...] + gate * y


def _out_proj(x, mod, parts, tokens_per_batch, mod_row, tm):
    T = x.shape[0]
    blocks_per_batch = tokens_per_batch // tm
    tok = lambda w: pl.BlockSpec((tm, w), lambda i: (i, 0))
    if mod_row is None:
        mod_spec = pl.BlockSpec((1, 1, 3 * D_MODEL), lambda i: (i // blocks_per_batch, 0, 0))
    else:
        mod_spec = pl.BlockSpec((1, 1, 3 * D_MODEL), lambda i: (mod_row, 0, 0))
    in_specs = [tok(D_MODEL), mod_spec]
    args = [x, mod]
    for a, w in parts:
        in_specs += [tok(a.shape[1]), _const_spec(w.shape)]
        args += [a, w]
    return pl.pallas_call(
        functools.partial(_out_kernel, len(parts)),
        grid=(T // tm,),
        in_specs=in_specs,
        out_specs=tok(D_MODEL),
        out_shape=jax.ShapeDtypeStruct((T, D_MODEL), F32),
        compiler_params=_params(1),
        name="out_proj",
    )(*args)


def kernel(x_prompt, x_sample, cache_l0_mla_ckv, cache_l0_mla_krope, state_l0_ret_fwd, state_l0_ret_bwd, cache_l1_win_k, cache_l1_win_v, cache_l2_mla_ckv, cache_l2_mla_krope, state_l2_ret_fwd, state_l2_ret_bwd, cache_l3_win_k, cache_l3_win_v, c, c_ctx, ada_w, ada_b, norm_g, ab_w_in, mla_q_norm_g, mla_w_uq, mla_kv_norm_g, mla_w_ukv, mla_q_head_g, mla_k_head_g, ret_decay, ret_norm_g, ab_w_out, win_w_in, win_q_head_g, win_k_head_g, win_sink, win_w_out):
    BP, LP, D = x_prompt.shape
    BS, LS, _ = x_sample.shape
    P = cache_l0_mla_ckv.shape[1]
    ctx_caches = ((cache_l0_mla_ckv, cache_l0_mla_krope, state_l0_ret_fwd, state_l0_ret_bwd),
                  (cache_l1_win_k, cache_l1_win_v),
                  (cache_l2_mla_ckv, cache_l2_mla_krope, state_l2_ret_fwd, state_l2_ret_bwd),
                  (cache_l3_win_k, cache_l3_win_v))

    cond = jnp.concatenate([c, c_ctx[None, :], jnp.zeros((8 - BS - 1, D), F32)], axis=0)
    mod_all = _ada_mod(cond, ada_w, ada_b).reshape(DEPTH, 8, 1, 3 * D)
    ctx_row = BS

    c32, s32 = _rope_tables(LS, MLA_ROPE)
    c64, s64 = _rope_tables(LS, RET_DK)
    ones = jnp.ones((LS, MLA_NOPE), F32)
    zeros = jnp.zeros((LS, MLA_NOPE), F32)
    z32 = jnp.zeros((LS, LANES - MLA_QK), F32)
    tables = dict(cm=jnp.concatenate([ones, c32, z32], -1), sm=jnp.concatenate([zeros, s32, z32], -1),
                  c64=jnp.concatenate([c64, c64], -1), s64=jnp.concatenate([s64, s64], -1))

    y_p = x_prompt.reshape(BP * LP, D)
    y_s = x_sample.reshape(BS * LS, D)
    tm = 512
    new_state = []
    for l in range(DEPTH):
        i = l // 2
        mod = mod_all[l]
        if l % 2 == 0:
            p = {'w_in': ab_w_in[i], 'q_norm_g': mla_q_norm_g[i], 'w_uq': mla_w_uq[i],
                 'kv_norm_g': mla_kv_norm_g[i], 'w_ukv': mla_w_ukv[i], 'q_head_g': mla_q_head_g[i],
                 'k_head_g': mla_k_head_g[i]}
            w_out = ab_w_out[i].astype(BF16)
            parts_w = (w_out[:MLA_WIDTH], w_out[MLA_WIDTH:])
            ckv_c, krope_c, sf, sb = ctx_caches[l]

            q, k, v, gates, rq, rk, rv, ckv, kr = _prep_ab(y_p, mod, norm_g[l], p, False, LP, ctx_row, None, tm)
            r3 = lambda a: a.reshape(BP, LP, a.shape[-1])
            a_p = _mla_attn(r3(q), r3(k), r3(v), r3(gates), PROMPT_ROWS_PER_STEP, LP)
            zst = jnp.zeros((BP, RET_HEADS, RET_DK, RET_DV), F32)
            r_p, rf, rb = _retention(r3(rq), r3(rk), r3(rv), zst, zst, ret_decay[i], ret_norm_g[i], r3(gates))
            new_state.append((ckv.reshape(BP, LP, KV_LORA),
                              kr[:, MLA_NOPE:MLA_QK].reshape(BP, LP, MLA_ROPE), rf, rb))
            y_p_next = _out_proj(y_p, mod, [(a_p.reshape(BP * LP, -1), parts_w[0]),
                                            (r_p.reshape(BP * LP, -1), parts_w[1])], LP, ctx_row, tm)

            q, k, v, gates, rq, rk, rv = _prep_ab(y_s, mod, norm_g[l], p, True, LS, None, tables, tm, P)
            _, _, _, wk, wv, _, _, kg, _ = _ab_weights(p['w_in'], p['w_uq'], p['w_ukv'],
                                                        p['q_head_g'], p['k_head_g'], False)
            kr_c = jnp.pad(krope_c.reshape(BS * P, MLA_ROPE), ((0, 0), (MLA_NOPE, LANES - MLA_QK)))
            k, v = _prep_ctx(ckv_c.reshape(BS * P, KV_LORA), kr_c, wk, wv, kg, k, v, P)
            r3 = lambda a: a.reshape(BS, -1, a.shape[-1])
            a_s = _mla_attn(r3(q), r3(k), r3(v), r3(gates), 1, 256)
            r_s, _, _ = _retention(r3(rq), r3(rk), r3(rv), sf, sb, ret_decay[i], ret_norm_g[i], r3(gates))
            y_s_next = _out_proj(y_s, mod, [(a_s.reshape(BS * LS, -1), parts_w[0]),
                                            (r_s.reshape(BS * LS, -1), parts_w[1])], LS, None, tm)
        else:
            p = {'w_in': win_w_in[i], 'q_head_g': win_q_head_g[i], 'k_head_g': win_k_head_g[i]}
            w_out = win_w_out[i].astype(BF16)
            sink = win_sink[i]
            kc, vc = ctx_caches[l]

            q, k, v, gates, kst, vst = _prep_win(y_p, mod, norm_g[l], p, False, LP, ctx_row, None, tm)
            r3 = lambda a: a.reshape(BP, LP, a.shape[-1])
            o_p = _win_attn(r3(q), r3(k), r3(v), None, r3(gates), sink, PROMPT_ROWS_PER_STEP, LP)
            new_state.append((kst.reshape(BP, LP, WIN_KV_HEADS, WIN_HEAD_DIM),
                              vst.reshape(BP, LP, WIN_KV_HEADS, WIN_HEAD_DIM)))
            y_p_next = _out_proj(y_p, mod, [(o_p.reshape(BP * LP, -1), w_out)], LP, ctx_row, tm)

            q, k, v, gates = _prep_win(y_s, mod, norm_g[l], p, True, LS, None, tables, tm)
            r3 = lambda a: a.reshape(BS, -1, a.shape[-1])
            def slots(a):
                a = a.astype(BF16)
                z = jnp.zeros(a.shape[:-1] + (2 * WIN_HEAD_DIM,), BF16)
                return jnp.concatenate([a, z, a], axis=-1).reshape(BS, P, 4 * 2 * LANES)

            o_s = _win_attn(r3(q), slots(kc), slots(vc), (r3(k), r3(v)), r3(gates), sink, 1, 256)
            y_s_next = _out_proj(y_s, mod, [(o_s.reshape(BS * LS, -1), w_out)], LS, None, tm)
        y_p, y_s = y_p_next, y_s_next

    (l0_ckv, l0_krope, l0_rf, l0_rb), (l1_k, l1_v), (l2_ckv, l2_krope, l2_rf, l2_rb), (l3_k, l3_v) = new_state
    return (y_p.reshape(BP, LP, D), y_s.reshape(BS, LS, D), l0_ckv, l0_krope, l0_rf, l0_rb, l1_k, l1_v,
            l2_ckv, l2_krope, l2_rf, l2_rb, l3_k, l3_v)
```

```python
import functools

import numpy as np
import jax
import jax.numpy as jnp
from jax import lax
from jax.experimental import pallas as pl
from jax.experimental.pallas import tpu as pltpu

D_MODEL = 1024
DEPTH = 4
GRID_W = 64
ROPE_BASE = 10000.0
EPS = 1e-6

MLA_HEADS = 8
MLA_NOPE = 64
MLA_ROPE = 32
MLA_QK = MLA_NOPE + MLA_ROPE
MLA_V = 64
Q_LORA = 384
KV_LORA = 256
MLA_WIDTH = MLA_HEADS * MLA_V

RET_HEADS = 4
RET_DK = 64
RET_DV = 128
RET_CHUNK = 256
RET_WIDTH = RET_HEADS * RET_DV

WIN_HEADS = 16
WIN_KV_HEADS = 4
WIN_HEAD_DIM = 64
WINDOW = 128
WIN_WIDTH = WIN_HEADS * WIN_HEAD_DIM

LANES = 128
F32 = jnp.float32
BF16 = jnp.bfloat16
NEG = -1e30
LOG2E = 1.4426950408889634
VMEM_LIMIT = 52 * 1024 * 1024
PROMPT_ROWS_PER_STEP = 4

AB_OFF = dict(cq=0, ckv=384, krope=640, ga=672, rq=1184, rk=1440, rv=1696, gb=2208)
WIN_OFF = dict(q=0, k=1024, v=1280, g=1536)


def _params(n_axes):
    return pltpu.CompilerParams(dimension_semantics=("arbitrary",) * n_axes, vmem_limit_bytes=VMEM_LIMIT)


def _const_spec(shape):
    nd = len(shape)
    return pl.BlockSpec(shape, lambda *_: (0,) * nd)


def _swap_idx(dim):
    d2 = dim // 2
    half = d2 // 2
    one = np.concatenate([np.arange(half, d2), np.arange(0, half)])
    return np.concatenate([one, d2 + one])


def _take_cols(w, cols):
    cols = np.asarray(cols, np.int32)
    picked = jnp.take(w.astype(BF16), jnp.asarray(np.maximum(cols, 0)), axis=1)
    return jnp.where(jnp.asarray(cols >= 0)[None, :], picked, jnp.zeros((), BF16))


def _rope_tables(n_tokens, dim):
    t = jnp.arange(n_tokens)
    rows = (t // GRID_W).astype(F32)
    cols = (t % GRID_W).astype(F32)
    half = dim // 4
    inv = jnp.power(jnp.float32(ROPE_BASE), -jnp.arange(half, dtype=F32) / half)

    def one(pos):
        ang = pos[:, None] * inv[None, :]
        c, s = jnp.cos(ang), jnp.sin(ang)
        return jnp.concatenate([c, c], -1), jnp.concatenate([-s, s], -1)

    cr, sr = one(rows)
    cc, sc = one(cols)
    return jnp.concatenate([cr, cc], -1), jnp.concatenate([sr, sc], -1)


def _ada_kernel(cond_ref, w_ref, b_ref, o_ref):
    c = cond_ref[...]
    sc = (c * jax.nn.sigmoid(c)).astype(BF16)
    o_ref[0] = jnp.dot(sc, w_ref[0].astype(BF16), preferred_element_type=F32) + b_ref[0]


def _ada_mod(cond, ada_w, ada_b):
    tn = 768
    n3 = 3 * D_MODEL
    return pl.pallas_call(
        _ada_kernel,
        grid=(DEPTH, n3 // tn),
        in_specs=[pl.BlockSpec((8, D_MODEL), lambda l, j: (0, 0)),
                  pl.BlockSpec((1, D_MODEL, tn), lambda l, j: (l, 0, j)),
                  pl.BlockSpec((1, 1, tn), lambda l, j: (l, 0, j))],
        out_specs=pl.BlockSpec((1, 8, tn), lambda l, j: (l, 0, j)),
        out_shape=jax.ShapeDtypeStruct((DEPTH, 8, n3), F32),
        compiler_params=_params(2),
        name="ada_mod",
    )(cond, ada_w, ada_b.reshape(DEPTH, 1, n3))


def _rms(x, g):
    return x * lax.rsqrt(jnp.mean(x * x, axis=-1, keepdims=True) + EPS) * g


def _modulated(x_ref, mod_ref, ng_ref):
    x = x_ref[...]
    shift = mod_ref[0, :, 0:D_MODEL]
    scale = mod_ref[0, :, D_MODEL:2 * D_MODEL]
    return (_rms(x, ng_ref[...]) * (1.0 + scale) + shift).astype(BF16)


def _seg_rsqrt(x, seg_ref, n_real):
    ssq = jnp.dot((x * x).astype(BF16), seg_ref[...], preferred_element_type=F32)
    return lax.rsqrt(ssq * (1.0 / n_real) + EPS)


def _silu(g):
    return g * jax.nn.sigmoid(g)


def _ctx_kv(ckv_ref, kr_ref, wk_ref, wv_ref, seg_ref, kg_ref, k_out, v_out):
    ckv = ckv_ref[...].astype(BF16)
    kn = jnp.dot(ckv, wk_ref[...], preferred_element_type=F32)
    v_out[...] = jnp.dot(ckv, wv_ref[...], preferred_element_type=F32).astype(BF16)
    kr = kr_ref[...]
    kg = kg_ref[...]
    kr2 = jnp.concatenate([kr, kr], axis=-1)
    kg2 = jnp.concatenate([kg, kg], axis=-1)
    for g in range(MLA_HEADS // 2):
        sl = slice(2 * LANES * g, 2 * LANES * (g + 1))
        kraw = kn[:, sl] + kr2
        k_out[:, sl] = (kraw * kg2 * _seg_rsqrt(kraw, seg_ref, MLA_QK)).astype(BF16)


def _prep_ab_kernel(rope, ctx_blocks, steps_per_batch, *refs):
    if not ctx_blocks:
        _prep_ab_body(rope, *refs)
        return
    n_in = 19
    ckvc_ref, krc_ref = refs[n_in:n_in + 2]
    body_refs = refs[:n_in] + refs[n_in + 2:]
    wk_ref, wv_ref, seg_ref, kg_ref = refs[8], refs[9], refs[10], refs[13]
    k_out, v_out = body_refs[n_in + 1], body_refs[n_in + 2]
    is_ctx = pl.program_id(0) % steps_per_batch < ctx_blocks

    @pl.when(is_ctx)
    def _():
        _ctx_kv(ckvc_ref, krc_ref, wk_ref, wv_ref, seg_ref, kg_ref, k_out, v_out)

    @pl.when(jnp.logical_not(is_ctx))
    def _():
        _prep_ab_body(rope, *body_refs)


def _prep_ab_body(rope, *refs):
    if rope:
        (x_ref, mod_ref, ng_ref, win_ref, qng_ref, kvng_ref, wq_ref, wqs_ref, wk_ref, wv_ref, seg_ref,
         qg_ref, qgs_ref, kg_ref, kgs_ref, cm_ref, sm_ref, cr_ref, sr_ref,
         q_out, k_out, v_out, gate_out, rq_out, rk_out, rv_out) = refs
        o_kr, o_g, o_rq, o_rk, o_rv = 640, 896, 1920, 2176, 2944
    else:
        (x_ref, mod_ref, ng_ref, win_ref, qng_ref, kvng_ref, wq_ref, wk_ref, wv_ref, seg_ref,
         qg_ref, kg_ref,
         q_out, k_out, v_out, gate_out, rq_out, rk_out, rv_out, ckv_out, kr_out) = refs
        o_kr, o_g, o_rq, o_rk, o_rv = 640, 768, 1792, 2048, 2304

    h = _modulated(x_ref, mod_ref, ng_ref)
    proj = jnp.dot(h, win_ref[...], preferred_element_type=F32)

    cqn = _rms(proj[:, 0:Q_LORA], qng_ref[...]).astype(BF16)
    ckvn_f = _rms(proj[:, Q_LORA:Q_LORA + KV_LORA], kvng_ref[...])
    ckvn = ckvn_f.astype(BF16)
    kr = proj[:, o_kr:o_kr + LANES]

    gate_out[...] = _silu(proj[:, o_g:o_g + 2 * MLA_WIDTH]).astype(BF16)
    rv_out[...] = proj[:, o_rv:o_rv + RET_WIDTH].astype(BF16)

    qm = jnp.dot(cqn, wq_ref[...], preferred_element_type=F32)
    kn = jnp.dot(ckvn, wk_ref[...], preferred_element_type=F32)
    v_out[...] = jnp.dot(ckvn, wv_ref[...], preferred_element_type=F32).astype(BF16)

    qscale = MLA_QK ** -0.5 * LOG2E
    if rope:
        qs = jnp.dot(cqn, wqs_ref[...], preferred_element_type=F32)
        krs = proj[:, 768:768 + LANES]
        cm, sm = cm_ref[...], sm_ref[...]
        q_c = cm * qg_ref[...]
        q_s = sm * qgs_ref[...]
        kr_rot = kr * (cm * kg_ref[...]) + krs * (sm * kgs_ref[...])
        cr, sr = cr_ref[...], sr_ref[...]
        rq = proj[:, o_rq:o_rq + 256]
        rk = proj[:, o_rk:o_rk + 256]
        rqs = proj[:, 2432:2432 + 256]
        rks = proj[:, 2688:2688 + 256]
        for c in range(2):
            sl = slice(LANES * c, LANES * (c + 1))
            rq_out[:, sl] = ((rq[:, sl] * cr + rqs[:, sl] * sr) * (RET_DK ** -0.5)).astype(BF16)
            rk_out[:, sl] = (rk[:, sl] * cr + rks[:, sl] * sr).astype(BF16)
    else:
        q_c = jnp.broadcast_to(qg_ref[...], (qm.shape[0], LANES))
        kr_rot = kr * kg_ref[...]
        rq_out[...] = (proj[:, o_rq:o_rq + 256] * (RET_DK ** -0.5)).astype(BF16)
        rk_out[...] = proj[:, o_rk:o_rk + 256].astype(BF16)
        ckv_out[...] = ckvn_f
        kr_out[...] = kr

    kg = kg_ref[...]
    q_c2 = jnp.concatenate([q_c, q_c], axis=-1)
    kr2 = jnp.concatenate([kr, kr], axis=-1)
    kr_rot2 = jnp.concatenate([kr_rot, kr_rot], axis=-1)
    kg2 = jnp.concatenate([kg, kg], axis=-1)
    if rope:
        q_s2 = jnp.concatenate([q_s, q_s], axis=-1)
    for g in range(MLA_HEADS // 2):
        sl = slice(2 * LANES * g, 2 * LANES * (g + 1))
        qg = qm[:, sl]
        rq_n = _seg_rsqrt(qg, seg_ref, MLA_QK)
        qv = qg * q_c2
        if rope:
            qv = qv + qs[:, sl] * q_s2
        q_out[:, sl] = (qv * (rq_n * qscale)).astype(BF16)
        kgp = kn[:, sl]
        rk_n = _seg_rsqrt(kgp + kr2, seg_ref, MLA_QK)
        k_out[:, sl] = ((kgp * kg2 + kr_rot2) * rk_n).astype(BF16)


def _ab_weights(w_in, w_uq, w_ukv, q_head_g, k_head_g, rope):
    sw32 = _swap_idx(MLA_ROPE)
    sw64 = _swap_idx(RET_DK)
    z = lambda n: [-1] * n
    o = AB_OFF
    cols = list(range(o['cq'], o['cq'] + Q_LORA)) + list(range(o['ckv'], o['ckv'] + KV_LORA))
    cols += z(64) + list(range(o['krope'], o['krope'] + MLA_ROPE)) + z(32)
    if rope:
        cols += z(64) + list(o['krope'] + sw32) + z(32)
    cols += list(range(o['ga'], o['ga'] + MLA_WIDTH)) + list(range(o['gb'], o['gb'] + RET_WIDTH))
    cols += list(range(o['rq'], o['rq'] + 256)) + list(range(o['rk'], o['rk'] + 256))
    if rope:
        for base in (o['rq'], o['rk']):
            for hh in range(RET_HEADS):
                cols += list(base + RET_DK * hh + sw64)
    cols += list(range(o['rv'], o['rv'] + RET_WIDTH))
    win = _take_cols(w_in, cols).astype(BF16)

    qc, qsc, kc, vc = [], [], [], []
    for hh in range(MLA_HEADS):
        qb = MLA_QK * hh
        qc += list(range(qb, qb + MLA_QK)) + z(32)
        qsc += z(64) + list(qb + MLA_NOPE + sw32) + z(32)
        kb = (MLA_NOPE + MLA_V) * hh
        kc += list(range(kb, kb + MLA_NOPE)) + z(64)
        vcols = list(range(kb + MLA_NOPE, kb + MLA_NOPE + MLA_V))
        vc += (vcols + z(64)) if hh % 2 == 0 else (z(64) + vcols)
    wq = _take_cols(w_uq, qc).astype(BF16)
    wqs = _take_cols(w_uq, qsc).astype(BF16) if rope else None
    wk = _take_cols(w_ukv, kc).astype(BF16)
    wv = _take_cols(w_ukv, vc).astype(BF16)

    pad = lambda g: jnp.concatenate([g, jnp.zeros((LANES - MLA_QK,), F32)])[None, :]
    gsw = lambda g: jnp.concatenate([jnp.zeros((MLA_NOPE,), F32), g[MLA_NOPE + sw32],
                                     jnp.zeros((LANES - MLA_QK,), F32)])[None, :]
    qg, kg = pad(q_head_g), pad(k_head_g)
    qgs, kgs = (gsw(q_head_g), gsw(k_head_g)) if rope else (None, None)
    return win, wq, wqs, wk, wv, qg, qgs, kg, kgs


def _seg_matrix(width):
    idx = np.arange(2 * LANES) // width
    return jnp.asarray((idx[:, None] == idx[None, :]).astype(np.float32), BF16)


def _prep_ab(x, mod, norm_g, p, rope, tokens_per_batch, mod_row, tables, tm, ctx=None):
    T = x.shape[0]
    win, wq, wqs, wk, wv, qg, qgs, kg, kgs = _ab_weights(
        p['w_in'], p['w_uq'], p['w_ukv'], p['q_head_g'], p['k_head_g'], rope)
    seg = _seg_matrix(LANES)
    blocks_per_batch = max(tokens_per_batch // tm, 1)
    ctx_blocks = 0
    if ctx is not None:
        n_batch = T // tokens_per_batch
        ctx_rows = ctx[0].shape[0] // n_batch
        assert rope and ctx_rows % tm == 0 and tokens_per_batch % tm == 0
        ctx_blocks = ctx_rows // tm
    steps_per_batch = blocks_per_batch + ctx_blocks
    batch = lambda i: i // steps_per_batch
    in_batch = lambda i: jnp.maximum(i % steps_per_batch - ctx_blocks, 0)
    own = lambda i: batch(i) * blocks_per_batch + in_batch(i)
    tok = lambda w: pl.BlockSpec((tm, w), lambda i: (own(i), 0))
    if mod_row is None:
        mod_spec = pl.BlockSpec((1, 1, 3 * D_MODEL), lambda i: (batch(i), 0, 0))
    else:
        mod_spec = pl.BlockSpec((1, 1, 3 * D_MODEL), lambda i: (mod_row, 0, 0))
    vec = lambda a: (a, _const_spec(a.shape))
    ins = [(x, tok(D_MODEL)), (mod, mod_spec), vec(norm_g[None, :]), vec(win),
           vec(p['q_norm_g'][None, :]), vec(p['kv_norm_g'][None, :]), vec(wq)]
    if rope:
        ins.append(vec(wqs))
    ins += [vec(wk), vec(wv), vec(seg), vec(qg)]
    if rope:
        ins.append(vec(qgs))
    ins.append(vec(kg))
    if rope:
        ins.append(vec(kgs))
        tab = lambda a: (a, pl.BlockSpec((tm, LANES), lambda i: (in_batch(i), 0)))
        ins += [tab(tables['cm']), tab(tables['sm']), tab(tables['c64']), tab(tables['s64'])]
    kv_rows, kv_spec = T, tok(1024)
    if ctx_blocks:
        cspec = lambda w: pl.BlockSpec((tm, w), lambda i: (
            batch(i) * ctx_blocks + jnp.minimum(i % steps_per_batch, ctx_blocks - 1), 0))
        ins += [(ctx[0], cspec(KV_LORA)), (ctx[1], cspec(LANES))]
        kv_rows = T + ctx[0].shape[0]
        kv_spec = pl.BlockSpec((tm, 1024), lambda i: (i, 0))
    outs = [(T, 1024, BF16, tok(1024)), (kv_rows, 1024, BF16, kv_spec), (kv_rows, 1024, BF16, kv_spec),
            (T, 1024, BF16, tok(1024)), (T, 256, BF16, tok(256)), (T, 256, BF16, tok(256)),
            (T, 512, BF16, tok(512))]
    if not rope:
        outs += [(T, KV_LORA, F32, tok(KV_LORA)), (T, LANES, F32, tok(LANES))]
    return pl.pallas_call(
        functools.partial(_prep_ab_kernel, rope, ctx_blocks, steps_per_batch),
        grid=(kv_rows // tm,),
        in_specs=[s for _, s in ins],
        out_specs=[o[3] for o in outs],
        out_shape=[jax.ShapeDtypeStruct(o[:2], o[2]) for o in outs],
        compiler_params=_params(1),
        name="prep_ab_rope" if rope else "prep_ab",
    )(*[a for a, _ in ins])


def _mla_attn_kernel(q_ref, k_ref, v_ref, g_ref, o_ref):
    dn = (((1,), (1,)), ((), ()))
    for b in range(q_ref.shape[0]):
        out = None
        for j in range(2):
            sl = slice(LANES * j, LANES * (j + 1))
            s = lax.dot_general(q_ref[b, :, sl], k_ref[b, :, sl], dn, preferred_element_type=F32)
            p = jnp.exp2(s - jnp.max(s, axis=-1, keepdims=True))
            l = jnp.sum(p, axis=-1, keepdims=True)
            oj = jnp.dot(p.astype(BF16), v_ref[b, :, sl], preferred_element_type=F32) / l
            out = oj if out is None else out + oj
        o_ref[b] = (out * g_ref[b].astype(F32)).astype(BF16)


def _mla_attn(q, k, v, gates, bb, tq):
    B, L, _ = q.shape
    Lk = k.shape[1]
    hp = MLA_HEADS // 2
    qspec = pl.BlockSpec((bb, tq, 2 * LANES), lambda b, h, i: (b, i, h))
    kspec = pl.BlockSpec((bb, Lk, 2 * LANES), lambda b, h, i: (b, 0, h))
    ospec = pl.BlockSpec((bb, tq, LANES), lambda b, h, i: (b, i, h))
    return pl.pallas_call(
        _mla_attn_kernel,
        grid=(B // bb, hp, L // tq),
        in_specs=[qspec, kspec, kspec, ospec],
        out_specs=ospec,
        out_shape=jax.ShapeDtypeStruct((B, L, MLA_WIDTH), BF16),
        compiler_params=_params(3),
        name="mla_attn",
    )(q, k, v, gates)


def _ret_kernel(n_chunks, decay_ref, q_ref, k_ref, v_ref, sf_ref, sb_ref, gn_ref, gate_ref,
                o_ref, rf_ref, rb_ref, acc_ref, st_ref, dmask_ref, lane_ref, sdec_ref):
    C = RET_CHUNK
    pair = pl.program_id(0)
    bb = q_ref.shape[0]
    lane_hi = lax.broadcasted_iota(jnp.int32, (C, LANES), 1) >= RET_DK

    @pl.when(pl.program_id(1) == 0)
    def _():
        row = lax.broadcasted_iota(jnp.int32, (C, C), 0).astype(F32)
        rel = row - lax.broadcasted_iota(jnp.int32, (C, C), 1).astype(F32)
        lrow = lax.broadcasted_iota(jnp.int32, (C, LANES), 0).astype(F32)
        srow_hi = lax.broadcasted_iota(jnp.int32, (LANES, 2 * RET_DV), 0) >= RET_DK
        scol_hi = lax.broadcasted_iota(jnp.int32, (LANES, 2 * RET_DV), 1) >= RET_DV
        same_head = srow_hi == scol_hi
        sdec_ref[2] = jnp.where(same_head, 1.0, 0.0)
        for d in range(2):
            log_decay = lambda shape, hh: -jnp.exp(jnp.full(shape, decay_ref[d, 2 * pair + hh], F32))
            for hh in range(2):
                lg = log_decay((C, C), hh)
                if d == 0:
                    dmask_ref[2 * d + hh] = jnp.where(rel >= 0, jnp.exp(jnp.maximum(rel, 0.0) * lg), 0.0)
                else:
                    dmask_ref[2 * d + hh] = jnp.where(rel <= 0, jnp.exp(jnp.maximum(-rel, 0.0) * lg), 0.0)
            lg_lane = jnp.where(lane_hi, log_decay((C, LANES), 1), log_decay((C, LANES), 0))
            if d == 0:
                lane_ref[0] = jnp.exp((C - 1.0 - lrow) * lg_lane)
                lane_ref[2] = jnp.exp((lrow + 1.0) * lg_lane)
            else:
                lane_ref[1] = jnp.exp(lrow * lg_lane)
                lane_ref[3] = jnp.exp((C - lrow) * lg_lane)
            lg_row = jnp.where(srow_hi, log_decay((LANES, 2 * RET_DV), 1), log_decay((LANES, 2 * RET_DV), 0))
            sdec_ref[d] = jnp.where(same_head, jnp.exp(C * lg_row), 0.0)

    for b in range(bb):
        for d, s_ref in ((0, sf_ref), (1, sb_ref)):
            st_ref[b, d] = jnp.zeros((LANES, 2 * RET_DV), F32)
            for hh in range(2):
                st_ref[b, d, RET_DK * hh:RET_DK * (hh + 1), RET_DV * hh:RET_DV * (hh + 1)] = s_ref[b, hh]
    acc_ref[...] = jnp.zeros_like(acc_ref)
    dn = (((1,), (1,)), ((), ()))

    def step(n, _):
        for b in range(bb):
            for d in range(2):
                cidx = n if d == 0 else n_chunks - 1 - n
                r0 = pl.multiple_of(cidx * C, C)
                qc = q_ref[b, pl.ds(r0, C), :].astype(F32)
                kc = k_ref[b, pl.ds(r0, C), :]
                vc = v_ref[b, pl.ds(r0, C), :]
                q2 = jnp.concatenate([jnp.where(lane_hi, 0.0, qc), jnp.where(lane_hi, qc, 0.0)], axis=0)
                s2 = lax.dot_general(q2.astype(BF16), kc, dn, preferred_element_type=F32)
                intra = [jnp.dot((s2[C * hh:C * (hh + 1)] * dmask_ref[2 * d + hh]).astype(BF16),
                                 vc[:, RET_DV * hh:RET_DV * (hh + 1)], preferred_element_type=F32)
                         for hh in range(2)]
                state = st_ref[b, d]
                inter = jnp.dot((qc * lane_ref[2 + d]).astype(BF16), state.astype(BF16),
                                preferred_element_type=F32)
                u = jnp.dot((kc.astype(F32) * lane_ref[d]).T.astype(BF16), vc, preferred_element_type=F32)
                st_ref[b, d] = sdec_ref[d] * state + u * sdec_ref[2]
                acc_ref[b, pl.ds(r0, C), :] += jnp.concatenate(intra, axis=-1) + inter
        return 0

    lax.fori_loop(0, n_chunks, step, 0)

    for b in range(bb):
        for hh in range(2):
            rf_ref[b, hh] = st_ref[b, 0, RET_DK * hh:RET_DK * (hh + 1), RET_DV * hh:RET_DV * (hh + 1)]
            rb_ref[b, hh] = st_ref[b, 1, RET_DK * hh:RET_DK * (hh + 1), RET_DV * hh:RET_DV * (hh + 1)]

    ones = jnp.ones((RET_DV, RET_DV), BF16)

    def norm(n, _):
        r0 = pl.multiple_of(n * C, C)
        for b in range(bb):
            for hh in range(2):
                sl = slice(RET_DV * hh, RET_DV * (hh + 1))
                y = acc_ref[b, pl.ds(r0, C), sl]
                yc = y - jnp.mean(y, axis=-1, keepdims=True)
                var = jnp.dot((yc * yc).astype(BF16), ones, preferred_element_type=F32) * (1.0 / RET_DV)
                out = yc * lax.rsqrt(var + EPS) * gn_ref[:, sl]
                o_ref[b, pl.ds(r0, C), sl] = (out * gate_ref[b, pl.ds(r0, C), sl].astype(F32)).astype(BF16)
        return 0

    lax.fori_loop(0, n_chunks, norm, 0)


def _retention(rq, rk, rv, sf, sb, decay, ret_norm_g, gates, bb):
    B, L, _ = rq.shape
    C = RET_CHUNK
    hp = RET_HEADS // 2
    st_spec = pl.BlockSpec((bb, 2, RET_DK, RET_DV), lambda h, b: (b, h, 0, 0))
    st_shape = jax.ShapeDtypeStruct((B, RET_HEADS, RET_DK, RET_DV), F32)
    qk_spec = pl.BlockSpec((bb, L, LANES), lambda h, b: (b, 0, h))
    v_spec = pl.BlockSpec((bb, L, 2 * RET_DV), lambda h, b: (b, 0, h))
    return pl.pallas_call(
        functools.partial(_ret_kernel, L // C),
        grid=(hp, B // bb),
        in_specs=[pl.BlockSpec(memory_space=pltpu.SMEM), qk_spec, qk_spec, v_spec, st_spec, st_spec,
                  pl.BlockSpec((1, 2 * RET_DV), lambda h, b: (0, h)),
                  pl.BlockSpec((bb, L, 2 * RET_DV), lambda h, b: (b, 0, hp + h))],
        out_specs=[v_spec, st_spec, st_spec],
        out_shape=[jax.ShapeDtypeStruct((B, L, RET_WIDTH), BF16), st_shape, st_shape],
        scratch_shapes=[pltpu.VMEM((bb, L, 2 * RET_DV), F32),
                        pltpu.VMEM((bb, 2, LANES, 2 * RET_DV), F32),
                        pltpu.VMEM((4, C, C), F32),
                        pltpu.VMEM((4, C, LANES), F32),
                        pltpu.VMEM((3, LANES, 2 * RET_DV), F32)],
        compiler_params=_params(2),
        name="retention",
    )(decay, rq, rk, rv, sf, sb, ret_norm_g[None, :], gates)


def _prep_win_kernel(rope, *refs):
    if rope:
        (x_ref, mod_ref, ng_ref, win_ref, seg_ref, exp_ref, qg_ref, qgs_ref, kg_ref, kgs_ref, c_ref, s_ref,
         q_out, k_out, v_out, gate_out) = refs
    else:
        (x_ref, mod_ref, ng_ref, win_ref, seg_ref, exp_ref, qg_ref, kg_ref,
         q_out, k_out, v_out, gate_out, kst_out, vst_out) = refs
    o = WIN_OFF
    h = _modulated(x_ref, mod_ref, ng_ref)
    proj = jnp.dot(h, win_ref[...], preferred_element_type=F32)
    gate_out[...] = _silu(proj[:, o['g']:o['g'] + WIN_WIDTH]).astype(BF16)

    qg2 = jnp.concatenate([qg_ref[...]] * 2, axis=-1)
    kg2 = jnp.concatenate([kg_ref[...]] * 2, axis=-1)
    if rope:
        c2 = jnp.concatenate([c_ref[...]] * 2, axis=-1)
        s2 = jnp.concatenate([s_ref[...]] * 2, axis=-1)
        q_c, q_s = c2 * qg2, s2 * jnp.concatenate([qgs_ref[...]] * 2, axis=-1)
        k_c, k_s = c2 * kg2, s2 * jnp.concatenate([kgs_ref[...]] * 2, axis=-1)
    qscale = WIN_HEAD_DIM ** -0.5 * LOG2E
    for g in range(WIN_WIDTH // 256):
        sl = slice(256 * g, 256 * (g + 1))
        qg = proj[:, sl]
        rn = _seg_rsqrt(qg, seg_ref, WIN_HEAD_DIM) * qscale
        if rope:
            qv = qg * q_c + proj[:, 2560 + 256 * g:2560 + 256 * (g + 1)] * q_s
        else:
            qv = qg * qg2
        q_out[:, sl] = (qv * rn).astype(BF16)
    kraw = proj[:, o['k']:o['k'] + 256]
    rn = _seg_rsqrt(kraw, seg_ref, WIN_HEAD_DIM)
    if rope:
        kn = (kraw * k_c + proj[:, 3584:3584 + 256] * k_s) * rn
    else:
        kn = kraw * kg2 * rn
    v = proj[:, o['v']:o['v'] + 256]
    if not rope:
        kst_out[...] = kn
        vst_out[...] = v
    k_out[...] = jnp.dot(kn.astype(BF16), exp_ref[...], preferred_element_type=F32).astype(BF16)
    v_out[...] = jnp.dot(v.astype(BF16), exp_ref[...], preferred_element_type=F32).astype(BF16)


def _win_expand_matrix():
    e = np.zeros((256, 1024), np.float32)
    for j in range(WIN_KV_HEADS):
        for i in range(WIN_HEAD_DIM):
            e[WIN_HEAD_DIM * j + i, 256 * j + i] = 1.0
            e[WIN_HEAD_DIM * j + i, 256 * j + 192 + i] = 1.0
    return jnp.asarray(e, BF16)


def _prep_win(x, mod, norm_g, p, rope, tokens_per_batch, mod_row, tables, tm):
    T = x.shape[0]
    o = WIN_OFF
    sw64 = _swap_idx(WIN_HEAD_DIM)
    if rope:
        cols = list(range(2560))
        for hh in range(WIN_HEADS):
            cols += list(o['q'] + WIN_HEAD_DIM * hh + sw64)
        for hh in range(WIN_KV_HEADS):
            cols += list(o['k'] + WIN_HEAD_DIM * hh + sw64)
        win = _take_cols(p['w_in'], cols).astype(BF16)
    else:
        win = p['w_in'].astype(BF16)
    seg = _seg_matrix(WIN_HEAD_DIM)
    expand = _win_expand_matrix()
    rep = lambda g: jnp.concatenate([g, g])[None, :]
    blocks_per_batch = tokens_per_batch // tm
    tok = lambda w: pl.BlockSpec((tm, w), lambda i: (i, 0))
    if mod_row is None:
        mod_spec = pl.BlockSpec((1, 1, 3 * D_MODEL), lambda i: (i // blocks_per_batch, 0, 0))
    else:
        mod_spec = pl.BlockSpec((1, 1, 3 * D_MODEL), lambda i: (mod_row, 0, 0))
    vec = lambda a: (a, _const_spec(a.shape))
    ins = [(x, tok(D_MODEL)), (mod, mod_spec), vec(norm_g[None, :]), vec(win), vec(seg), vec(expand),
           vec(rep(p['q_head_g']))]
    if rope:
        ins.append(vec(rep(p['q_head_g'][sw64])))
    ins.append(vec(rep(p['k_head_g'])))
    if rope:
        ins.append(vec(rep(p['k_head_g'][sw64])))
        tab = lambda a: (a, pl.BlockSpec((tm, LANES), lambda i: (i % blocks_per_batch, 0)))
        ins += [tab(tables['c64']), tab(tables['s64'])]
    outs = [(1024, BF16), (1024, BF16), (1024, BF16), (1024, BF16)]
    if not rope:
        outs += [(256, F32), (256, F32)]
    return pl.pallas_call(
        functools.partial(_prep_win_kernel, rope),
        grid=(T // tm,),
        in_specs=[s for _, s in ins],
        out_specs=[tok(w) for w, _ in outs],
        out_shape=[jax.ShapeDtypeStruct((T, w), dt) for w, dt in outs],
        compiler_params=_params(1),
        name="prep_win_rope" if rope else "prep_win",
    )(*[a for a, _ in ins])


def _win_attn_kernel(local, *refs):
    if local:
        sink_ref, q_ref, kf_ref, vf_ref, k_ref, v_ref, g_ref, bias_ref, o_ref = refs
    else:
        sink_ref, q_ref, kf_ref, vf_ref, g_ref, o_ref = refs
    tq = q_ref.shape[1]
    j = pl.program_id(1)
    upper = lax.broadcasted_iota(jnp.int32, (2 * tq, 1), 0) >= tq
    if local:
        span = bias_ref.shape[2]
        start = jnp.clip(pl.program_id(2) * tq - WINDOW, 0, k_ref.shape[1] - span)
        start = pl.multiple_of(start, WINDOW)
    dn = (((1,), (1,)), ((), ()))
    for b in range(q_ref.shape[0]):
        q2 = jnp.concatenate([q_ref[b, :, 0:LANES], q_ref[b, :, LANES:2 * LANES]], axis=0)
        out = None
        for half in range(2):
            sl = slice(LANES * half, LANES * (half + 1))
            sk = jnp.where(upper, sink_ref[4 * j + 2 + half], sink_ref[4 * j + half]) * LOG2E
            s_ctx = lax.dot_general(q2, kf_ref[b, :, sl], dn, preferred_element_type=F32)
            m = jnp.maximum(jnp.max(s_ctx, axis=-1, keepdims=True), sk)
            if local:
                s_loc = lax.dot_general(q2, k_ref[b, pl.ds(start, span), sl], dn,
                                        preferred_element_type=F32) + bias_ref[0]
                m = jnp.maximum(m, jnp.max(s_loc, axis=-1, keepdims=True))
            e_ctx = jnp.exp2(s_ctx - m)
            den = jnp.sum(e_ctx, axis=-1, keepdims=True) + jnp.exp2(sk - m)
            acc = jnp.dot(e_ctx.astype(BF16), vf_ref[b, :, sl], preferred_element_type=F32)
            if local:
                e_loc = jnp.exp2(s_loc - m)
                den = den + jnp.sum(e_loc, axis=-1, keepdims=True)
                acc = acc + jnp.dot(e_loc.astype(BF16), v_ref[b, pl.ds(start, span), sl],
                                    preferred_element_type=F32)
            oh = acc / den
            out = oh if out is None else out + oh
        gate = g_ref[b].astype(F32)
        o_ref[b, :, 0:LANES] = (out[0:tq] * gate[:, 0:LANES]).astype(BF16)
        o_ref[b, :, LANES:2 * LANES] = (out[tq:2 * tq] * gate[:, LANES:2 * LANES]).astype(BF16)


def _window_bias(tq):
    span = tq + 2 * WINDOW
    qi = jnp.arange(2 * tq)[:, None] % tq
    kk = jnp.arange(span)[None, :]
    cases = [jnp.abs(kk - off - qi) <= WINDOW for off in (0, WINDOW, 2 * WINDOW)]
    return jnp.where(jnp.stack(cases), 0.0, NEG).astype(F32)


def _win_attn(q, kf, vf, own, gates, sink, bb, tq):
    B, L, _ = q.shape
    nb = L // tq
    qspec = pl.BlockSpec((bb, tq, 2 * LANES), lambda b, h, i: (b, i, h))
    fspec = pl.BlockSpec((bb, kf.shape[1], 2 * LANES), lambda b, h, i: (b, 0, h))
    in_specs = [pl.BlockSpec(memory_space=pltpu.SMEM), qspec, fspec, fspec]
    args = [sink, q, kf, vf]
    if own is not None:
        assert nb >= 3 and bb == 1
        kspec = pl.BlockSpec((bb, L, 2 * LANES), lambda b, h, i: (b, 0, h))
        in_specs += [kspec, kspec]
        args += list(own)
    in_specs.append(qspec)
    args.append(gates)
    if own is not None:
        bias = _window_bias(tq)
        in_specs.append(pl.BlockSpec(
            (1,) + bias.shape[1:],
            lambda b, h, i: (jnp.where(i == 0, 0, jnp.where(i == nb - 1, 2, 1)), 0, 0)))
        args.append(bias)
    return pl.pallas_call(
        functools.partial(_win_attn_kernel, own is not None),
        grid=(B // bb, WIN_KV_HEADS, nb),
        in_specs=in_specs,
        out_specs=qspec,
        out_shape=jax.ShapeDtypeStruct((B, L, WIN_WIDTH), BF16),
        compiler_params=_params(3),
        name="win_attn_local" if own is not None else "win_attn",
    )(*args)


def _out_kernel(n_parts, *refs):
    x_ref, mod_ref = refs[0], refs[1]
    parts = refs[2:2 + 2 * n_parts]
    o_ref = refs[2 + 2 * n_parts]
    y = None
    for i in range(n_parts):
        t = jnp.dot(parts[2 * i][...], parts[2 * i + 1][...], preferred_element_type=F32)
        y = t if y is None else y + t
    gate = mod_ref[0, :, 2 * D_MODEL:3 * D_MODEL]
    o_ref[...] = x_ref[...] + gate * y


def _out_proj(x, mod, parts, tokens_per_batch, mod_row, tm):
    T = x.shape[0]
    blocks_per_batch = tokens_per_batch // tm
    tok = lambda w: pl.BlockSpec((tm, w), lambda i: (i, 0))
    if mod_row is None:
        mod_spec = pl.BlockSpec((1, 1, 3 * D_MODEL), lambda i: (i // blocks_per_batch, 0, 0))
    else:
        mod_spec = pl.BlockSpec((1, 1, 3 * D_MODEL), lambda i: (mod_row, 0, 0))
    in_specs = [tok(D_MODEL), mod_spec]
    args = [x, mod]
    for a, w in parts:
        in_specs += [tok(a.shape[1]), _const_spec(w.shape)]
        args += [a, w]
    return pl.pallas_call(
        functools.partial(_out_kernel, len(parts)),
        grid=(T // tm,),
        in_specs=in_specs,
        out_specs=tok(D_MODEL),
        out_shape=jax.ShapeDtypeStruct((T, D_MODEL), F32),
        compiler_params=_params(1),
        name="out_proj",
    )(*args)


def kernel(x_prompt, x_sample, cache_l0_mla_ckv, cache_l0_mla_krope, state_l0_ret_fwd, state_l0_ret_bwd, cache_l1_win_k, cache_l1_win_v, cache_l2_mla_ckv, cache_l2_mla_krope, state_l2_ret_fwd, state_l2_ret_bwd, cache_l3_win_k, cache_l3_win_v, c, c_ctx, ada_w, ada_b, norm_g, ab_w_in, mla_q_norm_g, mla_w_uq, mla_kv_norm_g, mla_w_ukv, mla_q_head_g, mla_k_head_g, ret_decay, ret_norm_g, ab_w_out, win_w_in, win_q_head_g, win_k_head_g, win_sink, win_w_out):
    BP, LP, D = x_prompt.shape
    BS, LS, _ = x_sample.shape
    P = cache_l0_mla_ckv.shape[1]
    ctx_caches = ((cache_l0_mla_ckv, cache_l0_mla_krope, state_l0_ret_fwd, state_l0_ret_bwd),
                  (cache_l1_win_k, cache_l1_win_v),
                  (cache_l2_mla_ckv, cache_l2_mla_krope, state_l2_ret_fwd, state_l2_ret_bwd),
                  (cache_l3_win_k, cache_l3_win_v))

    cond = jnp.concatenate([c, c_ctx[None, :], jnp.zeros((8 - BS - 1, D), F32)], axis=0)
    mod_all = _ada_mod(cond, ada_w, ada_b).reshape(DEPTH, 8, 1, 3 * D)
    ctx_row = BS

    c32, s32 = _rope_tables(LS, MLA_ROPE)
    c64, s64 = _rope_tables(LS, RET_DK)
    ones = jnp.ones((LS, MLA_NOPE), F32)
    zeros = jnp.zeros((LS, MLA_NOPE), F32)
    z32 = jnp.zeros((LS, LANES - MLA_QK), F32)
    tables = dict(cm=jnp.concatenate([ones, c32, z32], -1), sm=jnp.concatenate([zeros, s32, z32], -1),
                  c64=jnp.concatenate([c64, c64], -1), s64=jnp.concatenate([s64, s64], -1))

    y_p = x_prompt.reshape(BP * LP, D)
    y_s = x_sample.reshape(BS * LS, D)
    tm = 512
    new_state = []
    for l in range(DEPTH):
        i = l // 2
        mod = mod_all[l]
        if l % 2 == 0:
            p = {'w_in': ab_w_in[i], 'q_norm_g': mla_q_norm_g[i], 'w_uq': mla_w_uq[i],
                 'kv_norm_g': mla_kv_norm_g[i], 'w_ukv': mla_w_ukv[i], 'q_head_g': mla_q_head_g[i],
                 'k_head_g': mla_k_head_g[i]}
            w_out = ab_w_out[i].astype(BF16)
            parts_w = (w_out[:MLA_WIDTH], w_out[MLA_WIDTH:])
            ckv_c, krope_c, sf, sb = ctx_caches[l]

            q, k, v, gates, rq, rk, rv, ckv, kr = _prep_ab(y_p, mod, norm_g[l], p, False, LP, ctx_row, None, tm)
            r3 = lambda a: a.reshape(BP, LP, a.shape[-1])
            a_p = _mla_attn(r3(q), r3(k), r3(v), r3(gates), PROMPT_ROWS_PER_STEP, LP)
            zst = jnp.zeros((BP, RET_HEADS, RET_DK, RET_DV), F32)
            r_p, rf, rb = _retention(r3(rq), r3(rk), r3(rv), zst, zst, ret_decay[i], ret_norm_g[i], r3(gates),
                                     PROMPT_ROWS_PER_STEP)
            new_state.append((ckv.reshape(BP, LP, KV_LORA),
                              kr[:, MLA_NOPE:MLA_QK].reshape(BP, LP, MLA_ROPE), rf, rb))
            y_p_next = _out_proj(y_p, mod, [(a_p.reshape(BP * LP, -1), parts_w[0]),
                                            (r_p.reshape(BP * LP, -1), parts_w[1])], LP, ctx_row, tm)

            kr_c = jnp.pad(krope_c.reshape(BS * P, MLA_ROPE), ((0, 0), (MLA_NOPE, LANES - MLA_QK)))
            q, k, v, gates, rq, rk, rv = _prep_ab(y_s, mod, norm_g[l], p, True, LS, None, tables, tm,
                                                  (ckv_c.reshape(BS * P, KV_LORA), kr_c))
            r3 = lambda a: a.reshape(BS, -1, a.shape[-1])
            a_s = _mla_attn(r3(q), r3(k), r3(v), r3(gates), 1, 256)
            r_s, _, _ = _retention(r3(rq), r3(rk), r3(rv), sf, sb, ret_decay[i], ret_norm_g[i], r3(gates), 1)
            y_s_next = _out_proj(y_s, mod, [(a_s.reshape(BS * LS, -1), parts_w[0]),
                                            (r_s.reshape(BS * LS, -1), parts_w[1])], LS, None, tm)
        else:
            p = {'w_in': win_w_in[i], 'q_head_g': win_q_head_g[i], 'k_head_g': win_k_head_g[i]}
            w_out = win_w_out[i].astype(BF16)
            sink = win_sink[i]
            kc, vc = ctx_caches[l]

            q, k, v, gates, kst, vst = _prep_win(y_p, mod, norm_g[l], p, False, LP, ctx_row, None, tm)
            r3 = lambda a: a.reshape(BP, LP, a.shape[-1])
            o_p = _win_attn(r3(q), r3(k), r3(v), None, r3(gates), sink, PROMPT_ROWS_PER_STEP, LP)
            new_state.append((kst.reshape(BP, LP, WIN_KV_HEADS, WIN_HEAD_DIM),
                              vst.reshape(BP, LP, WIN_KV_HEADS, WIN_HEAD_DIM)))
            y_p_next = _out_proj(y_p, mod, [(o_p.reshape(BP * LP, -1), w_out)], LP, ctx_row, tm)

            q, k, v, gates = _prep_win(y_s, mod, norm_g[l], p, True, LS, None, tables, tm)
            r3 = lambda a: a.reshape(BS, -1, a.shape[-1])
            def slots(a):
                a = a.astype(BF16)
                z = jnp.zeros(a.shape[:-1] + (2 * WIN_HEAD_DIM,), BF16)
                return jnp.concatenate([a, z, a], axis=-1).reshape(BS, P, 4 * 2 * LANES)

            o_s = _win_attn(r3(q), slots(kc), slots(vc), (r3(k), r3(v)), r3(gates), sink, 1, 256)
            y_s_next = _out_proj(y_s, mod, [(o_s.reshape(BS * LS, -1), w_out)], LS, None, tm)
        y_p, y_s = y_p_next, y_s_next

    (l0_ckv, l0_krope, l0_rf, l0_rb), (l1_k, l1_v), (l2_ckv, l2_krope, l2_rf, l2_rb), (l3_k, l3_v) = new_state
    return (y_p.reshape(BP, LP, D), y_s.reshape(BS, LS, D), l0_ckv, l0_krope, l0_rf, l0_rb, l1_k, l1_v,
            l2_ckv, l2_krope, l2_rf, l2_rb, l3_k, l3_v)
```

```python
import functools

import numpy as np
import jax
import jax.numpy as jnp
from jax import lax
from jax.experimental import pallas as pl
from jax.experimental.pallas import tpu as pltpu

D_MODEL = 1024
DEPTH = 4
GRID_W = 64
ROPE_BASE = 10000.0
EPS = 1e-6

MLA_HEADS = 8
MLA_NOPE = 64
MLA_ROPE = 32
MLA_QK = MLA_NOPE + MLA_ROPE
MLA_V = 64
Q_LORA = 384
KV_LORA = 256
MLA_WIDTH = MLA_HEADS * MLA_V

RET_HEADS = 4
RET_DK = 64
RET_DV = 128
RET_CHUNK = 256
RET_WIDTH = RET_HEADS * RET_DV

WIN_HEADS = 16
WIN_KV_HEADS = 4
WIN_HEAD_DIM = 64
WINDOW = 128
WIN_WIDTH = WIN_HEADS * WIN_HEAD_DIM

LANES = 128
F32 = jnp.float32
BF16 = jnp.bfloat16
NEG = -1e30
LOG2E = 1.4426950408889634
VMEM_LIMIT = 52 * 1024 * 1024
PROMPT_ROWS_PER_STEP = 4
MLA_KEY_CHUNK = 512

AB_OFF = dict(cq=0, ckv=384, krope=640, ga=672, rq=1184, rk=1440, rv=1696, gb=2208)
WIN_OFF = dict(q=0, k=1024, v=1280, g=1536)


def _params(n_axes):
    return pltpu.CompilerParams(dimension_semantics=("arbitrary",) * n_axes, vmem_limit_bytes=VMEM_LIMIT)


def _const_spec(shape):
    nd = len(shape)
    return pl.BlockSpec(shape, lambda *_: (0,) * nd)


def _swap_idx(dim):
    d2 = dim // 2
    half = d2 // 2
    one = np.concatenate([np.arange(half, d2), np.arange(0, half)])
    return np.concatenate([one, d2 + one])


def _take_cols(w, cols):
    cols = np.asarray(cols, np.int32)
    picked = jnp.take(w.astype(BF16), jnp.asarray(np.maximum(cols, 0)), axis=1)
    return jnp.where(jnp.asarray(cols >= 0)[None, :], picked, jnp.zeros((), BF16))


def _rope_tables(n_tokens, dim):
    t = jnp.arange(n_tokens)
    rows = (t // GRID_W).astype(F32)
    cols = (t % GRID_W).astype(F32)
    half = dim // 4
    inv = jnp.power(jnp.float32(ROPE_BASE), -jnp.arange(half, dtype=F32) / half)

    def one(pos):
        ang = pos[:, None] * inv[None, :]
        c, s = jnp.cos(ang), jnp.sin(ang)
        return jnp.concatenate([c, c], -1), jnp.concatenate([-s, s], -1)

    cr, sr = one(rows)
    cc, sc = one(cols)
    return jnp.concatenate([cr, cc], -1), jnp.concatenate([sr, sc], -1)


def _ada_kernel(cond_ref, w_ref, b_ref, o_ref):
    c = cond_ref[...]
    sc = (c * jax.nn.sigmoid(c)).astype(BF16)
    o_ref[0] = jnp.dot(sc, w_ref[0].astype(BF16), preferred_element_type=F32) + b_ref[0]


def _ada_mod(cond, ada_w, ada_b):
    tn = 768
    n3 = 3 * D_MODEL
    return pl.pallas_call(
        _ada_kernel,
        grid=(DEPTH, n3 // tn),
        in_specs=[pl.BlockSpec((8, D_MODEL), lambda l, j: (0, 0)),
                  pl.BlockSpec((1, D_MODEL, tn), lambda l, j: (l, 0, j)),
                  pl.BlockSpec((1, 1, tn), lambda l, j: (l, 0, j))],
        out_specs=pl.BlockSpec((1, 8, tn), lambda l, j: (l, 0, j)),
        out_shape=jax.ShapeDtypeStruct((DEPTH, 8, n3), F32),
        compiler_params=_params(2),
        name="ada_mod",
    )(cond, ada_w, ada_b.reshape(DEPTH, 1, n3))


def _rms(x, g):
    return x * lax.rsqrt(jnp.mean(x * x, axis=-1, keepdims=True) + EPS) * g


def _modulated(x_ref, mod_ref, ng_ref):
    x = x_ref[...]
    shift = mod_ref[0, :, 0:D_MODEL]
    scale = mod_ref[0, :, D_MODEL:2 * D_MODEL]
    return (_rms(x, ng_ref[...]) * (1.0 + scale) + shift).astype(BF16)


def _seg_rsqrt(x, seg_ref, n_real):
    ssq = jnp.dot((x * x).astype(BF16), seg_ref[...], preferred_element_type=F32)
    return lax.rsqrt(ssq * (1.0 / n_real) + EPS)


def _silu(g):
    return g * jax.nn.sigmoid(g)


def _ones_lane(head):
    return MLA_V if head % 2 == 0 else 0


def _with_ones_lane(v):
    lane = lax.broadcasted_iota(jnp.int32, v.shape, 1)
    odd = (lane // LANES) % 2 == 1
    return jnp.where(lane % LANES == jnp.where(odd, _ones_lane(1), _ones_lane(0)), 1.0, v)


def _ctx_kv(ckv_ref, kr_ref, wk_ref, wv_ref, seg_ref, kg_ref, k_out, v_out):
    ckv = ckv_ref[...].astype(BF16)
    kn = jnp.dot(ckv, wk_ref[...], preferred_element_type=F32)
    v_out[...] = _with_ones_lane(jnp.dot(ckv, wv_ref[...], preferred_element_type=F32)).astype(BF16)
    kr = kr_ref[...]
    kg = kg_ref[...]
    kr2 = jnp.concatenate([kr, kr], axis=-1)
    kg2 = jnp.concatenate([kg, kg], axis=-1)
    for g in range(MLA_HEADS // 2):
        sl = slice(2 * LANES * g, 2 * LANES * (g + 1))
        kraw = kn[:, sl] + kr2
        k_out[:, sl] = (kraw * kg2 * _seg_rsqrt(kraw, seg_ref, MLA_QK)).astype(BF16)


def _prep_ab_kernel(rope, ctx_blocks, steps_per_batch, *refs):
    if not ctx_blocks:
        _prep_ab_body(rope, *refs)
        return
    n_in = 19
    ckvc_ref, krc_ref = refs[n_in:n_in + 2]
    body_refs = refs[:n_in] + refs[n_in + 2:]
    wk_ref, wv_ref, seg_ref, kg_ref = refs[8], refs[9], refs[10], refs[13]
    k_out, v_out = body_refs[n_in + 1], body_refs[n_in + 2]
    is_ctx = pl.program_id(0) % steps_per_batch < ctx_blocks

    @pl.when(is_ctx)
    def _():
        _ctx_kv(ckvc_ref, krc_ref, wk_ref, wv_ref, seg_ref, kg_ref, k_out, v_out)

    @pl.when(jnp.logical_not(is_ctx))
    def _():
        _prep_ab_body(rope, *body_refs)


def _prep_ab_body(rope, *refs):
    if rope:
        (x_ref, mod_ref, ng_ref, win_ref, qng_ref, kvng_ref, wq_ref, wqs_ref, wk_ref, wv_ref, seg_ref,
         qg_ref, qgs_ref, kg_ref, kgs_ref, cm_ref, sm_ref, cr_ref, sr_ref,
         q_out, k_out, v_out, gate_out, rq_out, rk_out, rv_out) = refs
        o_kr, o_g, o_rq, o_rk, o_rv = 640, 896, 1920, 2176, 2944
    else:
        (x_ref, mod_ref, ng_ref, win_ref, qng_ref, kvng_ref, wq_ref, wk_ref, wv_ref, seg_ref,
         qg_ref, kg_ref,
         q_out, k_out, v_out, gate_out, rq_out, rk_out, rv_out, ckv_out, kr_out) = refs
        o_kr, o_g, o_rq, o_rk, o_rv = 640, 768, 1792, 2048, 2304

    h = _modulated(x_ref, mod_ref, ng_ref)
    proj = jnp.dot(h, win_ref[...], preferred_element_type=F32)

    cqn = _rms(proj[:, 0:Q_LORA], qng_ref[...]).astype(BF16)
    ckvn_f = _rms(proj[:, Q_LORA:Q_LORA + KV_LORA], kvng_ref[...])
    ckvn = ckvn_f.astype(BF16)
    kr = proj[:, o_kr:o_kr + LANES]

    gate_out[...] = _silu(proj[:, o_g:o_g + 2 * MLA_WIDTH]).astype(BF16)
    rv_out[...] = proj[:, o_rv:o_rv + RET_WIDTH].astype(BF16)

    qm = jnp.dot(cqn, wq_ref[...], preferred_element_type=F32)
    kn = jnp.dot(ckvn, wk_ref[...], preferred_element_type=F32)
    v_out[...] = _with_ones_lane(jnp.dot(ckvn, wv_ref[...], preferred_element_type=F32)).astype(BF16)

    qscale = MLA_QK ** -0.5 * LOG2E
    if rope:
        qs = jnp.dot(cqn, wqs_ref[...], preferred_element_type=F32)
        krs = proj[:, 768:768 + LANES]
        cm, sm = cm_ref[...], sm_ref[...]
        q_c = cm * qg_ref[...]
        q_s = sm * qgs_ref[...]
        kr_rot = kr * (cm * kg_ref[...]) + krs * (sm * kgs_ref[...])
        cr, sr = cr_ref[...], sr_ref[...]
        rq = proj[:, o_rq:o_rq + 256]
        rk = proj[:, o_rk:o_rk + 256]
        rqs = proj[:, 2432:2432 + 256]
        rks = proj[:, 2688:2688 + 256]
        for c in range(2):
            sl = slice(LANES * c, LANES * (c + 1))
            rq_out[:, sl] = ((rq[:, sl] * cr + rqs[:, sl] * sr) * (RET_DK ** -0.5)).astype(BF16)
            rk_out[:, sl] = (rk[:, sl] * cr + rks[:, sl] * sr).astype(BF16)
    else:
        q_c = jnp.broadcast_to(qg_ref[...], (qm.shape[0], LANES))
        kr_rot = kr * kg_ref[...]
        rq_out[...] = (proj[:, o_rq:o_rq + 256] * (RET_DK ** -0.5)).astype(BF16)
        rk_out[...] = proj[:, o_rk:o_rk + 256].astype(BF16)
        ckv_out[...] = ckvn_f
        kr_out[...] = kr

    kg = kg_ref[...]
    q_c2 = jnp.concatenate([q_c, q_c], axis=-1)
    kr2 = jnp.concatenate([kr, kr], axis=-1)
    kr_rot2 = jnp.concatenate([kr_rot, kr_rot], axis=-1)
    kg2 = jnp.concatenate([kg, kg], axis=-1)
    if rope:
        q_s2 = jnp.concatenate([q_s, q_s], axis=-1)
    for g in range(MLA_HEADS // 2):
        sl = slice(2 * LANES * g, 2 * LANES * (g + 1))
        qg = qm[:, sl]
        rq_n = _seg_rsqrt(qg, seg_ref, MLA_QK)
        qv = qg * q_c2
        if rope:
            qv = qv + qs[:, sl] * q_s2
        q_out[:, sl] = (qv * (rq_n * qscale)).astype(BF16)
        kgp = kn[:, sl]
        rk_n = _seg_rsqrt(kgp + kr2, seg_ref, MLA_QK)
        k_out[:, sl] = ((kgp * kg2 + kr_rot2) * rk_n).astype(BF16)


def _ab_weights(w_in, w_uq, w_ukv, q_head_g, k_head_g, rope):
    sw32 = _swap_idx(MLA_ROPE)
    sw64 = _swap_idx(RET_DK)
    z = lambda n: [-1] * n
    o = AB_OFF
    cols = list(range(o['cq'], o['cq'] + Q_LORA)) + list(range(o['ckv'], o['ckv'] + KV_LORA))
    cols += z(64) + list(range(o['krope'], o['krope'] + MLA_ROPE)) + z(32)
    if rope:
        cols += z(64) + list(o['krope'] + sw32) + z(32)
    cols += list(range(o['ga'], o['ga'] + MLA_WIDTH)) + list(range(o['gb'], o['gb'] + RET_WIDTH))
    cols += list(range(o['rq'], o['rq'] + 256)) + list(range(o['rk'], o['rk'] + 256))
    if rope:
        for base in (o['rq'], o['rk']):
            for hh in range(RET_HEADS):
                cols += list(base + RET_DK * hh + sw64)
    cols += list(range(o['rv'], o['rv'] + RET_WIDTH))
    win = _take_cols(w_in, cols).astype(BF16)

    qc, qsc, kc, vc = [], [], [], []
    for hh in range(MLA_HEADS):
        qb = MLA_QK * hh
        qc += list(range(qb, qb + MLA_QK)) + z(32)
        qsc += z(64) + list(qb + MLA_NOPE + sw32) + z(32)
        kb = (MLA_NOPE + MLA_V) * hh
        kc += list(range(kb, kb + MLA_NOPE)) + z(64)
        vcols = list(range(kb + MLA_NOPE, kb + MLA_NOPE + MLA_V))
        vc += (vcols + z(64)) if hh % 2 == 0 else (z(64) + vcols)
    wq = _take_cols(w_uq, qc).astype(BF16)
    wqs = _take_cols(w_uq, qsc).astype(BF16) if rope else None
    wk = _take_cols(w_ukv, kc).astype(BF16)
    wv = _take_cols(w_ukv, vc).astype(BF16)

    pad = lambda g: jnp.concatenate([g, jnp.zeros((LANES - MLA_QK,), F32)])[None, :]
    gsw = lambda g: jnp.concatenate([jnp.zeros((MLA_NOPE,), F32), g[MLA_NOPE + sw32],
                                     jnp.zeros((LANES - MLA_QK,), F32)])[None, :]
    qg, kg = pad(q_head_g), pad(k_head_g)
    qgs, kgs = (gsw(q_head_g), gsw(k_head_g)) if rope else (None, None)
    return win, wq, wqs, wk, wv, qg, qgs, kg, kgs


def _seg_matrix(width):
    idx = np.arange(2 * LANES) // width
    return jnp.asarray((idx[:, None] == idx[None, :]).astype(np.float32), BF16)


def _prep_ab(x, mod, norm_g, p, rope, tokens_per_batch, mod_row, tables, tm, ctx=None):
    T = x.shape[0]
    win, wq, wqs, wk, wv, qg, qgs, kg, kgs = _ab_weights(
        p['w_in'], p['w_uq'], p['w_ukv'], p['q_head_g'], p['k_head_g'], rope)
    seg = _seg_matrix(LANES)
    blocks_per_batch = max(tokens_per_batch // tm, 1)
    ctx_blocks = 0
    if ctx is not None:
        n_batch = T // tokens_per_batch
        ctx_rows = ctx[0].shape[0] // n_batch
        assert rope and ctx_rows % tm == 0 and tokens_per_batch % tm == 0
        ctx_blocks = ctx_rows // tm
    steps_per_batch = blocks_per_batch + ctx_blocks
    batch = lambda i: i // steps_per_batch
    in_batch = lambda i: jnp.maximum(i % steps_per_batch - ctx_blocks, 0)
    own = lambda i: batch(i) * blocks_per_batch + in_batch(i)
    tok = lambda w: pl.BlockSpec((tm, w), lambda i: (own(i), 0))
    if mod_row is None:
        mod_spec = pl.BlockSpec((1, 1, 3 * D_MODEL), lambda i: (batch(i), 0, 0))
    else:
        mod_spec = pl.BlockSpec((1, 1, 3 * D_MODEL), lambda i: (mod_row, 0, 0))
    vec = lambda a: (a, _const_spec(a.shape))
    ins = [(x, tok(D_MODEL)), (mod, mod_spec), vec(norm_g[None, :]), vec(win),
           vec(p['q_norm_g'][None, :]), vec(p['kv_norm_g'][None, :]), vec(wq)]
    if rope:
        ins.append(vec(wqs))
    ins += [vec(wk), vec(wv), vec(seg), vec(qg)]
    if rope:
        ins.append(vec(qgs))
    ins.append(vec(kg))
    if rope:
        ins.append(vec(kgs))
        tab = lambda a: (a, pl.BlockSpec((tm, LANES), lambda i: (in_batch(i), 0)))
        ins += [tab(tables['cm']), tab(tables['sm']), tab(tables['c64']), tab(tables['s64'])]
    kv_rows, kv_spec = T, tok(1024)
    if ctx_blocks:
        cspec = lambda w: pl.BlockSpec((tm, w), lambda i: (
            batch(i) * ctx_blocks + jnp.minimum(i % steps_per_batch, ctx_blocks - 1), 0))
        ins += [(ctx[0], cspec(KV_LORA)), (ctx[1], cspec(LANES))]
        kv_rows = T + ctx[0].shape[0]
        kv_spec = pl.BlockSpec((tm, 1024), lambda i: (i, 0))
    outs = [(T, 1024, BF16, tok(1024)), (kv_rows, 1024, BF16, kv_spec), (kv_rows, 1024, BF16, kv_spec),
            (T, 1024, BF16, tok(1024)), (T, 256, BF16, tok(256)), (T, 256, BF16, tok(256)),
            (T, 512, BF16, tok(512))]
    if not rope:
        outs += [(T, KV_LORA, F32, tok(KV_LORA)), (T, LANES, F32, tok(LANES))]
    return pl.pallas_call(
        functools.partial(_prep_ab_kernel, rope, ctx_blocks, steps_per_batch),
        grid=(kv_rows // tm,),
        in_specs=[s for _, s in ins],
        out_specs=[o[3] for o in outs],
        out_shape=[jax.ShapeDtypeStruct(o[:2], o[2]) for o in outs],
        compiler_params=_params(1),
        name="prep_ab_rope" if rope else "prep_ab",
    )(*[a for a, _ in ins])


def _lane_groups(x, op):
    out = x[:, 0:LANES]
    for t in range(1, x.shape[1] // LANES):
        out = op(out, x[:, LANES * t:LANES * (t + 1)])
    return out


def _mla_attn_kernel(tk, q_ref, k_ref, v_ref, g_ref, o_ref, s_ref):
    dn = (((1,), (1,)), ((), ()))
    tq = q_ref.shape[1]
    nk = k_ref.shape[1] // tk
    heads = [(b, j) for b in range(q_ref.shape[0]) for j in range(q_ref.shape[2] // LANES)]
    m_prev, out = None, None
    for idx in range(len(heads) + 1):
        cur = heads[idx] if idx < len(heads) else None
        prev = heads[idx - 1] if idx > 0 else None
        if cur is not None:
            cb, cj = cur
            csl = slice(LANES * cj, LANES * (cj + 1))
            q = q_ref[cb, :, csl]
            mt = None
        if prev is not None:
            pb, pj = prev
            psl = slice(LANES * pj, LANES * (pj + 1))
            acc = jnp.zeros((tq, LANES), F32)
        for c in range(nk):
            rows = slice(c * tk, (c + 1) * tk)
            if cur is not None:
                s = lax.dot_general(q, k_ref[cb, rows, csl], dn, preferred_element_type=F32)
                s_ref[idx % 2, :, rows] = s
                smax = _lane_groups(s, jnp.maximum)
                mt = smax if mt is None else jnp.maximum(mt, smax)
            if prev is not None:
                p = jnp.exp2(s_ref[(idx - 1) % 2, :, rows] - m_prev)
                acc = acc + jnp.dot(p.astype(BF16), v_ref[pb, rows, psl], preferred_element_type=F32)
        if prev is not None:
            lane = lax.broadcasted_iota(jnp.int32, (tq, LANES), 1)
            l = jnp.sum(jnp.where(lane == _ones_lane(pj), acc, 0.0), axis=-1, keepdims=True)
            oj = jnp.where((lane >= MLA_V) == (pj % 2 == 1), acc, 0.0) / l
            out = oj if out is None else out + oj
            if pj % 2 == 1:
                osl = slice(LANES * (pj // 2), LANES * (pj // 2 + 1))
                o_ref[pb, :, osl] = (out * g_ref[pb, :, osl].astype(F32)).astype(BF16)
                out = None
        if cur is not None:
            m_prev = jnp.max(mt, axis=-1, keepdims=True)


def _mla_attn(q, k, v, gates, bb, tq, pairs):
    B, L, _ = q.shape
    Lk = k.shape[1]
    hp = MLA_HEADS // 2 // pairs
    qspec = pl.BlockSpec((bb, tq, 2 * LANES * pairs), lambda b, h, i: (b, i, h))
    kspec = pl.BlockSpec((bb, Lk, 2 * LANES * pairs), lambda b, h, i: (b, 0, h))
    ospec = pl.BlockSpec((bb, tq, LANES * pairs), lambda b, h, i: (b, i, h))
    tk = min(Lk, MLA_KEY_CHUNK)
    assert Lk % tk == 0
    return pl.pallas_call(
        functools.partial(_mla_attn_kernel, tk),
        grid=(B // bb, hp, L // tq),
        in_specs=[qspec, kspec, kspec, ospec],
        out_specs=ospec,
        out_shape=jax.ShapeDtypeStruct((B, L, MLA_WIDTH), BF16),
        scratch_shapes=[pltpu.VMEM((2, tq, Lk), F32)],
        compiler_params=_params(3),
        name="mla_attn",
    )(q, k, v, gates)


def _ret_kernel(n_chunks, decay_ref, q_ref, k_ref, v_ref, sf_ref, sb_ref, gn_ref, gate_ref,
                o_ref, rf_ref, rb_ref, acc_ref, st_ref, dmask_ref, lane_ref, sdec_ref):
    C = RET_CHUNK
    pair = pl.program_id(0)
    bb = q_ref.shape[0]
    lane_hi = lax.broadcasted_iota(jnp.int32, (C, LANES), 1) >= RET_DK

    @pl.when(pl.program_id(1) == 0)
    def _():
        row = lax.broadcasted_iota(jnp.int32, (C, C), 0).astype(F32)
        rel = row - lax.broadcasted_iota(jnp.int32, (C, C), 1).astype(F32)
        lrow = lax.broadcasted_iota(jnp.int32, (C, LANES), 0).astype(F32)
        srow_hi = lax.broadcasted_iota(jnp.int32, (LANES, 2 * RET_DV), 0) >= RET_DK
        scol_hi = lax.broadcasted_iota(jnp.int32, (LANES, 2 * RET_DV), 1) >= RET_DV
        same_head = srow_hi == scol_hi
        sdec_ref[2] = jnp.where(same_head, 1.0, 0.0)
        for d in range(2):
            log_decay = lambda shape, hh: -jnp.exp(jnp.full(shape, decay_ref[d, 2 * pair + hh], F32))
            for hh in range(2):
                lg = log_decay((C, C), hh)
                if d == 0:
                    dmask_ref[2 * d + hh] = jnp.where(rel >= 0, jnp.exp(jnp.maximum(rel, 0.0) * lg), 0.0)
                else:
                    dmask_ref[2 * d + hh] = jnp.where(rel <= 0, jnp.exp(jnp.maximum(-rel, 0.0) * lg), 0.0)
            lg_lane = jnp.where(lane_hi, log_decay((C, LANES), 1), log_decay((C, LANES), 0))
            if d == 0:
                lane_ref[0] = jnp.exp((C - 1.0 - lrow) * lg_lane)
                lane_ref[2] = jnp.exp((lrow + 1.0) * lg_lane)
            else:
                lane_ref[1] = jnp.exp(lrow * lg_lane)
                lane_ref[3] = jnp.exp((C - lrow) * lg_lane)
            lg_row = jnp.where(srow_hi, log_decay((LANES, 2 * RET_DV), 1), log_decay((LANES, 2 * RET_DV), 0))
            sdec_ref[d] = jnp.where(same_head, jnp.exp(C * lg_row), 0.0)

    for b in range(bb):
        for d, s_ref in ((0, sf_ref), (1, sb_ref)):
            st_ref[b, d] = jnp.zeros((LANES, 2 * RET_DV), F32)
            for hh in range(2):
                st_ref[b, d, RET_DK * hh:RET_DK * (hh + 1), RET_DV * hh:RET_DV * (hh + 1)] = s_ref[b, hh]
    acc_ref[...] = jnp.zeros_like(acc_ref)
    dn = (((1,), (1,)), ((), ()))

    def step(n, _):
        for b in range(bb):
            for d in range(2):
                cidx = n if d == 0 else n_chunks - 1 - n
                r0 = pl.multiple_of(cidx * C, C)
                qc = q_ref[b, pl.ds(r0, C), :].astype(F32)
                kc = k_ref[b, pl.ds(r0, C), :]
                vc = v_ref[b, pl.ds(r0, C), :]
                q2 = jnp.concatenate([jnp.where(lane_hi, 0.0, qc), jnp.where(lane_hi, qc, 0.0)], axis=0)
                s2 = lax.dot_general(q2.astype(BF16), kc, dn, preferred_element_type=F32)
                intra = [jnp.dot((s2[C * hh:C * (hh + 1)] * dmask_ref[2 * d + hh]).astype(BF16),
                                 vc[:, RET_DV * hh:RET_DV * (hh + 1)], preferred_element_type=F32)
                         for hh in range(2)]
                state = st_ref[b, d]
                inter = jnp.dot((qc * lane_ref[2 + d]).astype(BF16), state.astype(BF16),
                                preferred_element_type=F32)
                u = jnp.dot((kc.astype(F32) * lane_ref[d]).T.astype(BF16), vc, preferred_element_type=F32)
                st_ref[b, d] = sdec_ref[d] * state + u * sdec_ref[2]
                acc_ref[b, pl.ds(r0, C), :] += jnp.concatenate(intra, axis=-1) + inter
        return 0

    lax.fori_loop(0, n_chunks, step, 0)

    for b in range(bb):
        for hh in range(2):
            rf_ref[b, hh] = st_ref[b, 0, RET_DK * hh:RET_DK * (hh + 1), RET_DV * hh:RET_DV * (hh + 1)]
            rb_ref[b, hh] = st_ref[b, 1, RET_DK * hh:RET_DK * (hh + 1), RET_DV * hh:RET_DV * (hh + 1)]

    ones = jnp.ones((RET_DV, RET_DV), BF16)

    def norm(n, _):
        r0 = pl.multiple_of(n * C, C)
        for b in range(bb):
            for hh in range(2):
                sl = slice(RET_DV * hh, RET_DV * (hh + 1))
                y = acc_ref[b, pl.ds(r0, C), sl]
                yc = y - jnp.mean(y, axis=-1, keepdims=True)
                var = jnp.dot((yc * yc).astype(BF16), ones, preferred_element_type=F32) * (1.0 / RET_DV)
                out = yc * lax.rsqrt(var + EPS) * gn_ref[:, sl]
                o_ref[b, pl.ds(r0, C), sl] = (out * gate_ref[b, pl.ds(r0, C), sl].astype(F32)).astype(BF16)
        return 0

    lax.fori_loop(0, n_chunks, norm, 0)


def _retention(rq, rk, rv, sf, sb, decay, ret_norm_g, gates, bb):
    B, L, _ = rq.shape
    C = RET_CHUNK
    hp = RET_HEADS // 2
    st_spec = pl.BlockSpec((bb, 2, RET_DK, RET_DV), lambda h, b: (b, h, 0, 0))
    st_shape = jax.ShapeDtypeStruct((B, RET_HEADS, RET_DK, RET_DV), F32)
    qk_spec = pl.BlockSpec((bb, L, LANES), lambda h, b: (b, 0, h))
    v_spec = pl.BlockSpec((bb, L, 2 * RET_DV), lambda h, b: (b, 0, h))
    return pl.pallas_call(
        functools.partial(_ret_kernel, L // C),
        grid=(hp, B // bb),
        in_specs=[pl.BlockSpec(memory_space=pltpu.SMEM), qk_spec, qk_spec, v_spec, st_spec, st_spec,
                  pl.BlockSpec((1, 2 * RET_DV), lambda h, b: (0, h)),
                  pl.BlockSpec((bb, L, 2 * RET_DV), lambda h, b: (b, 0, hp + h))],
        out_specs=[v_spec, st_spec, st_spec],
        out_shape=[jax.ShapeDtypeStruct((B, L, RET_WIDTH), BF16), st_shape, st_shape],
        scratch_shapes=[pltpu.VMEM((bb, L, 2 * RET_DV), F32),
                        pltpu.VMEM((bb, 2, LANES, 2 * RET_DV), F32),
                        pltpu.VMEM((4, C, C), F32),
                        pltpu.VMEM((4, C, LANES), F32),
                        pltpu.VMEM((3, LANES, 2 * RET_DV), F32)],
        compiler_params=_params(2),
        name="retention",
    )(decay, rq, rk, rv, sf, sb, ret_norm_g[None, :], gates)


def _prep_win_kernel(rope, *refs):
    if rope:
        (x_ref, mod_ref, ng_ref, win_ref, seg_ref, exp_ref, qg_ref, qgs_ref, kg_ref, kgs_ref, c_ref, s_ref,
         q_out, k_out, v_out, gate_out) = refs
    else:
        (x_ref, mod_ref, ng_ref, win_ref, seg_ref, exp_ref, qg_ref, kg_ref,
         q_out, k_out, v_out, gate_out, kst_out, vst_out) = refs
    o = WIN_OFF
    h = _modulated(x_ref, mod_ref, ng_ref)
    proj = jnp.dot(h, win_ref[...], preferred_element_type=F32)
    gate_out[...] = _silu(proj[:, o['g']:o['g'] + WIN_WIDTH]).astype(BF16)

    qg2 = jnp.concatenate([qg_ref[...]] * 2, axis=-1)
    kg2 = jnp.concatenate([kg_ref[...]] * 2, axis=-1)
    if rope:
        c2 = jnp.concatenate([c_ref[...]] * 2, axis=-1)
        s2 = jnp.concatenate([s_ref[...]] * 2, axis=-1)
        q_c, q_s = c2 * qg2, s2 * jnp.concatenate([qgs_ref[...]] * 2, axis=-1)
        k_c, k_s = c2 * kg2, s2 * jnp.concatenate([kgs_ref[...]] * 2, axis=-1)
    qscale = WIN_HEAD_DIM ** -0.5 * LOG2E
    for g in range(WIN_WIDTH // 256):
        sl = slice(256 * g, 256 * (g + 1))
        qg = proj[:, sl]
        rn = _seg_rsqrt(qg, seg_ref, WIN_HEAD_DIM) * qscale
        if rope:
            qv = qg * q_c + proj[:, 2560 + 256 * g:2560 + 256 * (g + 1)] * q_s
        else:
            qv = qg * qg2
        q_out[:, sl] = (qv * rn).astype(BF16)
    kraw = proj[:, o['k']:o['k'] + 256]
    rn = _seg_rsqrt(kraw, seg_ref, WIN_HEAD_DIM)
    if rope:
        kn = (kraw * k_c + proj[:, 3584:3584 + 256] * k_s) * rn
    else:
        kn = kraw * kg2 * rn
    v = proj[:, o['v']:o['v'] + 256]
    if not rope:
        kst_out[...] = kn
        vst_out[...] = v
    k_out[...] = jnp.dot(kn.astype(BF16), exp_ref[...], preferred_element_type=F32).astype(BF16)
    v_out[...] = _with_ones_lane(jnp.dot(v.astype(BF16), exp_ref[...], preferred_element_type=F32)).astype(BF16)


def _win_expand_matrix():
    e = np.zeros((256, 1024), np.float32)
    for j in range(WIN_KV_HEADS):
        for i in range(WIN_HEAD_DIM):
            e[WIN_HEAD_DIM * j + i, 256 * j + i] = 1.0
            e[WIN_HEAD_DIM * j + i, 256 * j + 192 + i] = 1.0
    return jnp.asarray(e, BF16)


def _prep_win(x, mod, norm_g, p, rope, tokens_per_batch, mod_row, tables, tm):
    T = x.shape[0]
    o = WIN_OFF
    sw64 = _swap_idx(WIN_HEAD_DIM)
    if rope:
        cols = list(range(2560))
        for hh in range(WIN_HEADS):
            cols += list(o['q'] + WIN_HEAD_DIM * hh + sw64)
        for hh in range(WIN_KV_HEADS):
            cols += list(o['k'] + WIN_HEAD_DIM * hh + sw64)
        win = _take_cols(p['w_in'], cols).astype(BF16)
    else:
        win = p['w_in'].astype(BF16)
    seg = _seg_matrix(WIN_HEAD_DIM)
    expand = _win_expand_matrix()
    rep = lambda g: jnp.concatenate([g, g])[None, :]
    blocks_per_batch = tokens_per_batch // tm
    tok = lambda w: pl.BlockSpec((tm, w), lambda i: (i, 0))
    if mod_row is None:
        mod_spec = pl.BlockSpec((1, 1, 3 * D_MODEL), lambda i: (i // blocks_per_batch, 0, 0))
    else:
        mod_spec = pl.BlockSpec((1, 1, 3 * D_MODEL), lambda i: (mod_row, 0, 0))
    vec = lambda a: (a, _const_spec(a.shape))
    ins = [(x, tok(D_MODEL)), (mod, mod_spec), vec(norm_g[None, :]), vec(win), vec(seg), vec(expand),
           vec(rep(p['q_head_g']))]
    if rope:
        ins.append(vec(rep(p['q_head_g'][sw64])))
    ins.append(vec(rep(p['k_head_g'])))
    if rope:
        ins.append(vec(rep(p['k_head_g'][sw64])))
        tab = lambda a: (a, pl.BlockSpec((tm, LANES), lambda i: (i % blocks_per_batch, 0)))
        ins += [tab(tables['c64']), tab(tables['s64'])]
    outs = [(1024, BF16), (1024, BF16), (1024, BF16), (1024, BF16)]
    if not rope:
        outs += [(256, F32), (256, F32)]
    return pl.pallas_call(
        functools.partial(_prep_win_kernel, rope),
        grid=(T // tm,),
        in_specs=[s for _, s in ins],
        out_specs=[tok(w) for w, _ in outs],
        out_shape=[jax.ShapeDtypeStruct((T, w), dt) for w, dt in outs],
        compiler_params=_params(1),
        name="prep_win_rope" if rope else "prep_win",
    )(*[a for a, _ in ins])


def _win_attn_kernel(local, *refs):
    if local:
        sink_ref, q_ref, kf_ref, vf_ref, k_ref, v_ref, g_ref, bias_ref, o_ref, s_ref = refs
    else:
        sink_ref, q_ref, kf_ref, vf_ref, g_ref, o_ref, s_ref = refs
    tq = q_ref.shape[1]
    nf = kf_ref.shape[1]
    kv_per_step = q_ref.shape[2] // (2 * LANES)
    upper = lax.broadcasted_iota(jnp.int32, (2 * tq, 1), 0) >= tq
    lane = lax.broadcasted_iota(jnp.int32, (2 * tq, LANES), 1)
    if local:
        span = bias_ref.shape[2]
        start = jnp.clip(pl.program_id(2) * tq - WINDOW, 0, k_ref.shape[1] - span)
        start = pl.multiple_of(start, WINDOW)
    dn = (((1,), (1,)), ((), ()))
    units = [(b, jj, half) for b in range(q_ref.shape[0]) for jj in range(kv_per_step) for half in range(2)]
    m_prev, sk_prev, out = None, None, None
    for idx in range(len(units) + 1):
        cur = units[idx] if idx < len(units) else None
        prev = units[idx - 1] if idx > 0 else None
        if cur is not None:
            cb, cjj, chalf = cur
            base = 2 * LANES * cjj
            csl = slice(base + LANES * chalf, base + LANES * (chalf + 1))
            q2 = jnp.concatenate([q_ref[cb, :, base:base + LANES],
                                  q_ref[cb, :, base + LANES:base + 2 * LANES]], axis=0)
            head = 4 * (pl.program_id(1) * kv_per_step + cjj) + chalf
            sk = jnp.where(upper, sink_ref[head + 2], sink_ref[head]) * LOG2E
            s_ctx = lax.dot_general(q2, kf_ref[cb, :, csl], dn, preferred_element_type=F32)
            s_ref[idx % 2, :, 0:nf] = s_ctx
            mt = _lane_groups(s_ctx, jnp.maximum)
        if prev is not None:
            pb, pjj, phalf = prev
            pbase = 2 * LANES * pjj
            psl = slice(pbase + LANES * phalf, pbase + LANES * (phalf + 1))
            e_ctx = jnp.exp2(s_ref[(idx - 1) % 2, :, 0:nf] - m_prev)
            acc = jnp.dot(e_ctx.astype(BF16), vf_ref[pb, :, psl], preferred_element_type=F32)
        if local and cur is not None:
            s_loc = lax.dot_general(q2, k_ref[cb, pl.ds(start, span), csl], dn,
                                    preferred_element_type=F32) + bias_ref[0]
            s_ref[idx % 2, :, nf:nf + span] = s_loc
            mt = jnp.maximum(mt, _lane_groups(s_loc, jnp.maximum))
        if local and prev is not None:
            e_loc = jnp.exp2(s_ref[(idx - 1) % 2, :, nf:nf + span] - m_prev)
            acc = acc + jnp.dot(e_loc.astype(BF16), v_ref[pb, pl.ds(start, span), psl],
                                preferred_element_type=F32)
        if prev is not None:
            den = jnp.sum(jnp.where(lane == _ones_lane(phalf), acc, 0.0), axis=-1, keepdims=True)
            den = den + jnp.exp2(sk_prev - m_prev)
            oh = jnp.where((lane >= WIN_HEAD_DIM) == (phalf == 1), acc, 0.0) / den
            out = oh if out is None else out + oh
            if phalf == 1:
                gate = g_ref[pb, :, pbase:pbase + 2 * LANES].astype(F32)
                o_ref[pb, :, pbase:pbase + LANES] = (out[0:tq] * gate[:, 0:LANES]).astype(BF16)
                o_ref[pb, :, pbase + LANES:pbase + 2 * LANES] = (
                    out[tq:2 * tq] * gate[:, LANES:2 * LANES]).astype(BF16)
                out = None
        if cur is not None:
            m_prev = jnp.maximum(jnp.max(mt, axis=-1, keepdims=True), sk)
            sk_prev = sk


def _window_bias(tq):
    span = tq + 2 * WINDOW
    qi = jnp.arange(2 * tq)[:, None] % tq
    kk = jnp.arange(span)[None, :]
    cases = [jnp.abs(kk - off - qi) <= WINDOW for off in (0, WINDOW, 2 * WINDOW)]
    return jnp.where(jnp.stack(cases), 0.0, NEG).astype(F32)


def _win_attn(q, kf, vf, own, gates, sink, bb, tq, kv_per_step):
    B, L, _ = q.shape
    nb = L // tq
    width = 2 * LANES * kv_per_step
    qspec = pl.BlockSpec((bb, tq, width), lambda b, h, i: (b, i, h))
    fspec = pl.BlockSpec((bb, kf.shape[1], width), lambda b, h, i: (b, 0, h))
    in_specs = [pl.BlockSpec(memory_space=pltpu.SMEM), qspec, fspec, fspec]
    args = [sink, q, kf, vf]
    n_keys = kf.shape[1]
    if own is not None:
        assert nb >= 3 and bb == 1
        kspec = pl.BlockSpec((bb, L, width), lambda b, h, i: (b, 0, h))
        in_specs += [kspec, kspec]
        args += list(own)
    in_specs.append(qspec)
    args.append(gates)
    if own is not None:
        bias = _window_bias(tq)
        n_keys += bias.shape[2]
        in_specs.append(pl.BlockSpec(
            (1,) + bias.shape[1:],
            lambda b, h, i: (jnp.where(i == 0, 0, jnp.where(i == nb - 1, 2, 1)), 0, 0)))
        args.append(bias)
    return pl.pallas_call(
        functools.partial(_win_attn_kernel, own is not None),
        grid=(B // bb, WIN_KV_HEADS // kv_per_step, nb),
        in_specs=in_specs,
        out_specs=qspec,
        out_shape=jax.ShapeDtypeStruct((B, L, WIN_WIDTH), BF16),
        scratch_shapes=[pltpu.VMEM((2, 2 * tq, n_keys), F32)],
        compiler_params=_params(3),
        name="win_attn_local" if own is not None else "win_attn",
    )(*args)


def _out_kernel(n_parts, *refs):
    x_ref, mod_ref = refs[0], refs[1]
    parts = refs[2:2 + 2 * n_parts]
    o_ref = refs[2 + 2 * n_parts]
    y = None
    for i in range(n_parts):
        t = jnp.dot(parts[2 * i][...], parts[2 * i + 1][...], preferred_element_type=F32)
        y = t if y is None else y + t
    gate = mod_ref[0, :, 2 * D_MODEL:3 * D_MODEL]
    o_ref[...] = x_ref[...] + gate * y


def _out_proj(x, mod, parts, tokens_per_batch, mod_row, tm):
    T = x.shape[0]
    blocks_per_batch = tokens_per_batch // tm
    tok = lambda w: pl.BlockSpec((tm, w), lambda i: (i, 0))
    if mod_row is None:
        mod_spec = pl.BlockSpec((1, 1, 3 * D_MODEL), lambda i: (i // blocks_per_batch, 0, 0))
    else:
        mod_spec = pl.BlockSpec((1, 1, 3 * D_MODEL), lambda i: (mod_row, 0, 0))
    in_specs = [tok(D_MODEL), mod_spec]
    args = [x, mod]
    for a, w in parts:
        in_specs += [tok(a.shape[1]), _const_spec(w.shape)]
        args += [a, w]
    return pl.pallas_call(
        functools.partial(_out_kernel, len(parts)),
        grid=(T // tm,),
        in_specs=in_specs,
        out_specs=tok(D_MODEL),
        out_shape=jax.ShapeDtypeStruct((T, D_MODEL), F32),
        compiler_params=_params(1),
        name="out_proj",
    )(*args)


def kernel(x_prompt, x_sample, cache_l0_mla_ckv, cache_l0_mla_krope, state_l0_ret_fwd, state_l0_ret_bwd, cache_l1_win_k, cache_l1_win_v, cache_l2_mla_ckv, cache_l2_mla_krope, state_l2_ret_fwd, state_l2_ret_bwd, cache_l3_win_k, cache_l3_win_v, c, c_ctx, ada_w, ada_b, norm_g, ab_w_in, mla_q_norm_g, mla_w_uq, mla_kv_norm_g, mla_w_ukv, mla_q_head_g, mla_k_head_g, ret_decay, ret_norm_g, ab_w_out, win_w_in, win_q_head_g, win_k_head_g, win_sink, win_w_out):
    BP, LP, D = x_prompt.shape
    BS, LS, _ = x_sample.shape
    P = cache_l0_mla_ckv.shape[1]
    ctx_caches = ((cache_l0_mla_ckv, cache_l0_mla_krope, state_l0_ret_fwd, state_l0_ret_bwd),
                  (cache_l1_win_k, cache_l1_win_v),
                  (cache_l2_mla_ckv, cache_l2_mla_krope, state_l2_ret_fwd, state_l2_ret_bwd),
                  (cache_l3_win_k, cache_l3_win_v))

    cond = jnp.concatenate([c, c_ctx[None, :], jnp.zeros((8 - BS - 1, D), F32)], axis=0)
    mod_all = _ada_mod(cond, ada_w, ada_b).reshape(DEPTH, 8, 1, 3 * D)
    ctx_row = BS

    c32, s32 = _rope_tables(LS, MLA_ROPE)
    c64, s64 = _rope_tables(LS, RET_DK)
    ones = jnp.ones((LS, MLA_NOPE), F32)
    zeros = jnp.zeros((LS, MLA_NOPE), F32)
    z32 = jnp.zeros((LS, LANES - MLA_QK), F32)
    tables = dict(cm=jnp.concatenate([ones, c32, z32], -1), sm=jnp.concatenate([zeros, s32, z32], -1),
                  c64=jnp.concatenate([c64, c64], -1), s64=jnp.concatenate([s64, s64], -1))

    y_p = x_prompt.reshape(BP * LP, D)
    y_s = x_sample.reshape(BS * LS, D)
    tm = 512
    new_state = []
    for l in range(DEPTH):
        i = l // 2
        mod = mod_all[l]
        if l % 2 == 0:
            p = {'w_in': ab_w_in[i], 'q_norm_g': mla_q_norm_g[i], 'w_uq': mla_w_uq[i],
                 'kv_norm_g': mla_kv_norm_g[i], 'w_ukv': mla_w_ukv[i], 'q_head_g': mla_q_head_g[i],
                 'k_head_g': mla_k_head_g[i]}
            w_out = ab_w_out[i].astype(BF16)
            parts_w = (w_out[:MLA_WIDTH], w_out[MLA_WIDTH:])
            ckv_c, krope_c, sf, sb = ctx_caches[l]

            q, k, v, gates, rq, rk, rv, ckv, kr = _prep_ab(y_p, mod, norm_g[l], p, False, LP, ctx_row, None, tm)
            r3 = lambda a: a.reshape(BP, LP, a.shape[-1])
            a_p = _mla_attn(r3(q), r3(k), r3(v), r3(gates), PROMPT_ROWS_PER_STEP, LP, 1)
            zst = jnp.zeros((BP, RET_HEADS, RET_DK, RET_DV), F32)
            r_p, rf, rb = _retention(r3(rq), r3(rk), r3(rv), zst, zst, ret_decay[i], ret_norm_g[i], r3(gates),
                                     PROMPT_ROWS_PER_STEP)
            new_state.append((ckv.reshape(BP, LP, KV_LORA),
                              kr[:, MLA_NOPE:MLA_QK].reshape(BP, LP, MLA_ROPE), rf, rb))
            y_p_next = _out_proj(y_p, mod, [(a_p.reshape(BP * LP, -1), parts_w[0]),
                                            (r_p.reshape(BP * LP, -1), parts_w[1])], LP, ctx_row, tm)

            kr_c = jnp.pad(krope_c.reshape(BS * P, MLA_ROPE), ((0, 0), (MLA_NOPE, LANES - MLA_QK)))
            q, k, v, gates, rq, rk, rv = _prep_ab(y_s, mod, norm_g[l], p, True, LS, None, tables, tm,
                                                  (ckv_c.reshape(BS * P, KV_LORA), kr_c))
            r3 = lambda a: a.reshape(BS, -1, a.shape[-1])
            a_s = _mla_attn(r3(q), r3(k), r3(v), r3(gates), 1, 256, 2)
            r_s, _, _ = _retention(r3(rq), r3(rk), r3(rv), sf, sb, ret_decay[i], ret_norm_g[i], r3(gates), 1)
            y_s_next = _out_proj(y_s, mod, [(a_s.reshape(BS * LS, -1), parts_w[0]),
                                            (r_s.reshape(BS * LS, -1), parts_w[1])], LS, None, tm)
        else:
            p = {'w_in': win_w_in[i], 'q_head_g': win_q_head_g[i], 'k_head_g': win_k_head_g[i]}
            w_out = win_w_out[i].astype(BF16)
            sink = win_sink[i]
            kc, vc = ctx_caches[l]

            q, k, v, gates, kst, vst = _prep_win(y_p, mod, norm_g[l], p, False, LP, ctx_row, None, tm)
            r3 = lambda a: a.reshape(BP, LP, a.shape[-1])
            o_p = _win_attn(r3(q), r3(k), r3(v), None, r3(gates), sink, PROMPT_ROWS_PER_STEP, LP, 1)
            new_state.append((kst.reshape(BP, LP, WIN_KV_HEADS, WIN_HEAD_DIM),
                              vst.reshape(BP, LP, WIN_KV_HEADS, WIN_HEAD_DIM)))
            y_p_next = _out_proj(y_p, mod, [(o_p.reshape(BP * LP, -1), w_out)], LP, ctx_row, tm)

            q, k, v, gates = _prep_win(y_s, mod, norm_g[l], p, True, LS, None, tables, tm)
            r3 = lambda a: a.reshape(BS, -1, a.shape[-1])
            def slots(a, fill):
                a = a.astype(BF16)
                f = jnp.broadcast_to(fill.astype(BF16), a.shape)
                return jnp.concatenate([a, f, f, a], axis=-1).reshape(BS, P, 4 * 2 * LANES)

            zero = jnp.zeros((WIN_HEAD_DIM,), F32)
            o_s = _win_attn(r3(q), slots(kc, zero), slots(vc, zero.at[0].set(1.0)), (r3(k), r3(v)), r3(gates),
                            sink, 1, 256, 2)
            y_s_next = _out_proj(y_s, mod, [(o_s.reshape(BS * LS, -1), w_out)], LS, None, tm)
        y_p, y_s = y_p_next, y_s_next

    (l0_ckv, l0_krope, l0_rf, l0_rb), (l1_k, l1_v), (l2_ckv, l2_krope, l2_rf, l2_rb), (l3_k, l3_v) = new_state
    return (y_p.reshape(BP, LP, D), y_s.reshape(BS, LS, D), l0_ckv, l0_krope, l0_rf, l0_rb, l1_k, l1_v,
            l2_ckv, l2_krope, l2_rf, l2_rb, l3_k, l3_v)
```

```python
import functools

import numpy as np
import jax
import jax.numpy as jnp
from jax import lax
from jax.experimental import pallas as pl
from jax.experimental.pallas import tpu as pltpu

D_MODEL = 1024
DEPTH = 4
GRID_W = 64
ROPE_BASE = 10000.0
EPS = 1e-6

MLA_HEADS = 8
MLA_NOPE = 64
MLA_ROPE = 32
MLA_QK = MLA_NOPE + MLA_ROPE
MLA_V = 64
Q_LORA = 384
KV_LORA = 256
MLA_WIDTH = MLA_HEADS * MLA_V

RET_HEADS = 4
RET_DK = 64
RET_DV = 128
RET_CHUNK = 256
RET_WIDTH = RET_HEADS * RET_DV

WIN_HEADS = 16
WIN_KV_HEADS = 4
WIN_HEAD_DIM = 64
WINDOW = 128
WIN_WIDTH = WIN_HEADS * WIN_HEAD_DIM

LANES = 128
F32 = jnp.float32
BF16 = jnp.bfloat16
NEG = -1e30
LOG2E = 1.4426950408889634
VMEM_LIMIT = 52 * 1024 * 1024
PROMPT_ROWS_PER_STEP = 4
MLA_KEY_CHUNK = 512

AB_OFF = dict(cq=0, ckv=384, krope=640, ga=672, rq=1184, rk=1440, rv=1696, gb=2208)
AB_LAYOUT = dict(kr=640, g=768, rq=1792, rk=2048, rv=2304)
WIN_OFF = dict(q=0, k=1024, v=1280, g=1536)


def _params(n_axes):
    return pltpu.CompilerParams(dimension_semantics=("arbitrary",) * n_axes, vmem_limit_bytes=VMEM_LIMIT)


def _const_spec(shape):
    nd = len(shape)
    return pl.BlockSpec(shape, lambda *_: (0,) * nd)


def _swap_idx(dim):
    d2 = dim // 2
    half = d2 // 2
    one = np.concatenate([np.arange(half, d2), np.arange(0, half)])
    return np.concatenate([one, d2 + one])


def _take_cols(w, cols):
    cols = np.asarray(cols, np.int32)
    picked = jnp.take(w.astype(BF16), jnp.asarray(np.maximum(cols, 0)), axis=1)
    return jnp.where(jnp.asarray(cols >= 0)[None, :], picked, jnp.zeros((), BF16))


def _rope_tables(n_tokens, dim):
    t = jnp.arange(n_tokens)
    rows = (t // GRID_W).astype(F32)
    cols = (t % GRID_W).astype(F32)
    half = dim // 4
    inv = jnp.power(jnp.float32(ROPE_BASE), -jnp.arange(half, dtype=F32) / half)

    def one(pos):
        ang = pos[:, None] * inv[None, :]
        c, s = jnp.cos(ang), jnp.sin(ang)
        return jnp.concatenate([c, c], -1), jnp.concatenate([-s, s], -1)

    cr, sr = one(rows)
    cc, sc = one(cols)
    return jnp.concatenate([cr, cc], -1), jnp.concatenate([sr, sc], -1)


def _ada_kernel(cond_ref, w_ref, b_ref, o_ref):
    c = cond_ref[...]
    sc = (c * jax.nn.sigmoid(c)).astype(BF16)
    o_ref[0] = jnp.dot(sc, w_ref[0].astype(BF16), preferred_element_type=F32) + b_ref[0]


def _ada_mod(cond, ada_w, ada_b):
    tn = 768
    n3 = 3 * D_MODEL
    return pl.pallas_call(
        _ada_kernel,
        grid=(DEPTH, n3 // tn),
        in_specs=[pl.BlockSpec((8, D_MODEL), lambda l, j: (0, 0)),
                  pl.BlockSpec((1, D_MODEL, tn), lambda l, j: (l, 0, j)),
                  pl.BlockSpec((1, 1, tn), lambda l, j: (l, 0, j))],
        out_specs=pl.BlockSpec((1, 8, tn), lambda l, j: (l, 0, j)),
        out_shape=jax.ShapeDtypeStruct((DEPTH, 8, n3), F32),
        compiler_params=_params(2),
        name="ada_mod",
    )(cond, ada_w, ada_b.reshape(DEPTH, 1, n3))


def _rms(x, g):
    return x * lax.rsqrt(jnp.mean(x * x, axis=-1, keepdims=True) + EPS) * g


def _modulated(x_ref, mod_ref, ng_ref):
    x = x_ref[...]
    shift = mod_ref[0, :, 0:D_MODEL]
    scale = mod_ref[0, :, D_MODEL:2 * D_MODEL]
    return (_rms(x, ng_ref[...]) * (1.0 + scale) + shift).astype(BF16)


def _seg_rsqrt(x, seg_ref, n_real):
    ssq = jnp.dot((x * x).astype(BF16), seg_ref[...], preferred_element_type=F32)
    return lax.rsqrt(ssq * (1.0 / n_real) + EPS)


def _silu(g):
    return g * jax.nn.sigmoid(g)


def _rope_partner(x, half):
    lane = lax.broadcasted_iota(jnp.int32, (x.shape[0], LANES), 1)
    first = lane % (2 * half) < half
    cols = []
    for c in range(x.shape[1] // LANES):
        xc = x[:, LANES * c:LANES * (c + 1)]
        cols.append(jnp.where(first, pltpu.roll(xc, LANES - half, 1), pltpu.roll(xc, half, 1)))
    return cols[0] if len(cols) == 1 else jnp.concatenate(cols, axis=-1)


def _ones_lane(head):
    return MLA_V if head % 2 == 0 else 0


def _with_ones_lane(v):
    lane = lax.broadcasted_iota(jnp.int32, v.shape, 1)
    odd = (lane // LANES) % 2 == 1
    return jnp.where(lane % LANES == jnp.where(odd, _ones_lane(1), _ones_lane(0)), 1.0, v)


def _ctx_kv(ckv_ref, kr_ref, wk_ref, wv_ref, seg_ref, kg_ref, k_out, v_out):
    ckv = ckv_ref[...].astype(BF16)
    kn = jnp.dot(ckv, wk_ref[...], preferred_element_type=F32)
    v_out[...] = _with_ones_lane(jnp.dot(ckv, wv_ref[...], preferred_element_type=F32)).astype(BF16)
    kr = kr_ref[...]
    kg = kg_ref[...]
    kr2 = jnp.concatenate([kr, kr], axis=-1)
    kg2 = jnp.concatenate([kg, kg], axis=-1)
    for g in range(MLA_HEADS // 2):
        sl = slice(2 * LANES * g, 2 * LANES * (g + 1))
        kraw = kn[:, sl] + kr2
        k_out[:, sl] = (kraw * kg2 * _seg_rsqrt(kraw, seg_ref, MLA_QK)).astype(BF16)


def _prep_ab_kernel(rope, ctx_blocks, steps_per_batch, *refs):
    if not ctx_blocks:
        _prep_ab_body(rope, *refs)
        return
    n_in = 18
    ckvc_ref, krc_ref = refs[n_in:n_in + 2]
    body_refs = refs[:n_in] + refs[n_in + 2:]
    wk_ref, wv_ref, seg_ref, kg_ref = refs[7], refs[8], refs[9], refs[12]
    k_out, v_out = body_refs[n_in + 1], body_refs[n_in + 2]
    is_ctx = pl.program_id(0) % steps_per_batch < ctx_blocks

    @pl.when(is_ctx)
    def _():
        _ctx_kv(ckvc_ref, krc_ref, wk_ref, wv_ref, seg_ref, kg_ref, k_out, v_out)

    @pl.when(jnp.logical_not(is_ctx))
    def _():
        _prep_ab_body(rope, *body_refs)


def _prep_ab_body(rope, *refs):
    if rope:
        (x_ref, mod_ref, ng_ref, win_ref, qng_ref, kvng_ref, wq_ref, wk_ref, wv_ref, seg_ref,
         qg_ref, qgs_ref, kg_ref, kgs_ref, cm_ref, sm_ref, cr_ref, sr_ref,
         q_out, k_out, v_out, gate_out, rq_out, rk_out, rv_out) = refs
    else:
        (x_ref, mod_ref, ng_ref, win_ref, qng_ref, kvng_ref, wq_ref, wk_ref, wv_ref, seg_ref,
         qg_ref, kg_ref,
         q_out, k_out, v_out, gate_out, rq_out, rk_out, rv_out, ckv_out, kr_out) = refs
    o_kr, o_g, o_rq, o_rk, o_rv = AB_LAYOUT['kr'], AB_LAYOUT['g'], AB_LAYOUT['rq'], AB_LAYOUT['rk'], AB_LAYOUT['rv']

    h = _modulated(x_ref, mod_ref, ng_ref)
    proj = jnp.dot(h, win_ref[...], preferred_element_type=F32)

    cqn = _rms(proj[:, 0:Q_LORA], qng_ref[...]).astype(BF16)
    ckvn_f = _rms(proj[:, Q_LORA:Q_LORA + KV_LORA], kvng_ref[...])
    ckvn = ckvn_f.astype(BF16)
    kr = proj[:, o_kr:o_kr + LANES]

    gate_out[...] = _silu(proj[:, o_g:o_g + 2 * MLA_WIDTH]).astype(BF16)
    rv_out[...] = proj[:, o_rv:o_rv + RET_WIDTH].astype(BF16)

    qm = jnp.dot(cqn, wq_ref[...], preferred_element_type=F32)
    kn = jnp.dot(ckvn, wk_ref[...], preferred_element_type=F32)
    v_out[...] = _with_ones_lane(jnp.dot(ckvn, wv_ref[...], preferred_element_type=F32)).astype(BF16)

    qscale = MLA_QK ** -0.5 * LOG2E
    if rope:
        krs = _rope_partner(kr, MLA_ROPE // 4)
        cm, sm = cm_ref[...], sm_ref[...]
        q_c = cm * qg_ref[...]
        q_s = sm * qgs_ref[...]
        kr_rot = kr * (cm * kg_ref[...]) + krs * (sm * kgs_ref[...])
        cr, sr = cr_ref[...], sr_ref[...]
        rq = proj[:, o_rq:o_rq + 256]
        rk = proj[:, o_rk:o_rk + 256]
        rqs = _rope_partner(rq, RET_DK // 4)
        rks = _rope_partner(rk, RET_DK // 4)
        for c in range(2):
            sl = slice(LANES * c, LANES * (c + 1))
            rq_out[:, sl] = ((rq[:, sl] * cr + rqs[:, sl] * sr) * (RET_DK ** -0.5)).astype(BF16)
            rk_out[:, sl] = (rk[:, sl] * cr + rks[:, sl] * sr).astype(BF16)
    else:
        q_c = jnp.broadcast_to(qg_ref[...], (qm.shape[0], LANES))
        kr_rot = kr * kg_ref[...]
        rq_out[...] = (proj[:, o_rq:o_rq + 256] * (RET_DK ** -0.5)).astype(BF16)
        rk_out[...] = proj[:, o_rk:o_rk + 256].astype(BF16)
        ckv_out[...] = ckvn_f
        kr_out[...] = kr

    kg = kg_ref[...]
    q_c2 = jnp.concatenate([q_c, q_c], axis=-1)
    kr2 = jnp.concatenate([kr, kr], axis=-1)
    kr_rot2 = jnp.concatenate([kr_rot, kr_rot], axis=-1)
    kg2 = jnp.concatenate([kg, kg], axis=-1)
    if rope:
        q_s2 = jnp.concatenate([q_s, q_s], axis=-1)
    for g in range(MLA_HEADS // 2):
        sl = slice(2 * LANES * g, 2 * LANES * (g + 1))
        qg = qm[:, sl]
        rq_n = _seg_rsqrt(qg, seg_ref, MLA_QK)
        qv = qg * q_c2
        if rope:
            qv = qv + _rope_partner(qg, MLA_ROPE // 4) * q_s2
        q_out[:, sl] = (qv * (rq_n * qscale)).astype(BF16)
        kgp = kn[:, sl]
        rk_n = _seg_rsqrt(kgp + kr2, seg_ref, MLA_QK)
        k_out[:, sl] = ((kgp * kg2 + kr_rot2) * rk_n).astype(BF16)


def _ab_weights(w_in, w_uq, w_ukv, q_head_g, k_head_g, rope):
    sw32 = _swap_idx(MLA_ROPE)
    z = lambda n: [-1] * n
    o = AB_OFF
    cols = list(range(o['cq'], o['cq'] + Q_LORA)) + list(range(o['ckv'], o['ckv'] + KV_LORA))
    cols += z(64) + list(range(o['krope'], o['krope'] + MLA_ROPE)) + z(32)
    cols += list(range(o['ga'], o['ga'] + MLA_WIDTH)) + list(range(o['gb'], o['gb'] + RET_WIDTH))
    cols += list(range(o['rq'], o['rq'] + 256)) + list(range(o['rk'], o['rk'] + 256))
    cols += list(range(o['rv'], o['rv'] + RET_WIDTH))
    assert len(cols) == AB_LAYOUT['rv'] + RET_WIDTH
    win = _take_cols(w_in, cols).astype(BF16)

    qc, kc, vc = [], [], []
    for hh in range(MLA_HEADS):
        qb = MLA_QK * hh
        qc += list(range(qb, qb + MLA_QK)) + z(32)
        kb = (MLA_NOPE + MLA_V) * hh
        kc += list(range(kb, kb + MLA_NOPE)) + z(64)
        vcols = list(range(kb + MLA_NOPE, kb + MLA_NOPE + MLA_V))
        vc += (vcols + z(64)) if hh % 2 == 0 else (z(64) + vcols)
    wq = _take_cols(w_uq, qc).astype(BF16)
    wk = _take_cols(w_ukv, kc).astype(BF16)
    wv = _take_cols(w_ukv, vc).astype(BF16)

    pad = lambda g: jnp.concatenate([g, jnp.zeros((LANES - MLA_QK,), F32)])[None, :]
    gsw = lambda g: jnp.concatenate([jnp.zeros((MLA_NOPE,), F32), g[MLA_NOPE + sw32],
                                     jnp.zeros((LANES - MLA_QK,), F32)])[None, :]
    qg, kg = pad(q_head_g), pad(k_head_g)
    qgs, kgs = (gsw(q_head_g), gsw(k_head_g)) if rope else (None, None)
    return win, wq, wk, wv, qg, qgs, kg, kgs


def _seg_matrix(width):
    idx = np.arange(2 * LANES) // width
    return jnp.asarray((idx[:, None] == idx[None, :]).astype(np.float32), BF16)


def _prep_ab(x, mod, norm_g, p, rope, tokens_per_batch, mod_row, tables, tm, ctx=None):
    T = x.shape[0]
    win, wq, wk, wv, qg, qgs, kg, kgs = _ab_weights(
        p['w_in'], p['w_uq'], p['w_ukv'], p['q_head_g'], p['k_head_g'], rope)
    seg = _seg_matrix(LANES)
    blocks_per_batch = max(tokens_per_batch // tm, 1)
    ctx_blocks = 0
    if ctx is not None:
        n_batch = T // tokens_per_batch
        ctx_rows = ctx[0].shape[0] // n_batch
        assert rope and ctx_rows % tm == 0 and tokens_per_batch % tm == 0
        ctx_blocks = ctx_rows // tm
    steps_per_batch = blocks_per_batch + ctx_blocks
    batch = lambda i: i // steps_per_batch
    in_batch = lambda i: jnp.maximum(i % steps_per_batch - ctx_blocks, 0)
    own = lambda i: batch(i) * blocks_per_batch + in_batch(i)
    tok = lambda w: pl.BlockSpec((tm, w), lambda i: (own(i), 0))
    if mod_row is None:
        mod_spec = pl.BlockSpec((1, 1, 3 * D_MODEL), lambda i: (batch(i), 0, 0))
    else:
        mod_spec = pl.BlockSpec((1, 1, 3 * D_MODEL), lambda i: (mod_row, 0, 0))
    vec = lambda a: (a, _const_spec(a.shape))
    ins = [(x, tok(D_MODEL)), (mod, mod_spec), vec(norm_g[None, :]), vec(win),
           vec(p['q_norm_g'][None, :]), vec(p['kv_norm_g'][None, :]), vec(wq),
           vec(wk), vec(wv), vec(seg), vec(qg)]
    if rope:
        ins.append(vec(qgs))
    ins.append(vec(kg))
    if rope:
        ins.append(vec(kgs))
        tab = lambda a: (a, pl.BlockSpec((tm, LANES), lambda i: (in_batch(i), 0)))
        ins += [tab(tables['cm']), tab(tables['sm']), tab(tables['c64']), tab(tables['s64'])]
    kv_rows, kv_spec = T, tok(1024)
    if ctx_blocks:
        cspec = lambda w: pl.BlockSpec((tm, w), lambda i: (
            batch(i) * ctx_blocks + jnp.minimum(i % steps_per_batch, ctx_blocks - 1), 0))
        ins += [(ctx[0], cspec(KV_LORA)), (ctx[1], cspec(LANES))]
        kv_rows = T + ctx[0].shape[0]
        kv_spec = pl.BlockSpec((tm, 1024), lambda i: (i, 0))
    outs = [(T, 1024, BF16, tok(1024)), (kv_rows, 1024, BF16, kv_spec), (kv_rows, 1024, BF16, kv_spec),
            (T, 1024, BF16, tok(1024)), (T, 256, BF16, tok(256)), (T, 256, BF16, tok(256)),
            (T, 512, BF16, tok(512))]
    if not rope:
        outs += [(T, KV_LORA, F32, tok(KV_LORA)), (T, LANES, F32, tok(LANES))]
    return pl.pallas_call(
        functools.partial(_prep_ab_kernel, rope, ctx_blocks, steps_per_batch),
        grid=(kv_rows // tm,),
        in_specs=[s for _, s in ins],
        out_specs=[o[3] for o in outs],
        out_shape=[jax.ShapeDtypeStruct(o[:2], o[2]) for o in outs],
        compiler_params=_params(1),
        name="prep_ab_rope" if rope else "prep_ab",
    )(*[a for a, _ in ins])


def _lane_groups(x, op):
    out = x[:, 0:LANES]
    for t in range(1, x.shape[1] // LANES):
        out = op(out, x[:, LANES * t:LANES * (t + 1)])
    return out


def _mla_attn_kernel(tk, q_ref, k_ref, v_ref, g_ref, o_ref, s_ref):
    dn = (((1,), (1,)), ((), ()))
    tq = q_ref.shape[1]
    nk = k_ref.shape[1] // tk
    heads = [(b, j) for b in range(q_ref.shape[0]) for j in range(q_ref.shape[2] // LANES)]
    m_prev, out = None, None
    for idx in range(len(heads) + 1):
        cur = heads[idx] if idx < len(heads) else None
        prev = heads[idx - 1] if idx > 0 else None
        if cur is not None:
            cb, cj = cur
            csl = slice(LANES * cj, LANES * (cj + 1))
            q = q_ref[cb, :, csl]
            mt = None
        if prev is not None:
            pb, pj = prev
            psl = slice(LANES * pj, LANES * (pj + 1))
            acc = jnp.zeros((tq, LANES), F32)
        for c in range(nk):
            rows = slice(c * tk, (c + 1) * tk)
            if cur is not None:
                s = lax.dot_general(q, k_ref[cb, rows, csl], dn, preferred_element_type=F32)
                s_ref[idx % 2, :, rows] = s
                smax = _lane_groups(s, jnp.maximum)
                mt = smax if mt is None else jnp.maximum(mt, smax)
            if prev is not None:
                p = jnp.exp2(s_ref[(idx - 1) % 2, :, rows] - m_prev)
                acc = acc + jnp.dot(p.astype(BF16), v_ref[pb, rows, psl], preferred_element_type=F32)
        if prev is not None:
            lane = lax.broadcasted_iota(jnp.int32, (tq, LANES), 1)
            l = jnp.sum(jnp.where(lane == _ones_lane(pj), acc, 0.0), axis=-1, keepdims=True)
            oj = jnp.where((lane >= MLA_V) == (pj % 2 == 1), acc, 0.0) / l
            out = oj if out is None else out + oj
            if pj % 2 == 1:
                osl = slice(LANES * (pj // 2), LANES * (pj // 2 + 1))
                o_ref[pb, :, osl] = (out * g_ref[pb, :, osl].astype(F32)).astype(BF16)
                out = None
        if cur is not None:
            m_prev = jnp.max(mt, axis=-1, keepdims=True)


def _mla_attn(q, k, v, gates, bb, tq, pairs):
    B, L, _ = q.shape
    Lk = k.shape[1]
    hp = MLA_HEADS // 2 // pairs
    qspec = pl.BlockSpec((bb, tq, 2 * LANES * pairs), lambda b, h, i: (b, i, h))
    kspec = pl.BlockSpec((bb, Lk, 2 * LANES * pairs), lambda b, h, i: (b, 0, h))
    ospec = pl.BlockSpec((bb, tq, LANES * pairs), lambda b, h, i: (b, i, h))
    tk = min(Lk, MLA_KEY_CHUNK)
    assert Lk % tk == 0
    return pl.pallas_call(
        functools.partial(_mla_attn_kernel, tk),
        grid=(B // bb, hp, L // tq),
        in_specs=[qspec, kspec, kspec, ospec],
        out_specs=ospec,
        out_shape=jax.ShapeDtypeStruct((B, L, MLA_WIDTH), BF16),
        scratch_shapes=[pltpu.VMEM((2, tq, Lk), F32)],
        compiler_params=_params(3),
        name="mla_attn",
    )(q, k, v, gates)


def _ret_kernel(n_chunks, decay_ref, q_ref, k_ref, v_ref, sf_ref, sb_ref, gn_ref, gate_ref,
                o_ref, rf_ref, rb_ref, acc_ref, st_ref, dmask_ref, lane_ref, sdec_ref):
    C = RET_CHUNK
    pair = pl.program_id(0)
    bb = q_ref.shape[0]
    lane_hi = lax.broadcasted_iota(jnp.int32, (C, LANES), 1) >= RET_DK

    @pl.when(pl.program_id(1) == 0)
    def _():
        row = lax.broadcasted_iota(jnp.int32, (C, C), 0).astype(F32)
        rel = row - lax.broadcasted_iota(jnp.int32, (C, C), 1).astype(F32)
        lrow = lax.broadcasted_iota(jnp.int32, (C, LANES), 0).astype(F32)
        srow_hi = lax.broadcasted_iota(jnp.int32, (LANES, 2 * RET_DV), 0) >= RET_DK
        scol_hi = lax.broadcasted_iota(jnp.int32, (LANES, 2 * RET_DV), 1) >= RET_DV
        same_head = srow_hi == scol_hi
        sdec_ref[2] = jnp.where(same_head, 1.0, 0.0)
        for d in range(2):
            log_decay = lambda shape, hh: -jnp.exp(jnp.full(shape, decay_ref[d, 2 * pair + hh], F32))
            for hh in range(2):
                lg = log_decay((C, C), hh)
                if d == 0:
                    dmask_ref[2 * d + hh] = jnp.where(rel >= 0, jnp.exp(jnp.maximum(rel, 0.0) * lg), 0.0)
                else:
                    dmask_ref[2 * d + hh] = jnp.where(rel <= 0, jnp.exp(jnp.maximum(-rel, 0.0) * lg), 0.0)
            lg_lane = jnp.where(lane_hi, log_decay((C, LANES), 1), log_decay((C, LANES), 0))
            if d == 0:
                lane_ref[0] = jnp.exp((C - 1.0 - lrow) * lg_lane)
                lane_ref[2] = jnp.exp((lrow + 1.0) * lg_lane)
            else:
                lane_ref[1] = jnp.exp(lrow * lg_lane)
                lane_ref[3] = jnp.exp((C - lrow) * lg_lane)
            lg_row = jnp.where(srow_hi, log_decay((LANES, 2 * RET_DV), 1), log_decay((LANES, 2 * RET_DV), 0))
            sdec_ref[d] = jnp.where(same_head, jnp.exp(C * lg_row), 0.0)

    for b in range(bb):
        for d, s_ref in ((0, sf_ref), (1, sb_ref)):
            st_ref[b, d] = jnp.zeros((LANES, 2 * RET_DV), F32)
            for hh in range(2):
                st_ref[b, d, RET_DK * hh:RET_DK * (hh + 1), RET_DV * hh:RET_DV * (hh + 1)] = s_ref[b, hh]
    acc_ref[...] = jnp.zeros_like(acc_ref)
    dn = (((1,), (1,)), ((), ()))

    def step(n, _):
        for b in range(bb):
            for d in range(2):
                cidx = n if d == 0 else n_chunks - 1 - n
                r0 = pl.multiple_of(cidx * C, C)
                qc = q_ref[b, pl.ds(r0, C), :].astype(F32)
                kc = k_ref[b, pl.ds(r0, C), :]
                vc = v_ref[b, pl.ds(r0, C), :]
                q2 = jnp.concatenate([jnp.where(lane_hi, 0.0, qc), jnp.where(lane_hi, qc, 0.0)], axis=0)
                s2 = lax.dot_general(q2.astype(BF16), kc, dn, preferred_element_type=F32)
                intra = [jnp.dot((s2[C * hh:C * (hh + 1)] * dmask_ref[2 * d + hh]).astype(BF16),
                                 vc[:, RET_DV * hh:RET_DV * (hh + 1)], preferred_element_type=F32)
                         for hh in range(2)]
                state = st_ref[b, d]
                inter = jnp.dot((qc * lane_ref[2 + d]).astype(BF16), state.astype(BF16),
                                preferred_element_type=F32)
                u = jnp.dot((kc.astype(F32) * lane_ref[d]).T.astype(BF16), vc, preferred_element_type=F32)
                st_ref[b, d] = sdec_ref[d] * state + u * sdec_ref[2]
                acc_ref[b, pl.ds(r0, C), :] += jnp.concatenate(intra, axis=-1) + inter
        return 0

    lax.fori_loop(0, n_chunks, step, 0)

    for b in range(bb):
        for hh in range(2):
            rf_ref[b, hh] = st_ref[b, 0, RET_DK * hh:RET_DK * (hh + 1), RET_DV * hh:RET_DV * (hh + 1)]
            rb_ref[b, hh] = st_ref[b, 1, RET_DK * hh:RET_DK * (hh + 1), RET_DV * hh:RET_DV * (hh + 1)]

    ones = jnp.ones((RET_DV, RET_DV), BF16)

    def norm(n, _):
        r0 = pl.multiple_of(n * C, C)
        for b in range(bb):
            for hh in range(2):
                sl = slice(RET_DV * hh, RET_DV * (hh + 1))
                y = acc_ref[b, pl.ds(r0, C), sl]
                yc = y - jnp.mean(y, axis=-1, keepdims=True)
                var = jnp.dot((yc * yc).astype(BF16), ones, preferred_element_type=F32) * (1.0 / RET_DV)
                out = yc * lax.rsqrt(var + EPS) * gn_ref[:, sl]
                o_ref[b, pl.ds(r0, C), sl] = (out * gate_ref[b, pl.ds(r0, C), sl].astype(F32)).astype(BF16)
        return 0

    lax.fori_loop(0, n_chunks, norm, 0)


def _retention(rq, rk, rv, sf, sb, decay, ret_norm_g, gates, bb):
    B, L, _ = rq.shape
    C = RET_CHUNK
    hp = RET_HEADS // 2
    st_spec = pl.BlockSpec((bb, 2, RET_DK, RET_DV), lambda h, b: (b, h, 0, 0))
    st_shape = jax.ShapeDtypeStruct((B, RET_HEADS, RET_DK, RET_DV), F32)
    qk_spec = pl.BlockSpec((bb, L, LANES), lambda h, b: (b, 0, h))
    v_spec = pl.BlockSpec((bb, L, 2 * RET_DV), lambda h, b: (b, 0, h))
    return pl.pallas_call(
        functools.partial(_ret_kernel, L // C),
        grid=(hp, B // bb),
        in_specs=[pl.BlockSpec(memory_space=pltpu.SMEM), qk_spec, qk_spec, v_spec, st_spec, st_spec,
                  pl.BlockSpec((1, 2 * RET_DV), lambda h, b: (0, h)),
                  pl.BlockSpec((bb, L, 2 * RET_DV), lambda h, b: (b, 0, hp + h))],
        out_specs=[v_spec, st_spec, st_spec],
        out_shape=[jax.ShapeDtypeStruct((B, L, RET_WIDTH), BF16), st_shape, st_shape],
        scratch_shapes=[pltpu.VMEM((bb, L, 2 * RET_DV), F32),
                        pltpu.VMEM((bb, 2, LANES, 2 * RET_DV), F32),
                        pltpu.VMEM((4, C, C), F32),
                        pltpu.VMEM((4, C, LANES), F32),
                        pltpu.VMEM((3, LANES, 2 * RET_DV), F32)],
        compiler_params=_params(2),
        name="retention",
    )(decay, rq, rk, rv, sf, sb, ret_norm_g[None, :], gates)


def _prep_win_kernel(rope, *refs):
    if rope:
        (x_ref, mod_ref, ng_ref, win_ref, seg_ref, exp_ref, qg_ref, qgs_ref, kg_ref, kgs_ref, c_ref, s_ref,
         q_out, k_out, v_out, gate_out) = refs
    else:
        (x_ref, mod_ref, ng_ref, win_ref, seg_ref, exp_ref, qg_ref, kg_ref,
         q_out, k_out, v_out, gate_out, kst_out, vst_out) = refs
    o = WIN_OFF
    h = _modulated(x_ref, mod_ref, ng_ref)
    proj = jnp.dot(h, win_ref[...], preferred_element_type=F32)
    gate_out[...] = _silu(proj[:, o['g']:o['g'] + WIN_WIDTH]).astype(BF16)

    qg2 = jnp.concatenate([qg_ref[...]] * 2, axis=-1)
    kg2 = jnp.concatenate([kg_ref[...]] * 2, axis=-1)
    if rope:
        c2 = jnp.concatenate([c_ref[...]] * 2, axis=-1)
        s2 = jnp.concatenate([s_ref[...]] * 2, axis=-1)
        q_c, q_s = c2 * qg2, s2 * jnp.concatenate([qgs_ref[...]] * 2, axis=-1)
        k_c, k_s = c2 * kg2, s2 * jnp.concatenate([kgs_ref[...]] * 2, axis=-1)
    qscale = WIN_HEAD_DIM ** -0.5 * LOG2E
    for g in range(WIN_WIDTH // 256):
        sl = slice(256 * g, 256 * (g + 1))
        qg = proj[:, sl]
        rn = _seg_rsqrt(qg, seg_ref, WIN_HEAD_DIM) * qscale
        if rope:
            qv = qg * q_c + _rope_partner(qg, WIN_HEAD_DIM // 4) * q_s
        else:
            qv = qg * qg2
        q_out[:, sl] = (qv * rn).astype(BF16)
    kraw = proj[:, o['k']:o['k'] + 256]
    rn = _seg_rsqrt(kraw, seg_ref, WIN_HEAD_DIM)
    if rope:
        kn = (kraw * k_c + _rope_partner(kraw, WIN_HEAD_DIM // 4) * k_s) * rn
    else:
        kn = kraw * kg2 * rn
    v = proj[:, o['v']:o['v'] + 256]
    if not rope:
        kst_out[...] = kn
        vst_out[...] = v
    k_out[...] = jnp.dot(kn.astype(BF16), exp_ref[...], preferred_element_type=F32).astype(BF16)
    v_out[...] = _with_ones_lane(jnp.dot(v.astype(BF16), exp_ref[...], preferred_element_type=F32)).astype(BF16)


def _win_expand_matrix():
    e = np.zeros((256, 1024), np.float32)
    for j in range(WIN_KV_HEADS):
        for i in range(WIN_HEAD_DIM):
            e[WIN_HEAD_DIM * j + i, 256 * j + i] = 1.0
            e[WIN_HEAD_DIM * j + i, 256 * j + 192 + i] = 1.0
    return jnp.asarray(e, BF16)


def _prep_win(x, mod, norm_g, p, rope, tokens_per_batch, mod_row, tables, tm):
    T = x.shape[0]
    sw64 = _swap_idx(WIN_HEAD_DIM)
    win = p['w_in'].astype(BF16)
    seg = _seg_matrix(WIN_HEAD_DIM)
    expand = _win_expand_matrix()
    rep = lambda g: jnp.concatenate([g, g])[None, :]
    blocks_per_batch = tokens_per_batch // tm
    tok = lambda w: pl.BlockSpec((tm, w), lambda i: (i, 0))
    if mod_row is None:
        mod_spec = pl.BlockSpec((1, 1, 3 * D_MODEL), lambda i: (i // blocks_per_batch, 0, 0))
    else:
        mod_spec = pl.BlockSpec((1, 1, 3 * D_MODEL), lambda i: (mod_row, 0, 0))
    vec = lambda a: (a, _const_spec(a.shape))
    ins = [(x, tok(D_MODEL)), (mod, mod_spec), vec(norm_g[None, :]), vec(win), vec(seg), vec(expand),
           vec(rep(p['q_head_g']))]
    if rope:
        ins.append(vec(rep(p['q_head_g'][sw64])))
    ins.append(vec(rep(p['k_head_g'])))
    if rope:
        ins.append(vec(rep(p['k_head_g'][sw64])))
        tab = lambda a: (a, pl.BlockSpec((tm, LANES), lambda i: (i % blocks_per_batch, 0)))
        ins += [tab(tables['c64']), tab(tables['s64'])]
    outs = [(1024, BF16), (1024, BF16), (1024, BF16), (1024, BF16)]
    if not rope:
        outs += [(256, F32), (256, F32)]
    return pl.pallas_call(
        functools.partial(_prep_win_kernel, rope),
        grid=(T // tm,),
        in_specs=[s for _, s in ins],
        out_specs=[tok(w) for w, _ in outs],
        out_shape=[jax.ShapeDtypeStruct((T, w), dt) for w, dt in outs],
        compiler_params=_params(1),
        name="prep_win_rope" if rope else "prep_win",
    )(*[a for a, _ in ins])


def _win_attn_kernel(local, *refs):
    if local:
        sink_ref, q_ref, kf_ref, vf_ref, k_ref, v_ref, g_ref, bias_ref, o_ref, s_ref = refs
    else:
        sink_ref, q_ref, kf_ref, vf_ref, g_ref, o_ref, s_ref = refs
    tq = q_ref.shape[1]
    nf = kf_ref.shape[1]
    kv_per_step = q_ref.shape[2] // (2 * LANES)
    upper = lax.broadcasted_iota(jnp.int32, (2 * tq, 1), 0) >= tq
    lane = lax.broadcasted_iota(jnp.int32, (2 * tq, LANES), 1)
    if local:
        span = bias_ref.shape[2]
        start = jnp.clip(pl.program_id(2) * tq - WINDOW, 0, k_ref.shape[1] - span)
        start = pl.multiple_of(start, WINDOW)
    dn = (((1,), (1,)), ((), ()))
    units = [(b, jj, half) for b in range(q_ref.shape[0]) for jj in range(kv_per_step) for half in range(2)]
    m_prev, sk_prev, out = None, None, None
    for idx in range(len(units) + 1):
        cur = units[idx] if idx < len(units) else None
        prev = units[idx - 1] if idx > 0 else None
        if cur is not None:
            cb, cjj, chalf = cur
            base = 2 * LANES * cjj
            csl = slice(base + LANES * chalf, base + LANES * (chalf + 1))
            q2 = jnp.concatenate([q_ref[cb, :, base:base + LANES],
                                  q_ref[cb, :, base + LANES:base + 2 * LANES]], axis=0)
            head = 4 * (pl.program_id(1) * kv_per_step + cjj) + chalf
            sk = jnp.where(upper, sink_ref[head + 2], sink_ref[head]) * LOG2E
            s_ctx = lax.dot_general(q2, kf_ref[cb, :, csl], dn, preferred_element_type=F32)
            s_ref[idx % 2, :, 0:nf] = s_ctx
            mt = _lane_groups(s_ctx, jnp.maximum)
        if prev is not None:
            pb, pjj, phalf = prev
            pbase = 2 * LANES * pjj
            psl = slice(pbase + LANES * phalf, pbase + LANES * (phalf + 1))
            e_ctx = jnp.exp2(s_ref[(idx - 1) % 2, :, 0:nf] - m_prev)
            acc = jnp.dot(e_ctx.astype(BF16), vf_ref[pb, :, psl], preferred_element_type=F32)
        if local and cur is not None:
            s_loc = lax.dot_general(q2, k_ref[cb, pl.ds(start, span), csl], dn,
                                    preferred_element_type=F32) + bias_ref[0]
            s_ref[idx % 2, :, nf:nf + span] = s_loc
            mt = jnp.maximum(mt, _lane_groups(s_loc, jnp.maximum))
        if local and prev is not None:
            e_loc = jnp.exp2(s_ref[(idx - 1) % 2, :, nf:nf + span] - m_prev)
            acc = acc + jnp.dot(e_loc.astype(BF16), v_ref[pb, pl.ds(start, span), psl],
                                preferred_element_type=F32)
        if prev is not None:
            den = jnp.sum(jnp.where(lane == _ones_lane(phalf), acc, 0.0), axis=-1, keepdims=True)
            den = den + jnp.exp2(sk_prev - m_prev)
            oh = jnp.where((lane >= WIN_HEAD_DIM) == (phalf == 1), acc, 0.0) / den
            out = oh if out is None else out + oh
            if phalf == 1:
                gate = g_ref[pb, :, pbase:pbase + 2 * LANES].astype(F32)
                o_ref[pb, :, pbase:pbase + LANES] = (out[0:tq] * gate[:, 0:LANES]).astype(BF16)
                o_ref[pb, :, pbase + LANES:pbase + 2 * LANES] = (
                    out[tq:2 * tq] * gate[:, LANES:2 * LANES]).astype(BF16)
                out = None
        if cur is not None:
            m_prev = jnp.maximum(jnp.max(mt, axis=-1, keepdims=True), sk)
            sk_prev = sk


def _window_bias(tq):
    span = tq + 2 * WINDOW
    qi = jnp.arange(2 * tq)[:, None] % tq
    kk = jnp.arange(span)[None, :]
    cases = [jnp.abs(kk - off - qi) <= WINDOW for off in (0, WINDOW, 2 * WINDOW)]
    return jnp.where(jnp.stack(cases), 0.0, NEG).astype(F32)


def _win_attn(q, kf, vf, own, gates, sink, bb, tq, kv_per_step):
    B, L, _ = q.shape
    nb = L // tq
    width = 2 * LANES * kv_per_step
    qspec = pl.BlockSpec((bb, tq, width), lambda b, h, i: (b, i, h))
    fspec = pl.BlockSpec((bb, kf.shape[1], width), lambda b, h, i: (b, 0, h))
    in_specs = [pl.BlockSpec(memory_space=pltpu.SMEM), qspec, fspec, fspec]
    args = [sink, q, kf, vf]
    n_keys = kf.shape[1]
    if own is not None:
        assert nb >= 3 and bb == 1
        kspec = pl.BlockSpec((bb, L, width), lambda b, h, i: (b, 0, h), pipeline_mode=pl.Buffered(1))
        in_specs += [kspec, kspec]
        args += list(own)
    in_specs.append(qspec)
    args.append(gates)
    if own is not None:
        bias = _window_bias(tq)
        n_keys += bias.shape[2]
        in_specs.append(pl.BlockSpec(
            (1,) + bias.shape[1:],
            lambda b, h, i: (jnp.where(i == 0, 0, jnp.where(i == nb - 1, 2, 1)), 0, 0)))
        args.append(bias)
    return pl.pallas_call(
        functools.partial(_win_attn_kernel, own is not None),
        grid=(B // bb, WIN_KV_HEADS // kv_per_step, nb),
        in_specs=in_specs,
        out_specs=qspec,
        out_shape=jax.ShapeDtypeStruct((B, L, WIN_WIDTH), BF16),
        scratch_shapes=[pltpu.VMEM((2, 2 * tq, n_keys), F32)],
        compiler_params=_params(3),
        name="win_attn_local" if own is not None else "win_attn",
    )(*args)


def _out_kernel(n_parts, *refs):
    x_ref, mod_ref = refs[0], refs[1]
    parts = refs[2:2 + 2 * n_parts]
    o_ref = refs[2 + 2 * n_parts]
    y = None
    for i in range(n_parts):
        t = jnp.dot(parts[2 * i][...], parts[2 * i + 1][...], preferred_element_type=F32)
        y = t if y is None else y + t
    gate = mod_ref[0, :, 2 * D_MODEL:3 * D_MODEL]
    o_ref[...] = x_ref[...] + gate * y


def _out_proj(x, mod, parts, tokens_per_batch, mod_row, tm):
    T = x.shape[0]
    blocks_per_batch = tokens_per_batch // tm
    tok = lambda w: pl.BlockSpec((tm, w), lambda i: (i, 0))
    if mod_row is None:
        mod_spec = pl.BlockSpec((1, 1, 3 * D_MODEL), lambda i: (i // blocks_per_batch, 0, 0))
    else:
        mod_spec = pl.BlockSpec((1, 1, 3 * D_MODEL), lambda i: (mod_row, 0, 0))
    in_specs = [tok(D_MODEL), mod_spec]
    args = [x, mod]
    for a, w in parts:
        in_specs += [tok(a.shape[1]), _const_spec(w.shape)]
        args += [a, w]
    return pl.pallas_call(
        functools.partial(_out_kernel, len(parts)),
        grid=(T // tm,),
        in_specs=in_specs,
        out_specs=tok(D_MODEL),
        out_shape=jax.ShapeDtypeStruct((T, D_MODEL), F32),
        compiler_params=_params(1),
        name="out_proj",
    )(*args)


def kernel(x_prompt, x_sample, cache_l0_mla_ckv, cache_l0_mla_krope, state_l0_ret_fwd, state_l0_ret_bwd, cache_l1_win_k, cache_l1_win_v, cache_l2_mla_ckv, cache_l2_mla_krope, state_l2_ret_fwd, state_l2_ret_bwd, cache_l3_win_k, cache_l3_win_v, c, c_ctx, ada_w, ada_b, norm_g, ab_w_in, mla_q_norm_g, mla_w_uq, mla_kv_norm_g, mla_w_ukv, mla_q_head_g, mla_k_head_g, ret_decay, ret_norm_g, ab_w_out, win_w_in, win_q_head_g, win_k_head_g, win_sink, win_w_out):
    BP, LP, D = x_prompt.shape
    BS, LS, _ = x_sample.shape
    P = cache_l0_mla_ckv.shape[1]
    ctx_caches = ((cache_l0_mla_ckv, cache_l0_mla_krope, state_l0_ret_fwd, state_l0_ret_bwd),
                  (cache_l1_win_k, cache_l1_win_v),
                  (cache_l2_mla_ckv, cache_l2_mla_krope, state_l2_ret_fwd, state_l2_ret_bwd),
                  (cache_l3_win_k, cache_l3_win_v))

    cond = jnp.concatenate([c, c_ctx[None, :], jnp.zeros((8 - BS - 1, D), F32)], axis=0)
    mod_all = _ada_mod(cond, ada_w, ada_b).reshape(DEPTH, 8, 1, 3 * D)
    ctx_row = BS

    c32, s32 = _rope_tables(LS, MLA_ROPE)
    c64, s64 = _rope_tables(LS, RET_DK)
    ones = jnp.ones((LS, MLA_NOPE), F32)
    zeros = jnp.zeros((LS, MLA_NOPE), F32)
    z32 = jnp.zeros((LS, LANES - MLA_QK), F32)
    tables = dict(cm=jnp.concatenate([ones, c32, z32], -1), sm=jnp.concatenate([zeros, s32, z32], -1),
                  c64=jnp.concatenate([c64, c64], -1), s64=jnp.concatenate([s64, s64], -1))

    y_p = x_prompt.reshape(BP * LP, D)
    y_s = x_sample.reshape(BS * LS, D)
    tm = 512
    new_state = []
    for l in range(DEPTH):
        i = l // 2
        mod = mod_all[l]
        if l % 2 == 0:
            p = {'w_in': ab_w_in[i], 'q_norm_g': mla_q_norm_g[i], 'w_uq': mla_w_uq[i],
                 'kv_norm_g': mla_kv_norm_g[i], 'w_ukv': mla_w_ukv[i], 'q_head_g': mla_q_head_g[i],
                 'k_head_g': mla_k_head_g[i]}
            w_out = ab_w_out[i].astype(BF16)
            parts_w = (w_out[:MLA_WIDTH], w_out[MLA_WIDTH:])
            ckv_c, krope_c, sf, sb = ctx_caches[l]

            q, k, v, gates, rq, rk, rv, ckv, kr = _prep_ab(y_p, mod, norm_g[l], p, False, LP, ctx_row, None, tm)
            r3 = lambda a: a.reshape(BP, LP, a.shape[-1])
            a_p = _mla_attn(r3(q), r3(k), r3(v), r3(gates), PROMPT_ROWS_PER_STEP, LP, 1)
            zst = jnp.zeros((BP, RET_HEADS, RET_DK, RET_DV), F32)
            r_p, rf, rb = _retention(r3(rq), r3(rk), r3(rv), zst, zst, ret_decay[i], ret_norm_g[i], r3(gates),
                                     PROMPT_ROWS_PER_STEP)
            new_state.append((ckv.reshape(BP, LP, KV_LORA),
                              kr[:, MLA_NOPE:MLA_QK].reshape(BP, LP, MLA_ROPE), rf, rb))
            y_p_next = _out_proj(y_p, mod, [(a_p.reshape(BP * LP, -1), parts_w[0]),
                                            (r_p.reshape(BP * LP, -1), parts_w[1])], LP, ctx_row, tm)

            kr_c = jnp.pad(krope_c.reshape(BS * P, MLA_ROPE), ((0, 0), (MLA_NOPE, LANES - MLA_QK)))
            q, k, v, gates, rq, rk, rv = _prep_ab(y_s, mod, norm_g[l], p, True, LS, None, tables, tm,
                                                  (ckv_c.reshape(BS * P, KV_LORA), kr_c))
            r3 = lambda a: a.reshape(BS, -1, a.shape[-1])
            a_s = _mla_attn(r3(q), r3(k), r3(v), r3(gates), 1, 256, 2)
            r_s, _, _ = _retention(r3(rq), r3(rk), r3(rv), sf, sb, ret_decay[i], ret_norm_g[i], r3(gates), 1)
            y_s_next = _out_proj(y_s, mod, [(a_s.reshape(BS * LS, -1), parts_w[0]),
                                            (r_s.reshape(BS * LS, -1), parts_w[1])], LS, None, tm)
        else:
            p = {'w_in': win_w_in[i], 'q_head_g': win_q_head_g[i], 'k_head_g': win_k_head_g[i]}
            w_out = win_w_out[i].astype(BF16)
            sink = win_sink[i]
            kc, vc = ctx_caches[l]

            q, k, v, gates, kst, vst = _prep_win(y_p, mod, norm_g[l], p, False, LP, ctx_row, None, tm)
            r3 = lambda a: a.reshape(BP, LP, a.shape[-1])
            o_p = _win_attn(r3(q), r3(k), r3(v), None, r3(gates), sink, PROMPT_ROWS_PER_STEP, LP, 1)
            new_state.append((kst.reshape(BP, LP, WIN_KV_HEADS, WIN_HEAD_DIM),
                              vst.reshape(BP, LP, WIN_KV_HEADS, WIN_HEAD_DIM)))
            y_p_next = _out_proj(y_p, mod, [(o_p.reshape(BP * LP, -1), w_out)], LP, ctx_row, tm)

            q, k, v, gates = _prep_win(y_s, mod, norm_g[l], p, True, LS, None, tables, tm)
            r3 = lambda a: a.reshape(BS, -1, a.shape[-1])
            def slots(a, fill):
                a = a.astype(BF16)
                f = jnp.broadcast_to(fill.astype(BF16), a.shape)
                return jnp.concatenate([a, f, f, a], axis=-1).reshape(BS, P, 4 * 2 * LANES)

            zero = jnp.zeros((WIN_HEAD_DIM,), F32)
            o_s = _win_attn(r3(q), slots(kc, zero), slots(vc, zero.at[0].set(1.0)), (r3(k), r3(v)), r3(gates),
                            sink, 1, 256, 4)
            y_s_next = _out_proj(y_s, mod, [(o_s.reshape(BS * LS, -1), w_out)], LS, None, tm)
        y_p, y_s = y_p_next, y_s_next

    (l0_ckv, l0_krope, l0_rf, l0_rb), (l1_k, l1_v), (l2_ckv, l2_krope, l2_rf, l2_rb), (l3_k, l3_v) = new_state
    return (y_p.reshape(BP, LP, D), y_s.reshape(BS, LS, D), l0_ckv, l0_krope, l0_rf, l0_rb, l1_k, l1_v,
            l2_ckv, l2_krope, l2_rf, l2_rb, l3_k, l3_v)
```

```python
import functools

import numpy as np
import jax
import jax.numpy as jnp
from jax import lax
from jax.experimental import pallas as pl
from jax.experimental.pallas import tpu as pltpu

D_MODEL = 1024
DEPTH = 4
GRID_W = 64
ROPE_BASE = 10000.0
EPS = 1e-6

MLA_HEADS = 8
MLA_NOPE = 64
MLA_ROPE = 32
MLA_QK = MLA_NOPE + MLA_ROPE
MLA_V = 64
Q_LORA = 384
KV_LORA = 256
MLA_WIDTH = MLA_HEADS * MLA_V

RET_HEADS = 4
RET_DK = 64
RET_DV = 128
RET_CHUNK = 256
RET_WIDTH = RET_HEADS * RET_DV

WIN_HEADS = 16
WIN_KV_HEADS = 4
WIN_HEAD_DIM = 64
WINDOW = 128
WIN_WIDTH = WIN_HEADS * WIN_HEAD_DIM

LANES = 128
F32 = jnp.float32
BF16 = jnp.bfloat16
NEG = -1e30
LOG2E = 1.4426950408889634
VMEM_LIMIT = 52 * 1024 * 1024
PROMPT_ROWS_PER_STEP = 4
MLA_KEY_CHUNK = 512

AB_OFF = dict(cq=0, ckv=384, krope=640, ga=672, rq=1184, rk=1440, rv=1696, gb=2208)
AB_LAYOUT = dict(kr=640, g=768, rq=1792, rk=2048, rv=2304)
WIN_OFF = dict(q=0, k=1024, v=1280, g=1536)


def _params(n_axes):
    return pltpu.CompilerParams(dimension_semantics=("arbitrary",) * n_axes, vmem_limit_bytes=VMEM_LIMIT)


def _const_spec(shape):
    nd = len(shape)
    return pl.BlockSpec(shape, lambda *_: (0,) * nd)


def _swap_idx(dim):
    d2 = dim // 2
    half = d2 // 2
    one = np.concatenate([np.arange(half, d2), np.arange(0, half)])
    return np.concatenate([one, d2 + one])


def _take_cols(w, cols):
    cols = np.asarray(cols, np.int32)
    picked = jnp.take(w.astype(BF16), jnp.asarray(np.maximum(cols, 0)), axis=1)
    return jnp.where(jnp.asarray(cols >= 0)[None, :], picked, jnp.zeros((), BF16))


def _rope_tables(n_tokens, dim):
    n_rows = n_tokens // GRID_W
    half = dim // 4
    inv = jnp.power(jnp.float32(ROPE_BASE), -jnp.arange(half, dtype=F32) / half)

    def one(n_pos):
        ang = jnp.arange(n_pos, dtype=F32)[:, None] * inv[None, :]
        c, s = jnp.cos(ang), jnp.sin(ang)
        return jnp.concatenate([c, c], -1), jnp.concatenate([-s, s], -1)

    cr, sr = [jnp.repeat(a, GRID_W, axis=0) for a in one(n_rows)]
    cc, sc = [jnp.tile(a, (n_rows, 1)) for a in one(GRID_W)]
    return jnp.concatenate([cr, cc], -1), jnp.concatenate([sr, sc], -1)


def _ada_kernel(cond_ref, w_ref, b_ref, o_ref):
    c = cond_ref[...]
    sc = (c * jax.nn.sigmoid(c)).astype(BF16)
    o_ref[0] = jnp.dot(sc, w_ref[0].astype(BF16), preferred_element_type=F32) + b_ref[0]


def _ada_mod(cond, ada_w, ada_b):
    tn = 768
    n3 = 3 * D_MODEL
    return pl.pallas_call(
        _ada_kernel,
        grid=(DEPTH, n3 // tn),
        in_specs=[pl.BlockSpec((8, D_MODEL), lambda l, j: (0, 0)),
                  pl.BlockSpec((1, D_MODEL, tn), lambda l, j: (l, 0, j)),
                  pl.BlockSpec((1, 1, tn), lambda l, j: (l, 0, j))],
        out_specs=pl.BlockSpec((1, 8, tn), lambda l, j: (l, 0, j)),
        out_shape=jax.ShapeDtypeStruct((DEPTH, 8, n3), F32),
        compiler_params=_params(2),
        name="ada_mod",
    )(cond, ada_w, ada_b.reshape(DEPTH, 1, n3))


def _rms(x, g):
    return x * lax.rsqrt(jnp.mean(x * x, axis=-1, keepdims=True) + EPS) * g


def _modulated(x, mod_ref, ng_ref):
    shift = mod_ref[0, :, 0:D_MODEL]
    scale = mod_ref[0, :, D_MODEL:2 * D_MODEL]
    return (_rms(x, ng_ref[...]) * (1.0 + scale) + shift).astype(BF16)


def _block_input(x_ref, res_refs, y_out):
    x = x_ref[...]
    if not res_refs:
        return x
    upd = None
    for a_ref, w_ref in zip(res_refs[1::2], res_refs[2::2]):
        t = jnp.dot(a_ref[...], w_ref[...], preferred_element_type=F32)
        upd = t if upd is None else upd + t
    x = x + res_refs[0][0, :, 2 * D_MODEL:3 * D_MODEL] * upd
    y_out[...] = x
    return x


def _seg_rsqrt(x, seg_ref, n_real):
    ssq = jnp.dot((x * x).astype(BF16), seg_ref[...], preferred_element_type=F32)
    return lax.rsqrt(ssq * (1.0 / n_real) + EPS)


def _silu(g):
    return g * jax.nn.sigmoid(g)


def _rope_partner(x, half):
    lane = lax.broadcasted_iota(jnp.int32, (x.shape[0], LANES), 1)
    first = lane % (2 * half) < half
    cols = []
    for c in range(x.shape[1] // LANES):
        xc = x[:, LANES * c:LANES * (c + 1)]
        cols.append(jnp.where(first, pltpu.roll(xc, LANES - half, 1), pltpu.roll(xc, half, 1)))
    return cols[0] if len(cols) == 1 else jnp.concatenate(cols, axis=-1)


def _ones_lane(head):
    return MLA_V if head % 2 == 0 else 0


def _with_ones_lane(v):
    lane = lax.broadcasted_iota(jnp.int32, v.shape, 1)
    odd = (lane // LANES) % 2 == 1
    return jnp.where(lane % LANES == jnp.where(odd, _ones_lane(1), _ones_lane(0)), 1.0, v)


def _ctx_kv(ckv_ref, kr_ref, wk_ref, wv_ref, seg_ref, kg_ref, k_out, v_out):
    ckv = ckv_ref[...].astype(BF16)
    kn = jnp.dot(ckv, wk_ref[...], preferred_element_type=F32)
    v_out[...] = _with_ones_lane(jnp.dot(ckv, wv_ref[...], preferred_element_type=F32)).astype(BF16)
    kr = kr_ref[...]
    kg = kg_ref[...]
    kr2 = jnp.concatenate([kr, kr], axis=-1)
    kg2 = jnp.concatenate([kg, kg], axis=-1)
    for g in range(MLA_HEADS // 2):
        sl = slice(2 * LANES * g, 2 * LANES * (g + 1))
        kraw = kn[:, sl] + kr2
        k_out[:, sl] = (kraw * kg2 * _seg_rsqrt(kraw, seg_ref, MLA_QK)).astype(BF16)


def _prep_ab_kernel(rope, ctx_blocks, steps_per_batch, n_res, *refs):
    n_in = 18 if rope else 12
    n_ctx = 2 if ctx_blocks else 0
    main_in = refs[:n_in]
    ctx_in = refs[n_in:n_in + n_ctx]
    res_in = refs[n_in + n_ctx:n_in + n_ctx + n_res]
    outs = refs[n_in + n_ctx + n_res:]
    main_out, y_out = (outs[:-1], outs[-1]) if n_res else (outs, None)

    def own():
        x = _block_input(main_in[0], res_in, y_out)
        _prep_ab_body(rope, x, *main_in[1:], *main_out)

    if not ctx_blocks:
        own()
        return
    wk_ref, wv_ref, seg_ref, kg_ref = main_in[7], main_in[8], main_in[9], main_in[12]
    is_ctx = pl.program_id(0) % steps_per_batch < ctx_blocks

    @pl.when(is_ctx)
    def _():
        _ctx_kv(*ctx_in, wk_ref, wv_ref, seg_ref, kg_ref, main_out[1], main_out[2])

    pl.when(jnp.logical_not(is_ctx))(own)


def _prep_ab_body(rope, x, *refs):
    if rope:
        (mod_ref, ng_ref, win_ref, qng_ref, kvng_ref, wq_ref, wk_ref, wv_ref, seg_ref,
         qg_ref, qgs_ref, kg_ref, kgs_ref, cm_ref, sm_ref, cr_ref, sr_ref,
         q_out, k_out, v_out, gate_out, rq_out, rk_out, rv_out) = refs
    else:
        (mod_ref, ng_ref, win_ref, qng_ref, kvng_ref, wq_ref, wk_ref, wv_ref, seg_ref,
         qg_ref, kg_ref,
         q_out, k_out, v_out, gate_out, rq_out, rk_out, rv_out, ckv_out, kr_out) = refs
    o_kr, o_g, o_rq, o_rk, o_rv = AB_LAYOUT['kr'], AB_LAYOUT['g'], AB_LAYOUT['rq'], AB_LAYOUT['rk'], AB_LAYOUT['rv']

    h = _modulated(x, mod_ref, ng_ref)
    proj = jnp.dot(h, win_ref[...], preferred_element_type=F32)

    cqn = _rms(proj[:, 0:Q_LORA], qng_ref[...]).astype(BF16)
    ckvn_f = _rms(proj[:, Q_LORA:Q_LORA + KV_LORA], kvng_ref[...])
    ckvn = ckvn_f.astype(BF16)
    kr = proj[:, o_kr:o_kr + LANES]

    gate_out[...] = _silu(proj[:, o_g:o_g + 2 * MLA_WIDTH]).astype(BF16)
    rv_out[...] = proj[:, o_rv:o_rv + RET_WIDTH].astype(BF16)

    qm = jnp.dot(cqn, wq_ref[...], preferred_element_type=F32)
    kn = jnp.dot(ckvn, wk_ref[...], preferred_element_type=F32)
    v_out[...] = _with_ones_lane(jnp.dot(ckvn, wv_ref[...], preferred_element_type=F32)).astype(BF16)

    qscale = MLA_QK ** -0.5 * LOG2E
    if rope:
        krs = _rope_partner(kr, MLA_ROPE // 4)
        cm, sm = cm_ref[...], sm_ref[...]
        q_c = cm * qg_ref[...]
        q_s = sm * qgs_ref[...]
        kr_rot = kr * (cm * kg_ref[...]) + krs * (sm * kgs_ref[...])
        cr, sr = cr_ref[...], sr_ref[...]
        rq = proj[:, o_rq:o_rq + 256]
        rk = proj[:, o_rk:o_rk + 256]
        rqs = _rope_partner(rq, RET_DK // 4)
        rks = _rope_partner(rk, RET_DK // 4)
        for c in range(2):
            sl = slice(LANES * c, LANES * (c + 1))
            rq_out[:, sl] = ((rq[:, sl] * cr + rqs[:, sl] * sr) * (RET_DK ** -0.5)).astype(BF16)
            rk_out[:, sl] = (rk[:, sl] * cr + rks[:, sl] * sr).astype(BF16)
    else:
        q_c = jnp.broadcast_to(qg_ref[...], (qm.shape[0], LANES))
        kr_rot = kr * kg_ref[...]
        rq_out[...] = (proj[:, o_rq:o_rq + 256] * (RET_DK ** -0.5)).astype(BF16)
        rk_out[...] = proj[:, o_rk:o_rk + 256].astype(BF16)
        ckv_out[...] = ckvn_f
        kr_out[...] = kr

    kg = kg_ref[...]
    q_c2 = jnp.concatenate([q_c, q_c], axis=-1)
    kr2 = jnp.concatenate([kr, kr], axis=-1)
    kr_rot2 = jnp.concatenate([kr_rot, kr_rot], axis=-1)
    kg2 = jnp.concatenate([kg, kg], axis=-1)
    if rope:
        q_s2 = jnp.concatenate([q_s, q_s], axis=-1)
    for g in range(MLA_HEADS // 2):
        sl = slice(2 * LANES * g, 2 * LANES * (g + 1))
        qg = qm[:, sl]
        rq_n = _seg_rsqrt(qg, seg_ref, MLA_QK)
        qv = qg * q_c2
        if rope:
            qv = qv + _rope_partner(qg, MLA_ROPE // 4) * q_s2
        q_out[:, sl] = (qv * (rq_n * qscale)).astype(BF16)
        kgp = kn[:, sl]
        rk_n = _seg_rsqrt(kgp + kr2, seg_ref, MLA_QK)
        k_out[:, sl] = ((kgp * kg2 + kr_rot2) * rk_n).astype(BF16)


def _ab_weights(w_in, w_uq, w_ukv, q_head_g, k_head_g, rope):
    sw32 = _swap_idx(MLA_ROPE)
    z = lambda n: [-1] * n
    o = AB_OFF
    cols = list(range(o['cq'], o['cq'] + Q_LORA)) + list(range(o['ckv'], o['ckv'] + KV_LORA))
    cols += z(64) + list(range(o['krope'], o['krope'] + MLA_ROPE)) + z(32)
    cols += list(range(o['ga'], o['ga'] + MLA_WIDTH)) + list(range(o['gb'], o['gb'] + RET_WIDTH))
    cols += list(range(o['rq'], o['rq'] + 256)) + list(range(o['rk'], o['rk'] + 256))
    cols += list(range(o['rv'], o['rv'] + RET_WIDTH))
    assert len(cols) == AB_LAYOUT['rv'] + RET_WIDTH
    win = _take_cols(w_in, cols).astype(BF16)

    qc, kc, vc = [], [], []
    for hh in range(MLA_HEADS):
        qb = MLA_QK * hh
        qc += list(range(qb, qb + MLA_QK)) + z(32)
        kb = (MLA_NOPE + MLA_V) * hh
        kc += list(range(kb, kb + MLA_NOPE)) + z(64)
        vcols = list(range(kb + MLA_NOPE, kb + MLA_NOPE + MLA_V))
        vc += (vcols + z(64)) if hh % 2 == 0 else (z(64) + vcols)
    wq = _take_cols(w_uq, qc).astype(BF16)
    wk = _take_cols(w_ukv, kc).astype(BF16)
    wv = _take_cols(w_ukv, vc).astype(BF16)

    pad = lambda g: jnp.concatenate([g, jnp.zeros((LANES - MLA_QK,), F32)])[None, :]
    gsw = lambda g: jnp.concatenate([jnp.zeros((MLA_NOPE,), F32), g[MLA_NOPE + sw32],
                                     jnp.zeros((LANES - MLA_QK,), F32)])[None, :]
    qg, kg = pad(q_head_g), pad(k_head_g)
    qgs, kgs = (gsw(q_head_g), gsw(k_head_g)) if rope else (None, None)
    return win, wq, wk, wv, qg, qgs, kg, kgs


def _seg_matrix(width):
    idx = np.arange(2 * LANES) // width
    return jnp.asarray((idx[:, None] == idx[None, :]).astype(np.float32), BF16)


def _prep_ab(x, mod, norm_g, p, rope, tokens_per_batch, mod_row, tables, tm, ctx=None, res=None):
    T = x.shape[0]
    win, wq, wk, wv, qg, qgs, kg, kgs = _ab_weights(
        p['w_in'], p['w_uq'], p['w_ukv'], p['q_head_g'], p['k_head_g'], rope)
    seg = _seg_matrix(LANES)
    blocks_per_batch = max(tokens_per_batch // tm, 1)
    ctx_blocks = 0
    if ctx is not None:
        n_batch = T // tokens_per_batch
        ctx_rows = ctx[0].shape[0] // n_batch
        assert rope and ctx_rows % tm == 0 and tokens_per_batch % tm == 0
        ctx_blocks = ctx_rows // tm
    steps_per_batch = blocks_per_batch + ctx_blocks
    batch = lambda i: i // steps_per_batch
    in_batch = lambda i: jnp.maximum(i % steps_per_batch - ctx_blocks, 0)
    own = lambda i: batch(i) * blocks_per_batch + in_batch(i)
    tok = lambda w: pl.BlockSpec((tm, w), lambda i: (own(i), 0))
    if mod_row is None:
        mod_spec = pl.BlockSpec((1, 1, 3 * D_MODEL), lambda i: (batch(i), 0, 0))
    else:
        mod_spec = pl.BlockSpec((1, 1, 3 * D_MODEL), lambda i: (mod_row, 0, 0))
    vec = lambda a: (a, _const_spec(a.shape))
    ins = [(x, tok(D_MODEL)), (mod, mod_spec), vec(norm_g[None, :]), vec(win),
           vec(p['q_norm_g'][None, :]), vec(p['kv_norm_g'][None, :]), vec(wq),
           vec(wk), vec(wv), vec(seg), vec(qg)]
    if rope:
        ins.append(vec(qgs))
    ins.append(vec(kg))
    if rope:
        ins.append(vec(kgs))
        tab = lambda a: (a, pl.BlockSpec((tm, LANES), lambda i: (in_batch(i), 0)))
        ins += [tab(tables['cm']), tab(tables['sm']), tab(tables['c64']), tab(tables['s64'])]
    kv_rows, kv_spec = T, tok(1024)
    if ctx_blocks:
        cspec = lambda w: pl.BlockSpec((tm, w), lambda i: (
            batch(i) * ctx_blocks + jnp.minimum(i % steps_per_batch, ctx_blocks - 1), 0))
        ins += [(ctx[0], cspec(KV_LORA)), (ctx[1], cspec(LANES))]
        kv_rows = T + ctx[0].shape[0]
        kv_spec = pl.BlockSpec((tm, 1024), lambda i: (i, 0))
    outs = [(T, 1024, BF16, tok(1024)), (kv_rows, 1024, BF16, kv_spec), (kv_rows, 1024, BF16, kv_spec),
            (T, 1024, BF16, tok(1024)), (T, 256, BF16, tok(256)), (T, 256, BF16, tok(256)),
            (T, 512, BF16, tok(512))]
    if not rope:
        outs += [(T, KV_LORA, F32, tok(KV_LORA)), (T, LANES, F32, tok(LANES))]
    n_res = 0
    if res is not None:
        ins.append((res[0], mod_spec))
        for a, w in res[1]:
            ins += [(a, tok(a.shape[1])), vec(w)]
        n_res = 1 + 2 * len(res[1])
        outs.append((T, D_MODEL, F32, tok(D_MODEL)))
    return pl.pallas_call(
        functools.partial(_prep_ab_kernel, rope, ctx_blocks, steps_per_batch, n_res),
        grid=(kv_rows // tm,),
        in_specs=[s for _, s in ins],
        out_specs=[o[3] for o in outs],
        out_shape=[jax.ShapeDtypeStruct(o[:2], o[2]) for o in outs],
        compiler_params=_params(1),
        name="prep_ab_rope" if rope else "prep_ab",
    )(*[a for a, _ in ins])


def _lane_groups(x, op):
    out = x[:, 0:LANES]
    for t in range(1, x.shape[1] // LANES):
        out = op(out, x[:, LANES * t:LANES * (t + 1)])
    return out


def _mla_attn_kernel(tk, q_ref, k_ref, v_ref, g_ref, o_ref, s_ref):
    dn = (((1,), (1,)), ((), ()))
    tq = q_ref.shape[1]
    nk = k_ref.shape[1] // tk
    heads = [(b, j) for b in range(q_ref.shape[0]) for j in range(q_ref.shape[2] // LANES)]
    m_prev, out = None, None
    for idx in range(len(heads) + 1):
        cur = heads[idx] if idx < len(heads) else None
        prev = heads[idx - 1] if idx > 0 else None
        if cur is not None:
            cb, cj = cur
            csl = slice(LANES * cj, LANES * (cj + 1))
            q = q_ref[cb, :, csl]
            mt = None
        if prev is not None:
            pb, pj = prev
            psl = slice(LANES * pj, LANES * (pj + 1))
            acc = jnp.zeros((tq, LANES), F32)
        for c in range(nk):
            rows = slice(c * tk, (c + 1) * tk)
            if cur is not None:
                s = lax.dot_general(q, k_ref[cb, rows, csl], dn, preferred_element_type=F32)
                s_ref[idx % 2, :, rows] = s
                smax = _lane_groups(s, jnp.maximum)
                mt = smax if mt is None else jnp.maximum(mt, smax)
            if prev is not None:
                p = jnp.exp2(s_ref[(idx - 1) % 2, :, rows] - m_prev)
                acc = acc + jnp.dot(p.astype(BF16), v_ref[pb, rows, psl], preferred_element_type=F32)
        if prev is not None:
            lane = lax.broadcasted_iota(jnp.int32, (tq, LANES), 1)
            l = jnp.sum(jnp.where(lane == _ones_lane(pj), acc, 0.0), axis=-1, keepdims=True)
            oj = jnp.where((lane >= MLA_V) == (pj % 2 == 1), acc, 0.0) / l
            out = oj if out is None else out + oj
            if pj % 2 == 1:
                osl = slice(LANES * (pj // 2), LANES * (pj // 2 + 1))
                o_ref[pb, :, osl] = (out * g_ref[pb, :, osl].astype(F32)).astype(BF16)
                out = None
        if cur is not None:
            m_prev = jnp.max(mt, axis=-1, keepdims=True)


def _mla_attn(q, k, v, gates, bb, tq, pairs):
    B, L, _ = q.shape
    Lk = k.shape[1]
    hp = MLA_HEADS // 2 // pairs
    qspec = pl.BlockSpec((bb, tq, 2 * LANES * pairs), lambda b, h, i: (b, i, h))
    kspec = pl.BlockSpec((bb, Lk, 2 * LANES * pairs), lambda b, h, i: (b, 0, h))
    ospec = pl.BlockSpec((bb, tq, LANES * pairs), lambda b, h, i: (b, i, h))
    tk = min(Lk, MLA_KEY_CHUNK)
    assert Lk % tk == 0
    return pl.pallas_call(
        functools.partial(_mla_attn_kernel, tk),
        grid=(B // bb, hp, L // tq),
        in_specs=[qspec, kspec, kspec, ospec],
        out_specs=ospec,
        out_shape=jax.ShapeDtypeStruct((B, L, MLA_WIDTH), BF16),
        scratch_shapes=[pltpu.VMEM((2, tq, Lk), F32)],
        compiler_params=_params(3),
        name="mla_attn",
    )(q, k, v, gates)


def _ret_kernel(n_chunks, decay_ref, q_ref, k_ref, v_ref, sf_ref, sb_ref, gn_ref, gate_ref,
                o_ref, rf_ref, rb_ref, acc_ref, st_ref, dmask_ref, lane_ref, sdec_ref):
    C = RET_CHUNK
    pair = pl.program_id(0)
    bb = q_ref.shape[0]
    lane_hi = lax.broadcasted_iota(jnp.int32, (C, LANES), 1) >= RET_DK

    @pl.when(pl.program_id(1) == 0)
    def _():
        row = lax.broadcasted_iota(jnp.int32, (C, C), 0).astype(F32)
        rel = row - lax.broadcasted_iota(jnp.int32, (C, C), 1).astype(F32)
        lrow = lax.broadcasted_iota(jnp.int32, (C, LANES), 0).astype(F32)
        srow_hi = lax.broadcasted_iota(jnp.int32, (LANES, 2 * RET_DV), 0) >= RET_DK
        scol_hi = lax.broadcasted_iota(jnp.int32, (LANES, 2 * RET_DV), 1) >= RET_DV
        same_head = srow_hi == scol_hi
        sdec_ref[2] = jnp.where(same_head, 1.0, 0.0)
        for d in range(2):
            log_decay = lambda shape, hh: -jnp.exp(jnp.full(shape, decay_ref[d, 2 * pair + hh], F32))
            for hh in range(2):
                lg = log_decay((C, C), hh)
                if d == 0:
                    dmask_ref[2 * d + hh] = jnp.where(rel >= 0, jnp.exp(jnp.maximum(rel, 0.0) * lg), 0.0)
                else:
                    dmask_ref[2 * d + hh] = jnp.where(rel <= 0, jnp.exp(jnp.maximum(-rel, 0.0) * lg), 0.0)
            lg_lane = jnp.where(lane_hi, log_decay((C, LANES), 1), log_decay((C, LANES), 0))
            if d == 0:
                lane_ref[0] = jnp.exp((C - 1.0 - lrow) * lg_lane)
                lane_ref[2] = jnp.exp((lrow + 1.0) * lg_lane)
            else:
                lane_ref[1] = jnp.exp(lrow * lg_lane)
                lane_ref[3] = jnp.exp((C - lrow) * lg_lane)
            lg_row = jnp.where(srow_hi, log_decay((LANES, 2 * RET_DV), 1), log_decay((LANES, 2 * RET_DV), 0))
            sdec_ref[d] = jnp.where(same_head, jnp.exp(C * lg_row), 0.0)

    for b in range(bb):
        for d, s_ref in ((0, sf_ref), (1, sb_ref)):
            st_ref[b, d] = jnp.zeros((LANES, 2 * RET_DV), F32)
            for hh in range(2):
                st_ref[b, d, RET_DK * hh:RET_DK * (hh + 1), RET_DV * hh:RET_DV * (hh + 1)] = s_ref[b, hh]
    acc_ref[...] = jnp.zeros_like(acc_ref)
    dn = (((1,), (1,)), ((), ()))

    def step(n, _):
        for b in range(bb):
            for d in range(2):
                cidx = n if d == 0 else n_chunks - 1 - n
                r0 = pl.multiple_of(cidx * C, C)
                qc = q_ref[b, pl.ds(r0, C), :].astype(F32)
                kc = k_ref[b, pl.ds(r0, C), :]
                vc = v_ref[b, pl.ds(r0, C), :]
                q2 = jnp.concatenate([jnp.where(lane_hi, 0.0, qc), jnp.where(lane_hi, qc, 0.0)], axis=0)
                s2 = lax.dot_general(q2.astype(BF16), kc, dn, preferred_element_type=F32)
                intra = [jnp.dot((s2[C * hh:C * (hh + 1)] * dmask_ref[2 * d + hh]).astype(BF16),
                                 vc[:, RET_DV * hh:RET_DV * (hh + 1)], preferred_element_type=F32)
                         for hh in range(2)]
                state = st_ref[b, d]
                inter = jnp.dot((qc * lane_ref[2 + d]).astype(BF16), state.astype(BF16),
                                preferred_element_type=F32)
                u = jnp.dot((kc.astype(F32) * lane_ref[d]).T.astype(BF16), vc, preferred_element_type=F32)
                st_ref[b, d] = sdec_ref[d] * state + u * sdec_ref[2]
                acc_ref[b, pl.ds(r0, C), :] += jnp.concatenate(intra, axis=-1) + inter
        return 0

    lax.fori_loop(0, n_chunks, step, 0)

    for b in range(bb):
        for hh in range(2):
            rf_ref[b, hh] = st_ref[b, 0, RET_DK * hh:RET_DK * (hh + 1), RET_DV * hh:RET_DV * (hh + 1)]
            rb_ref[b, hh] = st_ref[b, 1, RET_DK * hh:RET_DK * (hh + 1), RET_DV * hh:RET_DV * (hh + 1)]

    ones = jnp.ones((RET_DV, RET_DV), BF16)

    def norm(n, _):
        r0 = pl.multiple_of(n * C, C)
        for b in range(bb):
            for hh in range(2):
                sl = slice(RET_DV * hh, RET_DV * (hh + 1))
                y = acc_ref[b, pl.ds(r0, C), sl]
                yc = y - jnp.mean(y, axis=-1, keepdims=True)
                var = jnp.dot((yc * yc).astype(BF16), ones, preferred_element_type=F32) * (1.0 / RET_DV)
                out = yc * lax.rsqrt(var + EPS) * gn_ref[:, sl]
                o_ref[b, pl.ds(r0, C), sl] = (out * gate_ref[b, pl.ds(r0, C), sl].astype(F32)).astype(BF16)
        return 0

    lax.fori_loop(0, n_chunks, norm, 0)


def _retention(rq, rk, rv, sf, sb, decay, ret_norm_g, gates, bb):
    B, L, _ = rq.shape
    C = RET_CHUNK
    hp = RET_HEADS // 2
    st_spec = pl.BlockSpec((bb, 2, RET_DK, RET_DV), lambda h, b: (b, h, 0, 0))
    st_shape = jax.ShapeDtypeStruct((B, RET_HEADS, RET_DK, RET_DV), F32)
    qk_spec = pl.BlockSpec((bb, L, LANES), lambda h, b: (b, 0, h))
    v_spec = pl.BlockSpec((bb, L, 2 * RET_DV), lambda h, b: (b, 0, h))
    return pl.pallas_call(
        functools.partial(_ret_kernel, L // C),
        grid=(hp, B // bb),
        in_specs=[pl.BlockSpec(memory_space=pltpu.SMEM), qk_spec, qk_spec, v_spec, st_spec, st_spec,
                  pl.BlockSpec((1, 2 * RET_DV), lambda h, b: (0, h)),
                  pl.BlockSpec((bb, L, 2 * RET_DV), lambda h, b: (b, 0, hp + h))],
        out_specs=[v_spec, st_spec, st_spec],
        out_shape=[jax.ShapeDtypeStruct((B, L, RET_WIDTH), BF16), st_shape, st_shape],
        scratch_shapes=[pltpu.VMEM((bb, L, 2 * RET_DV), F32),
                        pltpu.VMEM((bb, 2, LANES, 2 * RET_DV), F32),
                        pltpu.VMEM((4, C, C), F32),
                        pltpu.VMEM((4, C, LANES), F32),
                        pltpu.VMEM((3, LANES, 2 * RET_DV), F32)],
        compiler_params=_params(2),
        name="retention",
    )(decay, rq, rk, rv, sf, sb, ret_norm_g[None, :], gates)


def _prep_win_kernel(rope, n_res, *refs):
    n_in = 12 if rope else 8
    res_in = refs[n_in:n_in + n_res]
    outs = refs[n_in + n_res:]
    main_out, y_out = (outs[:-1], outs[-1]) if n_res else (outs, None)
    if rope:
        x_ref, mod_ref, ng_ref, win_ref, seg_ref, exp_ref, qg_ref, qgs_ref, kg_ref, kgs_ref, c_ref, s_ref = refs[:n_in]
        q_out, k_out, v_out, gate_out = main_out
    else:
        x_ref, mod_ref, ng_ref, win_ref, seg_ref, exp_ref, qg_ref, kg_ref = refs[:n_in]
        q_out, k_out, v_out, gate_out, kst_out, vst_out = main_out
    o = WIN_OFF
    h = _modulated(_block_input(x_ref, res_in, y_out), mod_ref, ng_ref)
    proj = jnp.dot(h, win_ref[...], preferred_element_type=F32)
    gate_out[...] = _silu(proj[:, o['g']:o['g'] + WIN_WIDTH]).astype(BF16)

    qg2 = jnp.concatenate([qg_ref[...]] * 2, axis=-1)
    kg2 = jnp.concatenate([kg_ref[...]] * 2, axis=-1)
    if rope:
        c2 = jnp.concatenate([c_ref[...]] * 2, axis=-1)
        s2 = jnp.concatenate([s_ref[...]] * 2, axis=-1)
        q_c, q_s = c2 * qg2, s2 * jnp.concatenate([qgs_ref[...]] * 2, axis=-1)
        k_c, k_s = c2 * kg2, s2 * jnp.concatenate([kgs_ref[...]] * 2, axis=-1)
    qscale = WIN_HEAD_DIM ** -0.5 * LOG2E
    for g in range(WIN_WIDTH // 256):
        sl = slice(256 * g, 256 * (g + 1))
        qg = proj[:, sl]
        rn = _seg_rsqrt(qg, seg_ref, WIN_HEAD_DIM) * qscale
        if rope:
            qv = qg * q_c + _rope_partner(qg, WIN_HEAD_DIM // 4) * q_s
        else:
            qv = qg * qg2
        q_out[:, sl] = (qv * rn).astype(BF16)
    kraw = proj[:, o['k']:o['k'] + 256]
    rn = _seg_rsqrt(kraw, seg_ref, WIN_HEAD_DIM)
    if rope:
        kn = (kraw * k_c + _rope_partner(kraw, WIN_HEAD_DIM // 4) * k_s) * rn
    else:
        kn = kraw * kg2 * rn
    v = proj[:, o['v']:o['v'] + 256]
    if not rope:
        kst_out[...] = kn
        vst_out[...] = v
    k_out[...] = jnp.dot(kn.astype(BF16), exp_ref[...], preferred_element_type=F32).astype(BF16)
    v_out[...] = _with_ones_lane(jnp.dot(v.astype(BF16), exp_ref[...], preferred_element_type=F32)).astype(BF16)


def _win_expand_matrix():
    e = np.zeros((256, 1024), np.float32)
    for j in range(WIN_KV_HEADS):
        for i in range(WIN_HEAD_DIM):
            e[WIN_HEAD_DIM * j + i, 256 * j + i] = 1.0
            e[WIN_HEAD_DIM * j + i, 256 * j + 192 + i] = 1.0
    return jnp.asarray(e, BF16)


def _prep_win(x, mod, norm_g, p, rope, tokens_per_batch, mod_row, tables, tm, res=None):
    T = x.shape[0]
    sw64 = _swap_idx(WIN_HEAD_DIM)
    win = p['w_in'].astype(BF16)
    seg = _seg_matrix(WIN_HEAD_DIM)
    expand = _win_expand_matrix()
    rep = lambda g: jnp.concatenate([g, g])[None, :]
    blocks_per_batch = tokens_per_batch // tm
    tok = lambda w: pl.BlockSpec((tm, w), lambda i: (i, 0))
    if mod_row is None:
        mod_spec = pl.BlockSpec((1, 1, 3 * D_MODEL), lambda i: (i // blocks_per_batch, 0, 0))
    else:
        mod_spec = pl.BlockSpec((1, 1, 3 * D_MODEL), lambda i: (mod_row, 0, 0))
    vec = lambda a: (a, _const_spec(a.shape))
    ins = [(x, tok(D_MODEL)), (mod, mod_spec), vec(norm_g[None, :]), vec(win), vec(seg), vec(expand),
           vec(rep(p['q_head_g']))]
    if rope:
        ins.append(vec(rep(p['q_head_g'][sw64])))
    ins.append(vec(rep(p['k_head_g'])))
    if rope:
        ins.append(vec(rep(p['k_head_g'][sw64])))
        tab = lambda a: (a, pl.BlockSpec((tm, LANES), lambda i: (i % blocks_per_batch, 0)))
        ins += [tab(tables['c64']), tab(tables['s64'])]
    outs = [(1024, BF16), (1024, BF16), (1024, BF16), (1024, BF16)]
    if not rope:
        outs += [(256, F32), (256, F32)]
    n_res = 0
    if res is not None:
        ins.append((res[0], mod_spec))
        for a, w in res[1]:
            ins += [(a, tok(a.shape[1])), vec(w)]
        n_res = 1 + 2 * len(res[1])
        outs.append((D_MODEL, F32))
    return pl.pallas_call(
        functools.partial(_prep_win_kernel, rope, n_res),
        grid=(T // tm,),
        in_specs=[s for _, s in ins],
        out_specs=[tok(w) for w, _ in outs],
        out_shape=[jax.ShapeDtypeStruct((T, w), dt) for w, dt in outs],
        compiler_params=_params(1),
        name="prep_win_rope" if rope else "prep_win",
    )(*[a for a, _ in ins])


def _win_attn_kernel(local, *refs):
    if local:
        sink_ref, q_ref, kf_ref, vf_ref, k_ref, v_ref, g_ref, bias_ref, o_ref, s_ref = refs
    else:
        sink_ref, q_ref, kf_ref, vf_ref, g_ref, o_ref, s_ref = refs
    tq = q_ref.shape[1]
    nf = kf_ref.shape[1]
    kv_per_step = q_ref.shape[2] // (2 * LANES)
    upper = lax.broadcasted_iota(jnp.int32, (2 * tq, 1), 0) >= tq
    lane = lax.broadcasted_iota(jnp.int32, (2 * tq, LANES), 1)
    if local:
        span = bias_ref.shape[2]
        start = jnp.clip(pl.program_id(2) * tq - WINDOW, 0, k_ref.shape[1] - span)
        start = pl.multiple_of(start, WINDOW)
    dn = (((1,), (1,)), ((), ()))
    units = [(b, jj, half) for b in range(q_ref.shape[0]) for jj in range(kv_per_step) for half in range(2)]
    m_prev, sk_prev, out = None, None, None
    for idx in range(len(units) + 1):
        cur = units[idx] if idx < len(units) else None
        prev = units[idx - 1] if idx > 0 else None
        if cur is not None:
            cb, cjj, chalf = cur
            base = 2 * LANES * cjj
            csl = slice(base + LANES * chalf, base + LANES * (chalf + 1))
            q2 = jnp.concatenate([q_ref[cb, :, base:base + LANES],
                                  q_ref[cb, :, base + LANES:base + 2 * LANES]], axis=0)
            head = 4 * (pl.program_id(1) * kv_per_step + cjj) + chalf
            sk = jnp.where(upper, sink_ref[head + 2], sink_ref[head]) * LOG2E
            s_ctx = lax.dot_general(q2, kf_ref[cb, :, csl], dn, preferred_element_type=F32)
            s_ref[idx % 2, :, 0:nf] = s_ctx
            mt = _lane_groups(s_ctx, jnp.maximum)
        if prev is not None:
            pb, pjj, phalf = prev
            pbase = 2 * LANES * pjj
            psl = slice(pbase + LANES * phalf, pbase + LANES * (phalf + 1))
            e_ctx = jnp.exp2(s_ref[(idx - 1) % 2, :, 0:nf] - m_prev)
            acc = jnp.dot(e_ctx.astype(BF16), vf_ref[pb, :, psl], preferred_element_type=F32)
        if local and cur is not None:
            s_loc = lax.dot_general(q2, k_ref[cb, pl.ds(start, span), csl], dn,
                                    preferred_element_type=F32) + bias_ref[0]
            s_ref[idx % 2, :, nf:nf + span] = s_loc
            mt = jnp.maximum(mt, _lane_groups(s_loc, jnp.maximum))
        if local and prev is not None:
            e_loc = jnp.exp2(s_ref[(idx - 1) % 2, :, nf:nf + span] - m_prev)
            acc = acc + jnp.dot(e_loc.astype(BF16), v_ref[pb, pl.ds(start, span), psl],
                                preferred_element_type=F32)
        if prev is not None:
            den = jnp.sum(jnp.where(lane == _ones_lane(phalf), acc, 0.0), axis=-1, keepdims=True)
            den = den + jnp.exp2(sk_prev - m_prev)
            oh = jnp.where((lane >= WIN_HEAD_DIM) == (phalf == 1), acc, 0.0) / den
            out = oh if out is None else out + oh
            if phalf == 1:
                gate = g_ref[pb, :, pbase:pbase + 2 * LANES].astype(F32)
                o_ref[pb, :, pbase:pbase + LANES] = (out[0:tq] * gate[:, 0:LANES]).astype(BF16)
                o_ref[pb, :, pbase + LANES:pbase + 2 * LANES] = (
                    out[tq:2 * tq] * gate[:, LANES:2 * LANES]).astype(BF16)
                out = None
        if cur is not None:
            m_prev = jnp.maximum(jnp.max(mt, axis=-1, keepdims=True), sk)
            sk_prev = sk


def _window_bias(tq):
    span = tq + 2 * WINDOW
    qi = jnp.arange(2 * tq)[:, None] % tq
    kk = jnp.arange(span)[None, :]
    cases = [jnp.abs(kk - off - qi) <= WINDOW for off in (0, WINDOW, 2 * WINDOW)]
    return jnp.where(jnp.stack(cases), 0.0, NEG).astype(F32)


def _win_attn(q, kf, vf, own, gates, sink, bb, tq, kv_per_step):
    B, L, _ = q.shape
    nb = L // tq
    width = 2 * LANES * kv_per_step
    qspec = pl.BlockSpec((bb, tq, width), lambda b, h, i: (b, i, h))
    fspec = pl.BlockSpec((bb, kf.shape[1], width), lambda b, h, i: (b, 0, h))
    in_specs = [pl.BlockSpec(memory_space=pltpu.SMEM), qspec, fspec, fspec]
    args = [sink, q, kf, vf]
    n_keys = kf.shape[1]
    if own is not None:
        assert nb >= 3 and bb == 1
        kspec = pl.BlockSpec((bb, L, width), lambda b, h, i: (b, 0, h), pipeline_mode=pl.Buffered(1))
        in_specs += [kspec, kspec]
        args += list(own)
    in_specs.append(qspec)
    args.append(gates)
    if own is not None:
        bias = _window_bias(tq)
        n_keys += bias.shape[2]
        in_specs.append(pl.BlockSpec(
            (1,) + bias.shape[1:],
            lambda b, h, i: (jnp.where(i == 0, 0, jnp.where(i == nb - 1, 2, 1)), 0, 0)))
        args.append(bias)
    return pl.pallas_call(
        functools.partial(_win_attn_kernel, own is not None),
        grid=(B // bb, WIN_KV_HEADS // kv_per_step, nb),
        in_specs=in_specs,
        out_specs=qspec,
        out_shape=jax.ShapeDtypeStruct((B, L, WIN_WIDTH), BF16),
        scratch_shapes=[pltpu.VMEM((2, 2 * tq, n_keys), F32)],
        compiler_params=_params(3),
        name="win_attn_local" if own is not None else "win_attn",
    )(*args)


def _out_kernel(n_parts, *refs):
    x_ref, mod_ref = refs[0], refs[1]
    parts = refs[2:2 + 2 * n_parts]
    o_ref = refs[2 + 2 * n_parts]
    y = None
    for i in range(n_parts):
        t = jnp.dot(parts[2 * i][...], parts[2 * i + 1][...], preferred_element_type=F32)
        y = t if y is None else y + t
    gate = mod_ref[0, :, 2 * D_MODEL:3 * D_MODEL]
    o_ref[...] = x_ref[...] + gate * y


def _out_proj(x, mod, parts, tokens_per_batch, mod_row, tm):
    T = x.shape[0]
    blocks_per_batch = tokens_per_batch // tm
    tok = lambda w: pl.BlockSpec((tm, w), lambda i: (i, 0))
    if mod_row is None:
        mod_spec = pl.BlockSpec((1, 1, 3 * D_MODEL), lambda i: (i // blocks_per_batch, 0, 0))
    else:
        mod_spec = pl.BlockSpec((1, 1, 3 * D_MODEL), lambda i: (mod_row, 0, 0))
    in_specs = [tok(D_MODEL), mod_spec]
    args = [x, mod]
    for a, w in parts:
        in_specs += [tok(a.shape[1]), _const_spec(w.shape)]
        args += [a, w]
    return pl.pallas_call(
        functools.partial(_out_kernel, len(parts)),
        grid=(T // tm,),
        in_specs=in_specs,
        out_specs=tok(D_MODEL),
        out_shape=jax.ShapeDtypeStruct((T, D_MODEL), F32),
        compiler_params=_params(1),
        name="out_proj",
    )(*args)


def kernel(x_prompt, x_sample, cache_l0_mla_ckv, cache_l0_mla_krope, state_l0_ret_fwd, state_l0_ret_bwd, cache_l1_win_k, cache_l1_win_v, cache_l2_mla_ckv, cache_l2_mla_krope, state_l2_ret_fwd, state_l2_ret_bwd, cache_l3_win_k, cache_l3_win_v, c, c_ctx, ada_w, ada_b, norm_g, ab_w_in, mla_q_norm_g, mla_w_uq, mla_kv_norm_g, mla_w_ukv, mla_q_head_g, mla_k_head_g, ret_decay, ret_norm_g, ab_w_out, win_w_in, win_q_head_g, win_k_head_g, win_sink, win_w_out):
    BP, LP, D = x_prompt.shape
    BS, LS, _ = x_sample.shape
    P = cache_l0_mla_ckv.shape[1]
    ctx_caches = ((cache_l0_mla_ckv, cache_l0_mla_krope, state_l0_ret_fwd, state_l0_ret_bwd),
                  (cache_l1_win_k, cache_l1_win_v),
                  (cache_l2_mla_ckv, cache_l2_mla_krope, state_l2_ret_fwd, state_l2_ret_bwd),
                  (cache_l3_win_k, cache_l3_win_v))

    cond = jnp.concatenate([c, c_ctx[None, :], jnp.zeros((8 - BS - 1, D), F32)], axis=0)
    mod_all = _ada_mod(cond, ada_w, ada_b).reshape(DEPTH, 8, 1, 3 * D)
    ctx_row = BS

    c32, s32 = _rope_tables(LS, MLA_ROPE)
    c64, s64 = _rope_tables(LS, RET_DK)
    ones = jnp.ones((LS, MLA_NOPE), F32)
    zeros = jnp.zeros((LS, MLA_NOPE), F32)
    z32 = jnp.zeros((LS, LANES - MLA_QK), F32)
    tables = dict(cm=jnp.concatenate([ones, c32, z32], -1), sm=jnp.concatenate([zeros, s32, z32], -1),
                  c64=jnp.concatenate([c64, c64], -1), s64=jnp.concatenate([s64, s64], -1))

    y_p = x_prompt.reshape(BP * LP, D)
    y_s = x_sample.reshape(BS * LS, D)
    tm = 512
    new_state = []
    pend_p = pend_s = None

    def take_stream(outs, pending, y):
        return (outs[:-1], outs[-1]) if pending is not None else (outs, y)

    for l in range(DEPTH):
        i = l // 2
        mod = mod_all[l]
        p3 = lambda a: a.reshape(BP, LP, a.shape[-1])
        s3 = lambda a: a.reshape(BS, -1, a.shape[-1])
        flat = lambda a: a.reshape(-1, a.shape[-1])
        if l % 2 == 0:
            p = {'w_in': ab_w_in[i], 'q_norm_g': mla_q_norm_g[i], 'w_uq': mla_w_uq[i],
                 'kv_norm_g': mla_kv_norm_g[i], 'w_ukv': mla_w_ukv[i], 'q_head_g': mla_q_head_g[i],
                 'k_head_g': mla_k_head_g[i]}
            w_out = ab_w_out[i].astype(BF16)
            parts_w = (w_out[:MLA_WIDTH], w_out[MLA_WIDTH:])
            ckv_c, krope_c, sf, sb = ctx_caches[l]

            outs, y_p = take_stream(_prep_ab(y_p, mod, norm_g[l], p, False, LP, ctx_row, None, tm, res=pend_p),
                                    pend_p, y_p)
            q, k, v, gates, rq, rk, rv, ckv, kr = outs
            a_p = _mla_attn(p3(q), p3(k), p3(v), p3(gates), PROMPT_ROWS_PER_STEP, LP, 1)
            zst = jnp.zeros((BP, RET_HEADS, RET_DK, RET_DV), F32)
            r_p, rf, rb = _retention(p3(rq), p3(rk), p3(rv), zst, zst, ret_decay[i], ret_norm_g[i], p3(gates),
                                     PROMPT_ROWS_PER_STEP)
            new_state.append((ckv.reshape(BP, LP, KV_LORA),
                              kr[:, MLA_NOPE:MLA_QK].reshape(BP, LP, MLA_ROPE), rf, rb))
            pend_p = (mod, [(flat(a_p), parts_w[0]), (flat(r_p), parts_w[1])])

            kr_c = jnp.pad(krope_c.reshape(BS * P, MLA_ROPE), ((0, 0), (MLA_NOPE, LANES - MLA_QK)))
            outs, y_s = take_stream(_prep_ab(y_s, mod, norm_g[l], p, True, LS, None, tables, tm,
                                             (ckv_c.reshape(BS * P, KV_LORA), kr_c), res=pend_s), pend_s, y_s)
            q, k, v, gates, rq, rk, rv = outs
            a_s = _mla_attn(s3(q), s3(k), s3(v), s3(gates), 1, 512, 2)
            r_s, _, _ = _retention(s3(rq), s3(rk), s3(rv), sf, sb, ret_decay[i], ret_norm_g[i], s3(gates), 1)
            pend_s = (mod, [(flat(a_s), parts_w[0]), (flat(r_s), parts_w[1])])
        else:
            p = {'w_in': win_w_in[i], 'q_head_g': win_q_head_g[i], 'k_head_g': win_k_head_g[i]}
            w_out = win_w_out[i].astype(BF16)
            sink = win_sink[i]
            kc, vc = ctx_caches[l]

            outs, y_p = take_stream(_prep_win(y_p, mod, norm_g[l], p, False, LP, ctx_row, None, tm, res=pend_p),
                                    pend_p, y_p)
            q, k, v, gates, kst, vst = outs
            o_p = _win_attn(p3(q), p3(k), p3(v), None, p3(gates), sink, PROMPT_ROWS_PER_STEP, LP, 1)
            new_state.append((kst.reshape(BP, LP, WIN_KV_HEADS, WIN_HEAD_DIM),
                              vst.reshape(BP, LP, WIN_KV_HEADS, WIN_HEAD_DIM)))
            pend_p = (mod, [(flat(o_p), w_out)])

            outs, y_s = take_stream(_prep_win(y_s, mod, norm_g[l], p, True, LS, None, tables, tm, res=pend_s),
                                    pend_s, y_s)
            q, k, v, gates = outs

            def slots(a, fill):
                a = a.astype(BF16)
                f = jnp.broadcast_to(fill.astype(BF16), a.shape)
                return jnp.concatenate([a, f, f, a], axis=-1).reshape(BS, P, 4 * 2 * LANES)

            zero = jnp.zeros((WIN_HEAD_DIM,), F32)
            o_s = _win_attn(s3(q), slots(kc, zero), slots(vc, zero.at[0].set(1.0)), (s3(k), s3(v)), s3(gates),
                            sink, 1, 256, 4)
            pend_s = (mod, [(flat(o_s), w_out)])

    y_p = _out_proj(y_p, pend_p[0], pend_p[1], LP, ctx_row, tm)
    y_s = _out_proj(y_s, pend_s[0], pend_s[1], LS, None, tm)

    (l0_ckv, l0_krope, l0_rf, l0_rb), (l1_k, l1_v), (l2_ckv, l2_krope, l2_rf, l2_rb), (l3_k, l3_v) = new_state
    return (y_p.reshape(BP, LP, D), y_s.reshape(BS, LS, D), l0_ckv, l0_krope, l0_rf, l0_rb, l1_k, l1_v,
            l2_ckv, l2_krope, l2_rf, l2_rb, l3_k, l3_v)
```

```python
import functools

import numpy as np
import jax
import jax.numpy as jnp
from jax import lax
from jax.experimental import pallas as pl
from jax.experimental.pallas import tpu as pltpu

D_MODEL = 1024
DEPTH = 4
GRID_W = 64
ROPE_BASE = 10000.0
EPS = 1e-6

MLA_HEADS = 8
MLA_NOPE = 64
MLA_ROPE = 32
MLA_QK = MLA_NOPE + MLA_ROPE
MLA_V = 64
Q_LORA = 384
KV_LORA = 256
MLA_WIDTH = MLA_HEADS * MLA_V

RET_HEADS = 4
RET_DK = 64
RET_DV = 128
RET_CHUNK = 256
RET_WIDTH = RET_HEADS * RET_DV

WIN_HEADS = 16
WIN_KV_HEADS = 4
WIN_HEAD_DIM = 64
WINDOW = 128
WIN_WIDTH = WIN_HEADS * WIN_HEAD_DIM

LANES = 128
F32 = jnp.float32
BF16 = jnp.bfloat16
NEG = -1e30
LOG2E = 1.4426950408889634
VMEM_LIMIT = 52 * 1024 * 1024
PROMPT_ROWS_PER_STEP = 8
MLA_KEY_CHUNK = 512

AB_OFF = dict(cq=0, ckv=384, krope=640, ga=672, rq=1184, rk=1440, rv=1696, gb=2208)
AB_LAYOUT = dict(kr=640, g=768, rq=1792, rk=2048, rv=2304)
WIN_OFF = dict(q=0, k=1024, v=1280, g=1536)


def _params(n_axes):
    return pltpu.CompilerParams(dimension_semantics=("arbitrary",) * n_axes, vmem_limit_bytes=VMEM_LIMIT)


def _const_spec(shape):
    nd = len(shape)
    return pl.BlockSpec(shape, lambda *_: (0,) * nd)


def _swap_idx(dim):
    d2 = dim // 2
    half = d2 // 2
    one = np.concatenate([np.arange(half, d2), np.arange(0, half)])
    return np.concatenate([one, d2 + one])


def _take_cols(w, cols):
    w = w.astype(BF16)
    pieces, i = [], 0
    while i < len(cols):
        j = i + 1
        if cols[i] < 0:
            while j < len(cols) and cols[j] < 0:
                j += 1
            pieces.append(jnp.zeros((w.shape[0], j - i), BF16))
        else:
            while j < len(cols) and cols[j] == cols[j - 1] + 1:
                j += 1
            pieces.append(w[:, cols[i]:cols[i] + j - i])
        i = j
    return jnp.concatenate(pieces, axis=1)


def _rope_tables(n_tokens, dim):
    n_rows = n_tokens // GRID_W
    half = dim // 4
    inv = jnp.power(jnp.float32(ROPE_BASE), -jnp.arange(half, dtype=F32) / half)

    def one(n_pos):
        ang = jnp.arange(n_pos, dtype=F32)[:, None] * inv[None, :]
        c, s = jnp.cos(ang), jnp.sin(ang)
        return jnp.concatenate([c, c], -1), jnp.concatenate([-s, s], -1)

    cr, sr = [jnp.repeat(a, GRID_W, axis=0) for a in one(n_rows)]
    cc, sc = [jnp.tile(a, (n_rows, 1)) for a in one(GRID_W)]
    return jnp.concatenate([cr, cc], -1), jnp.concatenate([sr, sc], -1)


def _ada_kernel(cond_ref, w_ref, b_ref, o_ref):
    c = cond_ref[...]
    sc = (c * jax.nn.sigmoid(c)).astype(BF16)
    o_ref[0] = jnp.dot(sc, w_ref[0].astype(BF16), preferred_element_type=F32) + b_ref[0]


def _ada_mod(cond, ada_w, ada_b):
    tn = 768
    n3 = 3 * D_MODEL
    return pl.pallas_call(
        _ada_kernel,
        grid=(DEPTH, n3 // tn),
        in_specs=[pl.BlockSpec((8, D_MODEL), lambda l, j: (0, 0)),
                  pl.BlockSpec((1, D_MODEL, tn), lambda l, j: (l, 0, j)),
                  pl.BlockSpec((1, 1, tn), lambda l, j: (l, 0, j))],
        out_specs=pl.BlockSpec((1, 8, tn), lambda l, j: (l, 0, j)),
        out_shape=jax.ShapeDtypeStruct((DEPTH, 8, n3), F32),
        compiler_params=_params(2),
        name="ada_mod",
    )(cond, ada_w, ada_b.reshape(DEPTH, 1, n3))


def _rms(x, g):
    return x * lax.rsqrt(jnp.mean(x * x, axis=-1, keepdims=True) + EPS) * g


def _modulated(x, mod_ref, ng_ref):
    shift = mod_ref[0, :, 0:D_MODEL]
    scale = mod_ref[0, :, D_MODEL:2 * D_MODEL]
    return (_rms(x, ng_ref[...]) * (1.0 + scale) + shift).astype(BF16)


def _block_input(x_ref, res_refs, y_out):
    x = x_ref[...]
    if not res_refs:
        return x
    upd = None
    for a_ref, w_ref in zip(res_refs[1::2], res_refs[2::2]):
        t = jnp.dot(a_ref[...], w_ref[...], preferred_element_type=F32)
        upd = t if upd is None else upd + t
    x = x + res_refs[0][0, :, 2 * D_MODEL:3 * D_MODEL] * upd
    y_out[...] = x
    return x


def _seg_rsqrt(x, seg_ref, n_real):
    ssq = jnp.dot((x * x).astype(BF16), seg_ref[...], preferred_element_type=F32)
    return lax.rsqrt(ssq * (1.0 / n_real) + EPS)


def _silu(g):
    return g * jax.nn.sigmoid(g)


def _rope_partner(x, half):
    lane = lax.broadcasted_iota(jnp.int32, (x.shape[0], LANES), 1)
    first = lane % (2 * half) < half
    cols = []
    for c in range(x.shape[1] // LANES):
        xc = x[:, LANES * c:LANES * (c + 1)]
        cols.append(jnp.where(first, pltpu.roll(xc, LANES - half, 1), pltpu.roll(xc, half, 1)))
    return cols[0] if len(cols) == 1 else jnp.concatenate(cols, axis=-1)


def _low_half(rows):
    return lax.broadcasted_iota(jnp.int32, (rows, LANES), 1) < LANES // 2


def _mla_kv_slots(nat):
    lo = _low_half(nat.shape[0])
    k_cols, v_cols = [], []
    for h in range(MLA_HEADS):
        col = nat[:, LANES * h:LANES * (h + 1)]
        k_cols.append(jnp.where(lo, col, 0.0))
        v_cols.append(jnp.where(lo, pltpu.roll(col, LANES // 2, 1), 0.0) if h % 2 == 0
                      else jnp.where(lo, 0.0, col))
    return jnp.concatenate(k_cols, axis=-1), _with_ones_lane(jnp.concatenate(v_cols, axis=-1))


def _win_kv_slots(x):
    lo = _low_half(x.shape[0])
    cols = []
    for c in range(x.shape[1] // LANES):
        xc = x[:, LANES * c:LANES * (c + 1)]
        xr = pltpu.roll(xc, LANES // 2, 1)
        cols += [jnp.where(lo, xc, 0.0), jnp.where(lo, 0.0, xr), jnp.where(lo, xr, 0.0), jnp.where(lo, 0.0, xc)]
    return jnp.concatenate(cols, axis=-1)


def _ones_lane(head):
    return MLA_V if head % 2 == 0 else 0


def _with_ones_lane(v):
    lane = lax.broadcasted_iota(jnp.int32, v.shape, 1)
    odd = (lane // LANES) % 2 == 1
    return jnp.where(lane % LANES == jnp.where(odd, _ones_lane(1), _ones_lane(0)), 1.0, v)


def _ctx_kv(ckv_ref, kr_ref, wkv_ref, seg_ref, kg_ref, k_out, v_out):
    kn, v = _mla_kv_slots(jnp.dot(ckv_ref[...].astype(BF16), wkv_ref[...], preferred_element_type=F32))
    v_out[...] = v.astype(BF16)
    kr = kr_ref[...]
    kg = kg_ref[...]
    kr2 = jnp.concatenate([kr, kr], axis=-1)
    kg2 = jnp.concatenate([kg, kg], axis=-1)
    for g in range(MLA_HEADS // 2):
        sl = slice(2 * LANES * g, 2 * LANES * (g + 1))
        kraw = kn[:, sl] + kr2
        k_out[:, sl] = (kraw * kg2 * _seg_rsqrt(kraw, seg_ref, MLA_QK)).astype(BF16)


def _prep_ab_kernel(rope, ctx_blocks, steps_per_batch, n_res, *refs):
    n_in = 17 if rope else 11
    n_ctx = 2 if ctx_blocks else 0
    main_in = refs[:n_in]
    ctx_in = refs[n_in:n_in + n_ctx]
    res_in = refs[n_in + n_ctx:n_in + n_ctx + n_res]
    outs = refs[n_in + n_ctx + n_res:]
    main_out, y_out = (outs[:-1], outs[-1]) if n_res else (outs, None)

    def own():
        x = _block_input(main_in[0], res_in, y_out)
        _prep_ab_body(rope, x, *main_in[1:], *main_out)

    if not ctx_blocks:
        own()
        return
    wkv_ref, seg_ref, kg_ref = main_in[7], main_in[8], main_in[11]
    is_ctx = pl.program_id(0) % steps_per_batch < ctx_blocks

    @pl.when(is_ctx)
    def _():
        _ctx_kv(*ctx_in, wkv_ref, seg_ref, kg_ref, main_out[1], main_out[2])

    pl.when(jnp.logical_not(is_ctx))(own)


def _prep_ab_body(rope, x, *refs):
    if rope:
        (mod_ref, ng_ref, win_ref, qng_ref, kvng_ref, wq_ref, wkv_ref, seg_ref,
         qg_ref, qgs_ref, kg_ref, kgs_ref, cm_ref, sm_ref, cr_ref, sr_ref,
         q_out, k_out, v_out, gate_out, rq_out, rk_out, rv_out) = refs
    else:
        (mod_ref, ng_ref, win_ref, qng_ref, kvng_ref, wq_ref, wkv_ref, seg_ref,
         qg_ref, kg_ref,
         q_out, k_out, v_out, gate_out, rq_out, rk_out, rv_out, ckv_out, kr_out) = refs
    o_kr, o_g, o_rq, o_rk, o_rv = AB_LAYOUT['kr'], AB_LAYOUT['g'], AB_LAYOUT['rq'], AB_LAYOUT['rk'], AB_LAYOUT['rv']

    h = _modulated(x, mod_ref, ng_ref)
    proj = jnp.dot(h, win_ref[...], preferred_element_type=F32)

    cqn = _rms(proj[:, 0:Q_LORA], qng_ref[...]).astype(BF16)
    ckvn_f = _rms(proj[:, Q_LORA:Q_LORA + KV_LORA], kvng_ref[...])
    ckvn = ckvn_f.astype(BF16)
    kr = proj[:, o_kr:o_kr + LANES]

    gate_out[...] = _silu(proj[:, o_g:o_g + 2 * MLA_WIDTH]).astype(BF16)
    rv_out[...] = proj[:, o_rv:o_rv + RET_WIDTH].astype(BF16)

    qm = jnp.dot(cqn, wq_ref[...], preferred_element_type=F32)
    kn, v = _mla_kv_slots(jnp.dot(ckvn, wkv_ref[...], preferred_element_type=F32))
    v_out[...] = v.astype(BF16)

    qscale = MLA_QK ** -0.5 * LOG2E
    if rope:
        krs = _rope_partner(kr, MLA_ROPE // 4)
        cm, sm = cm_ref[...], sm_ref[...]
        q_c = cm * qg_ref[...]
        q_s = sm * qgs_ref[...]
        kr_rot = kr * (cm * kg_ref[...]) + krs * (sm * kgs_ref[...])
        cr, sr = cr_ref[...], sr_ref[...]
        rq = proj[:, o_rq:o_rq + 256]
        rk = proj[:, o_rk:o_rk + 256]
        rqs = _rope_partner(rq, RET_DK // 4)
        rks = _rope_partner(rk, RET_DK // 4)
        for c in range(2):
            sl = slice(LANES * c, LANES * (c + 1))
            rq_out[:, sl] = ((rq[:, sl] * cr + rqs[:, sl] * sr) * (RET_DK ** -0.5)).astype(BF16)
            rk_out[:, sl] = (rk[:, sl] * cr + rks[:, sl] * sr).astype(BF16)
    else:
        q_c = jnp.broadcast_to(qg_ref[...], (qm.shape[0], LANES))
        kr_rot = kr * kg_ref[...]
        rq_out[...] = (proj[:, o_rq:o_rq + 256] * (RET_DK ** -0.5)).astype(BF16)
        rk_out[...] = proj[:, o_rk:o_rk + 256].astype(BF16)
        ckv_out[...] = ckvn_f
        kr_out[...] = kr

    kg = kg_ref[...]
    q_c2 = jnp.concatenate([q_c, q_c], axis=-1)
    kr2 = jnp.concatenate([kr, kr], axis=-1)
    kr_rot2 = jnp.concatenate([kr_rot, kr_rot], axis=-1)
    kg2 = jnp.concatenate([kg, kg], axis=-1)
    if rope:
        q_s2 = jnp.concatenate([q_s, q_s], axis=-1)
    for g in range(MLA_HEADS // 2):
        sl = slice(2 * LANES * g, 2 * LANES * (g + 1))
        qg = qm[:, sl]
        rq_n = _seg_rsqrt(qg, seg_ref, MLA_QK)
        qv = qg * q_c2
        if rope:
            qv = qv + _rope_partner(qg, MLA_ROPE // 4) * q_s2
        q_out[:, sl] = (qv * (rq_n * qscale)).astype(BF16)
        kgp = kn[:, sl]
        rk_n = _seg_rsqrt(kgp + kr2, seg_ref, MLA_QK)
        k_out[:, sl] = ((kgp * kg2 + kr_rot2) * rk_n).astype(BF16)


def _ab_weights(w_in, w_uq, w_ukv, q_head_g, k_head_g, rope):
    sw32 = _swap_idx(MLA_ROPE)
    z = lambda n: [-1] * n
    o = AB_OFF
    cols = list(range(o['cq'], o['cq'] + Q_LORA)) + list(range(o['ckv'], o['ckv'] + KV_LORA))
    cols += z(64) + list(range(o['krope'], o['krope'] + MLA_ROPE)) + z(32)
    cols += list(range(o['ga'], o['ga'] + MLA_WIDTH)) + list(range(o['gb'], o['gb'] + RET_WIDTH))
    cols += list(range(o['rq'], o['rq'] + 256)) + list(range(o['rk'], o['rk'] + 256))
    cols += list(range(o['rv'], o['rv'] + RET_WIDTH))
    assert len(cols) == AB_LAYOUT['rv'] + RET_WIDTH
    win = _take_cols(w_in, cols).astype(BF16)

    qc = []
    for hh in range(MLA_HEADS):
        qc += list(range(MLA_QK * hh, MLA_QK * (hh + 1))) + z(LANES - MLA_QK)
    wq = _take_cols(w_uq, qc).astype(BF16)
    wkv = w_ukv.astype(BF16)

    pad = lambda g: jnp.concatenate([g, jnp.zeros((LANES - MLA_QK,), F32)])[None, :]
    gsw = lambda g: jnp.concatenate([jnp.zeros((MLA_NOPE,), F32), g[MLA_NOPE + sw32],
                                     jnp.zeros((LANES - MLA_QK,), F32)])[None, :]
    qg, kg = pad(q_head_g), pad(k_head_g)
    qgs, kgs = (gsw(q_head_g), gsw(k_head_g)) if rope else (None, None)
    return win, wq, wkv, qg, qgs, kg, kgs


def _seg_matrix(width):
    idx = np.arange(2 * LANES) // width
    return jnp.asarray((idx[:, None] == idx[None, :]).astype(np.float32), BF16)


def _prep_ab(x, mod, norm_g, p, rope, tokens_per_batch, mod_row, tables, tm, ctx=None, res=None):
    T = x.shape[0]
    win, wq, wkv, qg, qgs, kg, kgs = _ab_weights(
        p['w_in'], p['w_uq'], p['w_ukv'], p['q_head_g'], p['k_head_g'], rope)
    seg = _seg_matrix(LANES)
    blocks_per_batch = max(tokens_per_batch // tm, 1)
    ctx_blocks = 0
    if ctx is not None:
        n_batch = T // tokens_per_batch
        ctx_rows = ctx[0].shape[0] // n_batch
        assert rope and ctx_rows % tm == 0 and tokens_per_batch % tm == 0
        ctx_blocks = ctx_rows // tm
    steps_per_batch = blocks_per_batch + ctx_blocks
    batch = lambda i: i // steps_per_batch
    in_batch = lambda i: jnp.maximum(i % steps_per_batch - ctx_blocks, 0)
    own = lambda i: batch(i) * blocks_per_batch + in_batch(i)
    tok = lambda w: pl.BlockSpec((tm, w), lambda i: (own(i), 0))
    if mod_row is None:
        mod_spec = pl.BlockSpec((1, 1, 3 * D_MODEL), lambda i: (batch(i), 0, 0))
    else:
        mod_spec = pl.BlockSpec((1, 1, 3 * D_MODEL), lambda i: (mod_row, 0, 0))
    vec = lambda a: (a, _const_spec(a.shape))
    ins = [(x, tok(D_MODEL)), (mod, mod_spec), vec(norm_g[None, :]), vec(win),
           vec(p['q_norm_g'][None, :]), vec(p['kv_norm_g'][None, :]), vec(wq),
           vec(wkv), vec(seg), vec(qg)]
    if rope:
        ins.append(vec(qgs))
    ins.append(vec(kg))
    if rope:
        ins.append(vec(kgs))
        tab = lambda a: (a, pl.BlockSpec((tm, LANES), lambda i: (in_batch(i), 0)))
        ins += [tab(tables['cm']), tab(tables['sm']), tab(tables['c64']), tab(tables['s64'])]
    kv_rows, kv_spec = T, tok(1024)
    if ctx_blocks:
        cspec = lambda w: pl.BlockSpec((tm, w), lambda i: (
            batch(i) * ctx_blocks + jnp.minimum(i % steps_per_batch, ctx_blocks - 1), 0))
        ins += [(ctx[0], cspec(KV_LORA)), (ctx[1], cspec(LANES))]
        kv_rows = T + ctx[0].shape[0]
        kv_spec = pl.BlockSpec((tm, 1024), lambda i: (i, 0))
    outs = [(T, 1024, BF16, tok(1024)), (kv_rows, 1024, BF16, kv_spec), (kv_rows, 1024, BF16, kv_spec),
            (T, 1024, BF16, tok(1024)), (T, 256, BF16, tok(256)), (T, 256, BF16, tok(256)),
            (T, 512, BF16, tok(512))]
    if not rope:
        outs += [(T, KV_LORA, F32, tok(KV_LORA)), (T, LANES, F32, tok(LANES))]
    n_res = 0
    if res is not None:
        ins.append((res[0], mod_spec))
        for a, w in res[1]:
            ins += [(a, tok(a.shape[1])), vec(w)]
        n_res = 1 + 2 * len(res[1])
        outs.append((T, D_MODEL, F32, tok(D_MODEL)))
    return pl.pallas_call(
        functools.partial(_prep_ab_kernel, rope, ctx_blocks, steps_per_batch, n_res),
        grid=(kv_rows // tm,),
        in_specs=[s for _, s in ins],
        out_specs=[o[3] for o in outs],
        out_shape=[jax.ShapeDtypeStruct(o[:2], o[2]) for o in outs],
        compiler_params=_params(1),
        name="prep_ab_rope" if rope else "prep_ab",
    )(*[a for a, _ in ins])


def _lane_groups(x, op):
    out = x[:, 0:LANES]
    for t in range(1, x.shape[1] // LANES):
        out = op(out, x[:, LANES * t:LANES * (t + 1)])
    return out


def _mla_attn_kernel(tk, q_ref, k_ref, v_ref, g_ref, o_ref, s_ref):
    dn = (((1,), (1,)), ((), ()))
    tq = q_ref.shape[1]
    nk = k_ref.shape[1] // tk
    heads = [(b, j) for b in range(q_ref.shape[0]) for j in range(q_ref.shape[2] // LANES)]
    m_prev, out = None, None
    for idx in range(len(heads) + 1):
        cur = heads[idx] if idx < len(heads) else None
        prev = heads[idx - 1] if idx > 0 else None
        if cur is not None:
            cb, cj = cur
            csl = slice(LANES * cj, LANES * (cj + 1))
            q = q_ref[cb, :, csl]
            mt = None
        if prev is not None:
            pb, pj = prev
            psl = slice(LANES * pj, LANES * (pj + 1))
            acc = jnp.zeros((tq, LANES), F32)
        for c in range(nk):
            rows = slice(c * tk, (c + 1) * tk)
            if cur is not None:
                s = lax.dot_general(q, k_ref[cb, rows, csl], dn, preferred_element_type=F32)
                s_ref[idx % 2, :, rows] = s
                smax = _lane_groups(s, jnp.maximum)
                mt = smax if mt is None else jnp.maximum(mt, smax)
            if prev is not None:
                p = jnp.exp2(s_ref[(idx - 1) % 2, :, rows] - m_prev)
                acc = acc + jnp.dot(p.astype(BF16), v_ref[pb, rows, psl], preferred_element_type=F32)
        if prev is not None:
            lane = lax.broadcasted_iota(jnp.int32, (tq, LANES), 1)
            l = jnp.sum(jnp.where(lane == _ones_lane(pj), acc, 0.0), axis=-1, keepdims=True)
            oj = jnp.where((lane >= MLA_V) == (pj % 2 == 1), acc, 0.0) / l
            out = oj if out is None else out + oj
            if pj % 2 == 1:
                osl = slice(LANES * (pj // 2), LANES * (pj // 2 + 1))
                o_ref[pb, :, osl] = (out * g_ref[pb, :, osl].astype(F32)).astype(BF16)
                out = None
        if cur is not None:
            m_prev = jnp.max(mt, axis=-1, keepdims=True)


def _mla_attn(q, k, v, gates, bb, tq, pairs):
    B, L, _ = q.shape
    Lk = k.shape[1]
    hp = MLA_HEADS // 2 // pairs
    qspec = pl.BlockSpec((bb, tq, 2 * LANES * pairs), lambda b, h, i: (b, i, h))
    kspec = pl.BlockSpec((bb, Lk, 2 * LANES * pairs), lambda b, h, i: (b, 0, h))
    ospec = pl.BlockSpec((bb, tq, LANES * pairs), lambda b, h, i: (b, i, h))
    tk = min(Lk, MLA_KEY_CHUNK)
    assert Lk % tk == 0
    return pl.pallas_call(
        functools.partial(_mla_attn_kernel, tk),
        grid=(B // bb, hp, L // tq),
        in_specs=[qspec, kspec, kspec, ospec],
        out_specs=ospec,
        out_shape=jax.ShapeDtypeStruct((B, L, MLA_WIDTH), BF16),
        scratch_shapes=[pltpu.VMEM((2, tq, Lk), F32)],
        compiler_params=_params(3),
        name="mla_attn",
    )(q, k, v, gates)


def _ret_kernel(n_chunks, decay_ref, q_ref, k_ref, v_ref, sf_ref, sb_ref, gn_ref, gate_ref,
                o_ref, rf_ref, rb_ref, acc_ref, st_ref, dmask_ref, lane_ref, sdec_ref):
    C = RET_CHUNK
    pair = pl.program_id(0)
    bb = q_ref.shape[0]
    lane_hi = lax.broadcasted_iota(jnp.int32, (C, LANES), 1) >= RET_DK

    @pl.when(pl.program_id(1) == 0)
    def _():
        row = lax.broadcasted_iota(jnp.int32, (C, C), 0).astype(F32)
        rel = row - lax.broadcasted_iota(jnp.int32, (C, C), 1).astype(F32)
        lrow = lax.broadcasted_iota(jnp.int32, (C, LANES), 0).astype(F32)
        srow_hi = lax.broadcasted_iota(jnp.int32, (LANES, 2 * RET_DV), 0) >= RET_DK
        scol_hi = lax.broadcasted_iota(jnp.int32, (LANES, 2 * RET_DV), 1) >= RET_DV
        same_head = srow_hi == scol_hi
        sdec_ref[2] = jnp.where(same_head, 1.0, 0.0)
        for d in range(2):
            log_decay = lambda shape, hh: -jnp.exp(jnp.full(shape, decay_ref[d, 2 * pair + hh], F32))
            for hh in range(2):
                lg = log_decay((C, C), hh)
                if d == 0:
                    dmask_ref[2 * d + hh] = jnp.where(rel >= 0, jnp.exp(jnp.maximum(rel, 0.0) * lg), 0.0)
                else:
                    dmask_ref[2 * d + hh] = jnp.where(rel <= 0, jnp.exp(jnp.maximum(-rel, 0.0) * lg), 0.0)
            lg_lane = jnp.where(lane_hi, log_decay((C, LANES), 1), log_decay((C, LANES), 0))
            if d == 0:
                lane_ref[0] = jnp.exp((C - 1.0 - lrow) * lg_lane)
                lane_ref[2] = jnp.exp((lrow + 1.0) * lg_lane)
            else:
                lane_ref[1] = jnp.exp(lrow * lg_lane)
                lane_ref[3] = jnp.exp((C - lrow) * lg_lane)
            lg_row = jnp.where(srow_hi, log_decay((LANES, 2 * RET_DV), 1), log_decay((LANES, 2 * RET_DV), 0))
            sdec_ref[d] = jnp.where(same_head, jnp.exp(C * lg_row), 0.0)

    for b in range(bb):
        for d, s_ref in ((0, sf_ref), (1, sb_ref)):
            st_ref[b, d] = jnp.zeros((LANES, 2 * RET_DV), F32)
            for hh in range(2):
                st_ref[b, d, RET_DK * hh:RET_DK * (hh + 1), RET_DV * hh:RET_DV * (hh + 1)] = s_ref[b, hh]
    acc_ref[...] = jnp.zeros_like(acc_ref)
    dn = (((1,), (1,)), ((), ()))

    def step(n, _):
        for b in range(bb):
            for d in range(2):
                cidx = n if d == 0 else n_chunks - 1 - n
                r0 = pl.multiple_of(cidx * C, C)
                qc = q_ref[b, pl.ds(r0, C), :].astype(F32)
                kc = k_ref[b, pl.ds(r0, C), :]
                vc = v_ref[b, pl.ds(r0, C), :]
                q2 = jnp.concatenate([jnp.where(lane_hi, 0.0, qc), jnp.where(lane_hi, qc, 0.0)], axis=0)
                s2 = lax.dot_general(q2.astype(BF16), kc, dn, preferred_element_type=F32)
                intra = [jnp.dot((s2[C * hh:C * (hh + 1)] * dmask_ref[2 * d + hh]).astype(BF16),
                                 vc[:, RET_DV * hh:RET_DV * (hh + 1)], preferred_element_type=F32)
                         for hh in range(2)]
                state = st_ref[b, d]
                inter = jnp.dot((qc * lane_ref[2 + d]).astype(BF16), state.astype(BF16),
                                preferred_element_type=F32)
                u = jnp.dot((kc.astype(F32) * lane_ref[d]).T.astype(BF16), vc, preferred_element_type=F32)
                st_ref[b, d] = sdec_ref[d] * state + u * sdec_ref[2]
                acc_ref[b, pl.ds(r0, C), :] += jnp.concatenate(intra, axis=-1) + inter
        return 0

    lax.fori_loop(0, n_chunks, step, 0, unroll=min(2, n_chunks))

    for b in range(bb):
        for hh in range(2):
            rf_ref[b, hh] = st_ref[b, 0, RET_DK * hh:RET_DK * (hh + 1), RET_DV * hh:RET_DV * (hh + 1)]
            rb_ref[b, hh] = st_ref[b, 1, RET_DK * hh:RET_DK * (hh + 1), RET_DV * hh:RET_DV * (hh + 1)]

    ones = jnp.ones((RET_DV, RET_DV), BF16)

    def norm(n, _):
        r0 = pl.multiple_of(n * C, C)
        for b in range(bb):
            for hh in range(2):
                sl = slice(RET_DV * hh, RET_DV * (hh + 1))
                y = acc_ref[b, pl.ds(r0, C), sl]
                yc = y - jnp.mean(y, axis=-1, keepdims=True)
                var = jnp.dot((yc * yc).astype(BF16), ones, preferred_element_type=F32) * (1.0 / RET_DV)
                out = yc * lax.rsqrt(var + EPS) * gn_ref[:, sl]
                o_ref[b, pl.ds(r0, C), sl] = (out * gate_ref[b, pl.ds(r0, C), sl].astype(F32)).astype(BF16)
        return 0

    lax.fori_loop(0, n_chunks, norm, 0, unroll=min(2, n_chunks))


def _retention(rq, rk, rv, sf, sb, decay, ret_norm_g, gates, bb):
    B, L, _ = rq.shape
    C = RET_CHUNK
    hp = RET_HEADS // 2
    st_spec = pl.BlockSpec((bb, 2, RET_DK, RET_DV), lambda h, b: (b, h, 0, 0))
    st_shape = jax.ShapeDtypeStruct((B, RET_HEADS, RET_DK, RET_DV), F32)
    qk_spec = pl.BlockSpec((bb, L, LANES), lambda h, b: (b, 0, h))
    v_spec = pl.BlockSpec((bb, L, 2 * RET_DV), lambda h, b: (b, 0, h))
    return pl.pallas_call(
        functools.partial(_ret_kernel, L // C),
        grid=(hp, B // bb),
        in_specs=[pl.BlockSpec(memory_space=pltpu.SMEM), qk_spec, qk_spec, v_spec, st_spec, st_spec,
                  pl.BlockSpec((1, 2 * RET_DV), lambda h, b: (0, h)),
                  pl.BlockSpec((bb, L, 2 * RET_DV), lambda h, b: (b, 0, hp + h))],
        out_specs=[v_spec, st_spec, st_spec],
        out_shape=[jax.ShapeDtypeStruct((B, L, RET_WIDTH), BF16), st_shape, st_shape],
        scratch_shapes=[pltpu.VMEM((bb, L, 2 * RET_DV), F32),
                        pltpu.VMEM((bb, 2, LANES, 2 * RET_DV), F32),
                        pltpu.VMEM((4, C, C), F32),
                        pltpu.VMEM((4, C, LANES), F32),
                        pltpu.VMEM((3, LANES, 2 * RET_DV), F32)],
        compiler_params=_params(2),
        name="retention",
    )(decay, rq, rk, rv, sf, sb, ret_norm_g[None, :], gates)


def _prep_win_kernel(rope, n_res, *refs):
    n_in = 11 if rope else 7
    res_in = refs[n_in:n_in + n_res]
    outs = refs[n_in + n_res:]
    main_out, y_out = (outs[:-1], outs[-1]) if n_res else (outs, None)
    if rope:
        x_ref, mod_ref, ng_ref, win_ref, seg_ref, qg_ref, qgs_ref, kg_ref, kgs_ref, c_ref, s_ref = refs[:n_in]
        q_out, k_out, v_out, gate_out = main_out
    else:
        x_ref, mod_ref, ng_ref, win_ref, seg_ref, qg_ref, kg_ref = refs[:n_in]
        q_out, k_out, v_out, gate_out, kst_out, vst_out = main_out
    o = WIN_OFF
    h = _modulated(_block_input(x_ref, res_in, y_out), mod_ref, ng_ref)
    proj = jnp.dot(h, win_ref[...], preferred_element_type=F32)
    gate_out[...] = _silu(proj[:, o['g']:o['g'] + WIN_WIDTH]).astype(BF16)

    qg2 = jnp.concatenate([qg_ref[...]] * 2, axis=-1)
    kg2 = jnp.concatenate([kg_ref[...]] * 2, axis=-1)
    if rope:
        c2 = jnp.concatenate([c_ref[...]] * 2, axis=-1)
        s2 = jnp.concatenate([s_ref[...]] * 2, axis=-1)
        q_c, q_s = c2 * qg2, s2 * jnp.concatenate([qgs_ref[...]] * 2, axis=-1)
        k_c, k_s = c2 * kg2, s2 * jnp.concatenate([kgs_ref[...]] * 2, axis=-1)
    qscale = WIN_HEAD_DIM ** -0.5 * LOG2E
    for g in range(WIN_WIDTH // 256):
        sl = slice(256 * g, 256 * (g + 1))
        qg = proj[:, sl]
        rn = _seg_rsqrt(qg, seg_ref, WIN_HEAD_DIM) * qscale
        if rope:
            qv = qg * q_c + _rope_partner(qg, WIN_HEAD_DIM // 4) * q_s
        else:
            qv = qg * qg2
        q_out[:, sl] = (qv * rn).astype(BF16)
    kraw = proj[:, o['k']:o['k'] + 256]
    rn = _seg_rsqrt(kraw, seg_ref, WIN_HEAD_DIM)
    if rope:
        kn = (kraw * k_c + _rope_partner(kraw, WIN_HEAD_DIM // 4) * k_s) * rn
    else:
        kn = kraw * kg2 * rn
    v = proj[:, o['v']:o['v'] + 256]
    if not rope:
        kst_out[...] = kn
        vst_out[...] = v
    k_out[...] = _win_kv_slots(kn).astype(BF16)
    v_out[...] = _with_ones_lane(_win_kv_slots(v)).astype(BF16)


def _prep_win(x, mod, norm_g, p, rope, tokens_per_batch, mod_row, tables, tm, res=None):
    T = x.shape[0]
    sw64 = _swap_idx(WIN_HEAD_DIM)
    win = p['w_in'].astype(BF16)
    seg = _seg_matrix(WIN_HEAD_DIM)
    rep = lambda g: jnp.concatenate([g, g])[None, :]
    blocks_per_batch = tokens_per_batch // tm
    tok = lambda w: pl.BlockSpec((tm, w), lambda i: (i, 0))
    if mod_row is None:
        mod_spec = pl.BlockSpec((1, 1, 3 * D_MODEL), lambda i: (i // blocks_per_batch, 0, 0))
    else:
        mod_spec = pl.BlockSpec((1, 1, 3 * D_MODEL), lambda i: (mod_row, 0, 0))
    vec = lambda a: (a, _const_spec(a.shape))
    ins = [(x, tok(D_MODEL)), (mod, mod_spec), vec(norm_g[None, :]), vec(win), vec(seg),
           vec(rep(p['q_head_g']))]
    if rope:
        ins.append(vec(rep(p['q_head_g'][sw64])))
    ins.append(vec(rep(p['k_head_g'])))
    if rope:
        ins.append(vec(rep(p['k_head_g'][sw64])))
        tab = lambda a: (a, pl.BlockSpec((tm, LANES), lambda i: (i % blocks_per_batch, 0)))
        ins += [tab(tables['c64']), tab(tables['s64'])]
    outs = [(1024, BF16), (1024, BF16), (1024, BF16), (1024, BF16)]
    if not rope:
        outs += [(256, F32), (256, F32)]
    n_res = 0
    if res is not None:
        ins.append((res[0], mod_spec))
        for a, w in res[1]:
            ins += [(a, tok(a.shape[1])), vec(w)]
        n_res = 1 + 2 * len(res[1])
        outs.append((D_MODEL, F32))
    return pl.pallas_call(
        functools.partial(_prep_win_kernel, rope, n_res),
        grid=(T // tm,),
        in_specs=[s for _, s in ins],
        out_specs=[tok(w) for w, _ in outs],
        out_shape=[jax.ShapeDtypeStruct((T, w), dt) for w, dt in outs],
        compiler_params=_params(1),
        name="prep_win_rope" if rope else "prep_win",
    )(*[a for a, _ in ins])


def _win_attn_kernel(local, *refs):
    if local:
        sink_ref, q_ref, kf_ref, vf_ref, k_ref, v_ref, g_ref, bias_ref, o_ref, s_ref = refs
    else:
        sink_ref, q_ref, kf_ref, vf_ref, g_ref, o_ref, s_ref = refs
    tq = q_ref.shape[1]
    nf = kf_ref.shape[1]
    kv_per_step = q_ref.shape[2] // (2 * LANES)
    upper = lax.broadcasted_iota(jnp.int32, (2 * tq, 1), 0) >= tq
    lane = lax.broadcasted_iota(jnp.int32, (2 * tq, LANES), 1)
    if local:
        span = bias_ref.shape[2]
        start = jnp.clip(pl.program_id(2) * tq - WINDOW, 0, k_ref.shape[1] - span)
        start = pl.multiple_of(start, WINDOW)
    dn = (((1,), (1,)), ((), ()))
    units = [(b, jj, half) for b in range(q_ref.shape[0]) for jj in range(kv_per_step) for half in range(2)]
    m_prev, sk_prev, out = None, None, None
    for idx in range(len(units) + 1):
        cur = units[idx] if idx < len(units) else None
        prev = units[idx - 1] if idx > 0 else None
        if cur is not None:
            cb, cjj, chalf = cur
            base = 2 * LANES * cjj
            csl = slice(base + LANES * chalf, base + LANES * (chalf + 1))
            q2 = jnp.concatenate([q_ref[cb, :, base:base + LANES],
                                  q_ref[cb, :, base + LANES:base + 2 * LANES]], axis=0)
            head = 4 * (pl.program_id(1) * kv_per_step + cjj) + chalf
            sk = jnp.where(upper, sink_ref[head + 2], sink_ref[head]) * LOG2E
            s_ctx = lax.dot_general(q2, kf_ref[cb, :, csl], dn, preferred_element_type=F32)
            s_ref[idx % 2, :, 0:nf] = s_ctx
            mt = _lane_groups(s_ctx, jnp.maximum)
        if prev is not None:
            pb, pjj, phalf = prev
            pbase = 2 * LANES * pjj
            psl = slice(pbase + LANES * phalf, pbase + LANES * (phalf + 1))
            e_ctx = jnp.exp2(s_ref[(idx - 1) % 2, :, 0:nf] - m_prev)
            acc = jnp.dot(e_ctx.astype(BF16), vf_ref[pb, :, psl], preferred_element_type=F32)
        if local and cur is not None:
            s_loc = lax.dot_general(q2, k_ref[cb, pl.ds(start, span), csl], dn,
                                    preferred_element_type=F32) + bias_ref[0]
            s_ref[idx % 2, :, nf:nf + span] = s_loc
            mt = jnp.maximum(mt, _lane_groups(s_loc, jnp.maximum))
        if local and prev is not None:
            e_loc = jnp.exp2(s_ref[(idx - 1) % 2, :, nf:nf + span] - m_prev)
            acc = acc + jnp.dot(e_loc.astype(BF16), v_ref[pb, pl.ds(start, span), psl],
                                preferred_element_type=F32)
        if prev is not None:
            den = jnp.sum(jnp.where(lane == _ones_lane(phalf), acc, 0.0), axis=-1, keepdims=True)
            den = den + jnp.exp2(sk_prev - m_prev)
            oh = jnp.where((lane >= WIN_HEAD_DIM) == (phalf == 1), acc, 0.0) / den
            out = oh if out is None else out + oh
            if phalf == 1:
                gate = g_ref[pb, :, pbase:pbase + 2 * LANES].astype(F32)
                o_ref[pb, :, pbase:pbase + LANES] = (out[0:tq] * gate[:, 0:LANES]).astype(BF16)
                o_ref[pb, :, pbase + LANES:pbase + 2 * LANES] = (
                    out[tq:2 * tq] * gate[:, LANES:2 * LANES]).astype(BF16)
                out = None
        if cur is not None:
            m_prev = jnp.maximum(jnp.max(mt, axis=-1, keepdims=True), sk)
            sk_prev = sk


def _window_bias(tq):
    span = tq + 2 * WINDOW
    qi = jnp.arange(2 * tq)[:, None] % tq
    kk = jnp.arange(span)[None, :]
    cases = [jnp.abs(kk - off - qi) <= WINDOW for off in (0, WINDOW, 2 * WINDOW)]
    return jnp.where(jnp.stack(cases), 0.0, NEG).astype(F32)


def _win_attn(q, kf, vf, own, gates, sink, bb, tq, kv_per_step):
    B, L, _ = q.shape
    nb = L // tq
    width = 2 * LANES * kv_per_step
    qspec = pl.BlockSpec((bb, tq, width), lambda b, h, i: (b, i, h))
    fspec = pl.BlockSpec((bb, kf.shape[1], width), lambda b, h, i: (b, 0, h))
    in_specs = [pl.BlockSpec(memory_space=pltpu.SMEM), qspec, fspec, fspec]
    args = [sink, q, kf, vf]
    n_keys = kf.shape[1]
    if own is not None:
        assert nb >= 3 and bb == 1
        kspec = pl.BlockSpec((bb, L, width), lambda b, h, i: (b, 0, h), pipeline_mode=pl.Buffered(1))
        in_specs += [kspec, kspec]
        args += list(own)
    in_specs.append(qspec)
    args.append(gates)
    if own is not None:
        bias = _window_bias(tq)
        n_keys += bias.shape[2]
        in_specs.append(pl.BlockSpec(
            (1,) + bias.shape[1:],
            lambda b, h, i: (jnp.where(i == 0, 0, jnp.where(i == nb - 1, 2, 1)), 0, 0)))
        args.append(bias)
    return pl.pallas_call(
        functools.partial(_win_attn_kernel, own is not None),
        grid=(B // bb, WIN_KV_HEADS // kv_per_step, nb),
        in_specs=in_specs,
        out_specs=qspec,
        out_shape=jax.ShapeDtypeStruct((B, L, WIN_WIDTH), BF16),
        scratch_shapes=[pltpu.VMEM((2, 2 * tq, n_keys), F32)],
        compiler_params=_params(3),
        name="win_attn_local" if own is not None else "win_attn",
    )(*args)


def _out_kernel(n_parts, *refs):
    x_ref, mod_ref = refs[0], refs[1]
    parts = refs[2:2 + 2 * n_parts]
    o_ref = refs[2 + 2 * n_parts]
    y = None
    for i in range(n_parts):
        t = jnp.dot(parts[2 * i][...], parts[2 * i + 1][...], preferred_element_type=F32)
        y = t if y is None else y + t
    gate = mod_ref[0, :, 2 * D_MODEL:3 * D_MODEL]
    o_ref[...] = x_ref[...] + gate * y


def _out_proj(x, mod, parts, tokens_per_batch, mod_row, tm):
    T = x.shape[0]
    blocks_per_batch = tokens_per_batch // tm
    tok = lambda w: pl.BlockSpec((tm, w), lambda i: (i, 0))
    if mod_row is None:
        mod_spec = pl.BlockSpec((1, 1, 3 * D_MODEL), lambda i: (i // blocks_per_batch, 0, 0))
    else:
        mod_spec = pl.BlockSpec((1, 1, 3 * D_MODEL), lambda i: (mod_row, 0, 0))
    in_specs = [tok(D_MODEL), mod_spec]
    args = [x, mod]
    for a, w in parts:
        in_specs += [tok(a.shape[1]), _const_spec(w.shape)]
        args += [a, w]
    return pl.pallas_call(
        functools.partial(_out_kernel, len(parts)),
        grid=(T // tm,),
        in_specs=in_specs,
        out_specs=tok(D_MODEL),
        out_shape=jax.ShapeDtypeStruct((T, D_MODEL), F32),
        compiler_params=_params(1),
        name="out_proj",
    )(*args)


def kernel(x_prompt, x_sample, cache_l0_mla_ckv, cache_l0_mla_krope, state_l0_ret_fwd, state_l0_ret_bwd, cache_l1_win_k, cache_l1_win_v, cache_l2_mla_ckv, cache_l2_mla_krope, state_l2_ret_fwd, state_l2_ret_bwd, cache_l3_win_k, cache_l3_win_v, c, c_ctx, ada_w, ada_b, norm_g, ab_w_in, mla_q_norm_g, mla_w_uq, mla_kv_norm_g, mla_w_ukv, mla_q_head_g, mla_k_head_g, ret_decay, ret_norm_g, ab_w_out, win_w_in, win_q_head_g, win_k_head_g, win_sink, win_w_out):
    BP, LP, D = x_prompt.shape
    BS, LS, _ = x_sample.shape
    P = cache_l0_mla_ckv.shape[1]
    ctx_caches = ((cache_l0_mla_ckv, cache_l0_mla_krope, state_l0_ret_fwd, state_l0_ret_bwd),
                  (cache_l1_win_k, cache_l1_win_v),
                  (cache_l2_mla_ckv, cache_l2_mla_krope, state_l2_ret_fwd, state_l2_ret_bwd),
                  (cache_l3_win_k, cache_l3_win_v))

    cond = jnp.concatenate([c, c_ctx[None, :], jnp.zeros((8 - BS - 1, D), F32)], axis=0)
    mod_all = _ada_mod(cond, ada_w, ada_b).reshape(DEPTH, 8, 1, 3 * D)
    ctx_row = BS

    c32, s32 = _rope_tables(LS, MLA_ROPE)
    c64, s64 = _rope_tables(LS, RET_DK)
    ones = jnp.ones((LS, MLA_NOPE), F32)
    zeros = jnp.zeros((LS, MLA_NOPE), F32)
    z32 = jnp.zeros((LS, LANES - MLA_QK), F32)
    tables = dict(cm=jnp.concatenate([ones, c32, z32], -1), sm=jnp.concatenate([zeros, s32, z32], -1),
                  c64=jnp.concatenate([c64, c64], -1), s64=jnp.concatenate([s64, s64], -1))

    y_p = x_prompt.reshape(BP * LP, D)
    y_s = x_sample.reshape(BS * LS, D)
    tm = 512
    new_state = []
    pend_p = pend_s = None

    def take_stream(outs, pending, y):
        return (outs[:-1], outs[-1]) if pending is not None else (outs, y)

    for l in range(DEPTH):
        i = l // 2
        mod = mod_all[l]
        p3 = lambda a: a.reshape(BP, LP, a.shape[-1])
        s3 = lambda a: a.reshape(BS, -1, a.shape[-1])
        flat = lambda a: a.reshape(-1, a.shape[-1])
        if l % 2 == 0:
            p = {'w_in': ab_w_in[i], 'q_norm_g': mla_q_norm_g[i], 'w_uq': mla_w_uq[i],
                 'kv_norm_g': mla_kv_norm_g[i], 'w_ukv': mla_w_ukv[i], 'q_head_g': mla_q_head_g[i],
                 'k_head_g': mla_k_head_g[i]}
            w_out = ab_w_out[i].astype(BF16)
            parts_w = (w_out[:MLA_WIDTH], w_out[MLA_WIDTH:])
            ckv_c, krope_c, sf, sb = ctx_caches[l]

            outs, y_p = take_stream(_prep_ab(y_p, mod, norm_g[l], p, False, LP, ctx_row, None, tm, res=pend_p),
                                    pend_p, y_p)
            q, k, v, gates, rq, rk, rv, ckv, kr = outs
            a_p = _mla_attn(p3(q), p3(k), p3(v), p3(gates), PROMPT_ROWS_PER_STEP, LP, 2)
            zst = jnp.zeros((BP, RET_HEADS, RET_DK, RET_DV), F32)
            r_p, rf, rb = _retention(p3(rq), p3(rk), p3(rv), zst, zst, ret_decay[i], ret_norm_g[i], p3(gates),
                                     PROMPT_ROWS_PER_STEP)
            new_state.append((ckv.reshape(BP, LP, KV_LORA),
                              kr[:, MLA_NOPE:MLA_QK].reshape(BP, LP, MLA_ROPE), rf, rb))
            pend_p = (mod, [(flat(a_p), parts_w[0]), (flat(r_p), parts_w[1])])

            kr_c = jnp.pad(krope_c.reshape(BS * P, MLA_ROPE), ((0, 0), (MLA_NOPE, LANES - MLA_QK)))
            outs, y_s = take_stream(_prep_ab(y_s, mod, norm_g[l], p, True, LS, None, tables, tm,
                                             (ckv_c.reshape(BS * P, KV_LORA), kr_c), res=pend_s), pend_s, y_s)
            q, k, v, gates, rq, rk, rv = outs
            a_s = _mla_attn(s3(q), s3(k), s3(v), s3(gates), 1, 512, 2)
            r_s, _, _ = _retention(s3(rq), s3(rk), s3(rv), sf, sb, ret_decay[i], ret_norm_g[i], s3(gates), 1)
            pend_s = (mod, [(flat(a_s), parts_w[0]), (flat(r_s), parts_w[1])])
        else:
            p = {'w_in': win_w_in[i], 'q_head_g': win_q_head_g[i], 'k_head_g': win_k_head_g[i]}
            w_out = win_w_out[i].astype(BF16)
            sink = win_sink[i]
            kc, vc = ctx_caches[l]

            outs, y_p = take_stream(_prep_win(y_p, mod, norm_g[l], p, False, LP, ctx_row, None, tm, res=pend_p),
                                    pend_p, y_p)
            q, k, v, gates, kst, vst = outs
            o_p = _win_attn(p3(q), p3(k), p3(v), None, p3(gates), sink, PROMPT_ROWS_PER_STEP, LP, 2)
            new_state.append((kst.reshape(BP, LP, WIN_KV_HEADS, WIN_HEAD_DIM),
                              vst.reshape(BP, LP, WIN_KV_HEADS, WIN_HEAD_DIM)))
            pend_p = (mod, [(flat(o_p), w_out)])

            outs, y_s = take_stream(_prep_win(y_s, mod, norm_g[l], p, True, LS, None, tables, tm, res=pend_s),
                                    pend_s, y_s)
            q, k, v, gates = outs

            def slots(a, fill):
                a = a.astype(BF16)
                f = jnp.broadcast_to(fill.astype(BF16), a.shape)
                return jnp.concatenate([a, f, f, a], axis=-1).reshape(BS, P, 4 * 2 * LANES)

            zero = jnp.zeros((WIN_HEAD_DIM,), F32)
            o_s = _win_attn(s3(q), slots(kc, zero), slots(vc, zero.at[0].set(1.0)), (s3(k), s3(v)), s3(gates),
                            sink, 1, 256, 4)
            pend_s = (mod, [(flat(o_s), w_out)])

    y_p = _out_proj(y_p, pend_p[0], pend_p[1], LP, ctx_row, tm)
    y_s = _out_proj(y_s, pend_s[0], pend_s[1], LS, None, tm)

    (l0_ckv, l0_krope, l0_rf, l0_rb), (l1_k, l1_v), (l2_ckv, l2_krope, l2_rf, l2_rb), (l3_k, l3_v) = new_state
    return (y_p.reshape(BP, LP, D), y_s.reshape(BS, LS, D), l0_ckv, l0_krope, l0_rf, l0_rb, l1_k, l1_v,
            l2_ckv, l2_krope, l2_rf, l2_rb, l3_k, l3_v)
```

```python
import functools

import numpy as np
import jax
import jax.numpy as jnp
from jax import lax
from jax.experimental import pallas as pl
from jax.experimental.pallas import tpu as pltpu

D_MODEL = 1024
DEPTH = 4
GRID_W = 64
ROPE_BASE = 10000.0
EPS = 1e-6

MLA_HEADS = 8
MLA_NOPE = 64
MLA_ROPE = 32
MLA_QK = MLA_NOPE + MLA_ROPE
MLA_V = 64
Q_LORA = 384
KV_LORA = 256
MLA_WIDTH = MLA_HEADS * MLA_V

RET_HEADS = 4
RET_DK = 64
RET_DV = 128
RET_CHUNK = 256
RET_WIDTH = RET_HEADS * RET_DV

WIN_HEADS = 16
WIN_KV_HEADS = 4
WIN_HEAD_DIM = 64
WINDOW = 128
WIN_WIDTH = WIN_HEADS * WIN_HEAD_DIM

LANES = 128
F32 = jnp.float32
BF16 = jnp.bfloat16
NEG = -1e30
LOG2E = 1.4426950408889634
VMEM_LIMIT = 52 * 1024 * 1024
PROMPT_ROWS_PER_STEP = 8
MLA_KEY_CHUNK = 512

AB_OFF = dict(cq=0, ckv=384, krope=640, ga=672, rq=1184, rk=1440, rv=1696, gb=2208)
AB_LAYOUT = dict(kr=640, g=768, rq=1792, rk=2048, rv=2304)
WIN_OFF = dict(q=0, k=1024, v=1280, g=1536)


def _params(n_axes):
    return pltpu.CompilerParams(dimension_semantics=("arbitrary",) * n_axes, vmem_limit_bytes=VMEM_LIMIT)


def _const_spec(shape):
    nd = len(shape)
    return pl.BlockSpec(shape, lambda *_: (0,) * nd)


def _swap_idx(dim):
    d2 = dim // 2
    half = d2 // 2
    one = np.concatenate([np.arange(half, d2), np.arange(0, half)])
    return np.concatenate([one, d2 + one])


def _take_cols(w, cols):
    w = w.astype(BF16)
    pieces, i = [], 0
    while i < len(cols):
        j = i + 1
        if cols[i] < 0:
            while j < len(cols) and cols[j] < 0:
                j += 1
            pieces.append(jnp.zeros((w.shape[0], j - i), BF16))
        else:
            while j < len(cols) and cols[j] == cols[j - 1] + 1:
                j += 1
            pieces.append(w[:, cols[i]:cols[i] + j - i])
        i = j
    return jnp.concatenate(pieces, axis=1)


def _rope_tables(n_tokens, dim):
    n_rows = n_tokens // GRID_W
    half = dim // 4
    inv = jnp.power(jnp.float32(ROPE_BASE), -jnp.arange(half, dtype=F32) / half)

    def one(n_pos):
        ang = jnp.arange(n_pos, dtype=F32)[:, None] * inv[None, :]
        c, s = jnp.cos(ang), jnp.sin(ang)
        return jnp.concatenate([c, c], -1), jnp.concatenate([-s, s], -1)

    cr, sr = [jnp.repeat(a, GRID_W, axis=0) for a in one(n_rows)]
    cc, sc = [jnp.tile(a, (n_rows, 1)) for a in one(GRID_W)]
    return jnp.concatenate([cr, cc], -1), jnp.concatenate([sr, sc], -1)


def _ada_kernel(cond_ref, w_ref, b_ref, o_ref):
    c = cond_ref[...]
    sc = (c * jax.nn.sigmoid(c)).astype(BF16)
    o_ref[0] = jnp.dot(sc, w_ref[0].astype(BF16), preferred_element_type=F32) + b_ref[0]


def _ada_mod(cond, ada_w, ada_b):
    tn = 768
    n3 = 3 * D_MODEL
    return pl.pallas_call(
        _ada_kernel,
        grid=(DEPTH, n3 // tn),
        in_specs=[pl.BlockSpec((8, D_MODEL), lambda l, j: (0, 0)),
                  pl.BlockSpec((1, D_MODEL, tn), lambda l, j: (l, 0, j)),
                  pl.BlockSpec((1, 1, tn), lambda l, j: (l, 0, j))],
        out_specs=pl.BlockSpec((1, 8, tn), lambda l, j: (l, 0, j)),
        out_shape=jax.ShapeDtypeStruct((DEPTH, 8, n3), F32),
        compiler_params=_params(2),
        name="ada_mod",
    )(cond, ada_w, ada_b.reshape(DEPTH, 1, n3))


def _rms(x, g):
    return x * lax.rsqrt(jnp.mean(x * x, axis=-1, keepdims=True) + EPS) * g


def _modulated(x, mod_ref, ng_ref):
    shift = mod_ref[0, :, 0:D_MODEL]
    scale = mod_ref[0, :, D_MODEL:2 * D_MODEL]
    return (_rms(x, ng_ref[...]) * (1.0 + scale) + shift).astype(BF16)


def _block_input(x_ref, res_refs, y_out):
    x = x_ref[...]
    if not res_refs:
        return x
    upd = None
    for a_ref, w_ref in zip(res_refs[1::2], res_refs[2::2]):
        t = jnp.dot(a_ref[...], w_ref[...], preferred_element_type=F32)
        upd = t if upd is None else upd + t
    x = x + res_refs[0][0, :, 2 * D_MODEL:3 * D_MODEL] * upd
    y_out[...] = x
    return x


def _seg_rsqrt(x, seg_ref, n_real):
    ssq = jnp.dot((x * x).astype(BF16), seg_ref[...], preferred_element_type=F32)
    return lax.rsqrt(ssq * (1.0 / n_real) + EPS)


def _silu(g):
    return g * jax.nn.sigmoid(g)


def _rope_partner(x, half):
    lane = lax.broadcasted_iota(jnp.int32, (x.shape[0], LANES), 1)
    first = lane % (2 * half) < half
    cols = []
    for c in range(x.shape[1] // LANES):
        xc = x[:, LANES * c:LANES * (c + 1)]
        cols.append(jnp.where(first, pltpu.roll(xc, LANES - half, 1), pltpu.roll(xc, half, 1)))
    return cols[0] if len(cols) == 1 else jnp.concatenate(cols, axis=-1)


def _low_half(rows):
    return lax.broadcasted_iota(jnp.int32, (rows, LANES), 1) < LANES // 2


def _mla_kv_slots(nat):
    lo = _low_half(nat.shape[0])
    k_cols, v_cols = [], []
    for h in range(MLA_HEADS):
        col = nat[:, LANES * h:LANES * (h + 1)]
        k_cols.append(jnp.where(lo, col, 0.0))
        v_cols.append(jnp.where(lo, pltpu.roll(col, LANES // 2, 1), 0.0) if h % 2 == 0
                      else jnp.where(lo, 0.0, col))
    return jnp.concatenate(k_cols, axis=-1), _with_ones_lane(jnp.concatenate(v_cols, axis=-1))


def _win_kv_slots(x):
    lo = _low_half(x.shape[0])
    cols = []
    for c in range(x.shape[1] // LANES):
        xc = x[:, LANES * c:LANES * (c + 1)]
        xr = pltpu.roll(xc, LANES // 2, 1)
        cols += [jnp.where(lo, xc, 0.0), jnp.where(lo, 0.0, xr), jnp.where(lo, xr, 0.0), jnp.where(lo, 0.0, xc)]
    return jnp.concatenate(cols, axis=-1)


def _ones_lane(head):
    return MLA_V if head % 2 == 0 else 0


def _with_ones_lane(v):
    lane = lax.broadcasted_iota(jnp.int32, v.shape, 1)
    odd = (lane // LANES) % 2 == 1
    return jnp.where(lane % LANES == jnp.where(odd, _ones_lane(1), _ones_lane(0)), 1.0, v)


def _ctx_kv(ckv_ref, kr_ref, wkv_ref, seg_ref, kg_ref, k_out, v_out):
    kn, v = _mla_kv_slots(jnp.dot(ckv_ref[...].astype(BF16), wkv_ref[...], preferred_element_type=F32))
    v_out[...] = v.astype(BF16)
    kr = kr_ref[...]
    kg = kg_ref[...]
    kr2 = jnp.concatenate([kr, kr], axis=-1)
    kg2 = jnp.concatenate([kg, kg], axis=-1)
    for g in range(MLA_HEADS // 2):
        sl = slice(2 * LANES * g, 2 * LANES * (g + 1))
        kraw = kn[:, sl] + kr2
        k_out[:, sl] = (kraw * kg2 * _seg_rsqrt(kraw, seg_ref, MLA_QK)).astype(BF16)


def _prep_ab_kernel(rope, ctx_blocks, steps_per_batch, n_res, *refs):
    n_in = 17 if rope else 11
    n_ctx = 2 if ctx_blocks else 0
    main_in = refs[:n_in]
    ctx_in = refs[n_in:n_in + n_ctx]
    res_in = refs[n_in + n_ctx:n_in + n_ctx + n_res]
    outs = refs[n_in + n_ctx + n_res:]
    main_out, y_out = (outs[:-1], outs[-1]) if n_res else (outs, None)

    def own():
        x = _block_input(main_in[0], res_in, y_out)
        _prep_ab_body(rope, x, *main_in[1:], *main_out)

    if not ctx_blocks:
        own()
        return
    wkv_ref, seg_ref, kg_ref = main_in[7], main_in[8], main_in[11]
    is_ctx = pl.program_id(0) % steps_per_batch < ctx_blocks

    @pl.when(is_ctx)
    def _():
        _ctx_kv(*ctx_in, wkv_ref, seg_ref, kg_ref, main_out[1], main_out[2])

    pl.when(jnp.logical_not(is_ctx))(own)


def _prep_ab_body(rope, x, *refs):
    if rope:
        (mod_ref, ng_ref, win_ref, qng_ref, kvng_ref, wq_ref, wkv_ref, seg_ref,
         qg_ref, qgs_ref, kg_ref, kgs_ref, cm_ref, sm_ref, cr_ref, sr_ref,
         q_out, k_out, v_out, gate_out, rq_out, rk_out, rv_out) = refs
    else:
        (mod_ref, ng_ref, win_ref, qng_ref, kvng_ref, wq_ref, wkv_ref, seg_ref,
         qg_ref, kg_ref,
         q_out, k_out, v_out, gate_out, rq_out, rk_out, rv_out, ckv_out, kr_out) = refs
    o_kr, o_g, o_rq, o_rk, o_rv = AB_LAYOUT['kr'], AB_LAYOUT['g'], AB_LAYOUT['rq'], AB_LAYOUT['rk'], AB_LAYOUT['rv']

    h = _modulated(x, mod_ref, ng_ref)
    proj = jnp.dot(h, win_ref[...], preferred_element_type=F32)

    cqn = _rms(proj[:, 0:Q_LORA], qng_ref[...]).astype(BF16)
    ckvn_f = _rms(proj[:, Q_LORA:Q_LORA + KV_LORA], kvng_ref[...])
    ckvn = ckvn_f.astype(BF16)
    kr = proj[:, o_kr:o_kr + LANES]

    gate_out[...] = _silu(proj[:, o_g:o_g + 2 * MLA_WIDTH]).astype(BF16)
    rv_out[...] = proj[:, o_rv:o_rv + RET_WIDTH].astype(BF16)

    qm = jnp.dot(cqn, wq_ref[...], preferred_element_type=F32)
    kn, v = _mla_kv_slots(jnp.dot(ckvn, wkv_ref[...], preferred_element_type=F32))
    v_out[...] = v.astype(BF16)

    qscale = MLA_QK ** -0.5 * LOG2E
    if rope:
        krs = _rope_partner(kr, MLA_ROPE // 4)
        cm, sm = cm_ref[...], sm_ref[...]
        q_c = cm * qg_ref[...]
        q_s = sm * qgs_ref[...]
        kr_rot = kr * (cm * kg_ref[...]) + krs * (sm * kgs_ref[...])
        cr, sr = cr_ref[...], sr_ref[...]
        rq = proj[:, o_rq:o_rq + 256]
        rk = proj[:, o_rk:o_rk + 256]
        rqs = _rope_partner(rq, RET_DK // 4)
        rks = _rope_partner(rk, RET_DK // 4)
        for c in range(2):
            sl = slice(LANES * c, LANES * (c + 1))
            rq_out[:, sl] = ((rq[:, sl] * cr + rqs[:, sl] * sr) * (RET_DK ** -0.5)).astype(BF16)
            rk_out[:, sl] = (rk[:, sl] * cr + rks[:, sl] * sr).astype(BF16)
    else:
        q_c = jnp.broadcast_to(qg_ref[...], (qm.shape[0], LANES))
        kr_rot = kr * kg_ref[...]
        rq_out[...] = (proj[:, o_rq:o_rq + 256] * (RET_DK ** -0.5)).astype(BF16)
        rk_out[...] = proj[:, o_rk:o_rk + 256].astype(BF16)
        ckv_out[...] = ckvn_f
        kr_out[...] = kr

    kg = kg_ref[...]
    q_c2 = jnp.concatenate([q_c, q_c], axis=-1)
    kr2 = jnp.concatenate([kr, kr], axis=-1)
    kr_rot2 = jnp.concatenate([kr_rot, kr_rot], axis=-1)
    kg2 = jnp.concatenate([kg, kg], axis=-1)
    if rope:
        q_s2 = jnp.concatenate([q_s, q_s], axis=-1)
    for g in range(MLA_HEADS // 2):
        sl = slice(2 * LANES * g, 2 * LANES * (g + 1))
        qg = qm[:, sl]
        rq_n = _seg_rsqrt(qg, seg_ref, MLA_QK)
        qv = qg * q_c2
        if rope:
            qv = qv + _rope_partner(qg, MLA_ROPE // 4) * q_s2
        q_out[:, sl] = (qv * (rq_n * qscale)).astype(BF16)
        kgp = kn[:, sl]
        rk_n = _seg_rsqrt(kgp + kr2, seg_ref, MLA_QK)
        k_out[:, sl] = ((kgp * kg2 + kr_rot2) * rk_n).astype(BF16)


def _ab_weights(w_in, w_uq, w_ukv, q_head_g, k_head_g, rope):
    sw32 = _swap_idx(MLA_ROPE)
    z = lambda n: [-1] * n
    o = AB_OFF
    cols = list(range(o['cq'], o['cq'] + Q_LORA)) + list(range(o['ckv'], o['ckv'] + KV_LORA))
    cols += z(64) + list(range(o['krope'], o['krope'] + MLA_ROPE)) + z(32)
    cols += list(range(o['ga'], o['ga'] + MLA_WIDTH)) + list(range(o['gb'], o['gb'] + RET_WIDTH))
    cols += list(range(o['rq'], o['rq'] + 256)) + list(range(o['rk'], o['rk'] + 256))
    cols += list(range(o['rv'], o['rv'] + RET_WIDTH))
    assert len(cols) == AB_LAYOUT['rv'] + RET_WIDTH
    win = _take_cols(w_in, cols).astype(BF16)

    qc = []
    for hh in range(MLA_HEADS):
        qc += list(range(MLA_QK * hh, MLA_QK * (hh + 1))) + z(LANES - MLA_QK)
    wq = _take_cols(w_uq, qc).astype(BF16)
    wkv = w_ukv.astype(BF16)

    pad = lambda g: jnp.concatenate([g, jnp.zeros((LANES - MLA_QK,), F32)])[None, :]
    gsw = lambda g: jnp.concatenate([jnp.zeros((MLA_NOPE,), F32), g[MLA_NOPE + sw32],
                                     jnp.zeros((LANES - MLA_QK,), F32)])[None, :]
    qg, kg = pad(q_head_g), pad(k_head_g)
    qgs, kgs = (gsw(q_head_g), gsw(k_head_g)) if rope else (None, None)
    return win, wq, wkv, qg, qgs, kg, kgs


def _seg_matrix(width):
    idx = np.arange(2 * LANES) // width
    return jnp.asarray((idx[:, None] == idx[None, :]).astype(np.float32), BF16)


def _prep_ab(x, mod, norm_g, p, rope, tokens_per_batch, mod_row, tables, tm, ctx=None, res=None):
    T = x.shape[0]
    win, wq, wkv, qg, qgs, kg, kgs = _ab_weights(
        p['w_in'], p['w_uq'], p['w_ukv'], p['q_head_g'], p['k_head_g'], rope)
    seg = _seg_matrix(LANES)
    blocks_per_batch = max(tokens_per_batch // tm, 1)
    ctx_blocks = 0
    if ctx is not None:
        n_batch = T // tokens_per_batch
        ctx_rows = ctx[0].shape[0] // n_batch
        assert rope and ctx_rows % tm == 0 and tokens_per_batch % tm == 0
        ctx_blocks = ctx_rows // tm
    steps_per_batch = blocks_per_batch + ctx_blocks
    batch = lambda i: i // steps_per_batch
    in_batch = lambda i: jnp.maximum(i % steps_per_batch - ctx_blocks, 0)
    own = lambda i: batch(i) * blocks_per_batch + in_batch(i)
    tok = lambda w: pl.BlockSpec((tm, w), lambda i: (own(i), 0))
    if mod_row is None:
        mod_spec = pl.BlockSpec((1, 1, 3 * D_MODEL), lambda i: (batch(i), 0, 0))
    else:
        mod_spec = pl.BlockSpec((1, 1, 3 * D_MODEL), lambda i: (mod_row, 0, 0))
    vec = lambda a: (a, _const_spec(a.shape))
    ins = [(x, tok(D_MODEL)), (mod, mod_spec), vec(norm_g[None, :]), vec(win),
           vec(p['q_norm_g'][None, :]), vec(p['kv_norm_g'][None, :]), vec(wq),
           vec(wkv), vec(seg), vec(qg)]
    if rope:
        ins.append(vec(qgs))
    ins.append(vec(kg))
    if rope:
        ins.append(vec(kgs))
        tab = lambda a: (a, pl.BlockSpec((tm, LANES), lambda i: (in_batch(i), 0)))
        ins += [tab(tables['cm']), tab(tables['sm']), tab(tables['c64']), tab(tables['s64'])]
    kv_rows, kv_spec = T, tok(1024)
    if ctx_blocks:
        cspec = lambda w: pl.BlockSpec((tm, w), lambda i: (
            batch(i) * ctx_blocks + jnp.minimum(i % steps_per_batch, ctx_blocks - 1), 0))
        ins += [(ctx[0], cspec(KV_LORA)), (ctx[1], cspec(LANES))]
        kv_rows = T + ctx[0].shape[0]
        kv_spec = pl.BlockSpec((tm, 1024), lambda i: (i, 0))
    outs = [(T, 1024, BF16, tok(1024)), (kv_rows, 1024, BF16, kv_spec), (kv_rows, 1024, BF16, kv_spec),
            (T, 1024, BF16, tok(1024)), (T, 256, BF16, tok(256)), (T, 256, BF16, tok(256)),
            (T, 512, BF16, tok(512))]
    if not rope:
        outs += [(T, KV_LORA, F32, tok(KV_LORA)), (T, LANES, F32, tok(LANES))]
    n_res = 0
    if res is not None:
        ins.append((res[0], mod_spec))
        for a, w in res[1]:
            ins += [(a, tok(a.shape[1])), vec(w)]
        n_res = 1 + 2 * len(res[1])
        outs.append((T, D_MODEL, F32, tok(D_MODEL)))
    return pl.pallas_call(
        functools.partial(_prep_ab_kernel, rope, ctx_blocks, steps_per_batch, n_res),
        grid=(kv_rows // tm,),
        in_specs=[s for _, s in ins],
        out_specs=[o[3] for o in outs],
        out_shape=[jax.ShapeDtypeStruct(o[:2], o[2]) for o in outs],
        compiler_params=_params(1),
        name="prep_ab_rope" if rope else "prep_ab",
    )(*[a for a, _ in ins])


def _lane_groups(x, op):
    out = x[:, 0:LANES]
    for t in range(1, x.shape[1] // LANES):
        out = op(out, x[:, LANES * t:LANES * (t + 1)])
    return out


def _mla_attn_kernel(tk, q_ref, k_ref, v_ref, g_ref, o_ref, s_ref):
    dn = (((1,), (1,)), ((), ()))
    tq = q_ref.shape[1]
    nk = k_ref.shape[1] // tk
    heads = [(b, j) for b in range(q_ref.shape[0]) for j in range(q_ref.shape[2] // LANES)]
    m_prev, out = None, None
    for idx in range(len(heads) + 1):
        cur = heads[idx] if idx < len(heads) else None
        prev = heads[idx - 1] if idx > 0 else None
        if cur is not None:
            cb, cj = cur
            csl = slice(LANES * cj, LANES * (cj + 1))
            q = q_ref[cb, :, csl]
            mt = None
        if prev is not None:
            pb, pj = prev
            psl = slice(LANES * pj, LANES * (pj + 1))
            acc = jnp.zeros((tq, LANES), F32)
        for c in range(nk):
            rows = slice(c * tk, (c + 1) * tk)
            if cur is not None:
                s = lax.dot_general(q, k_ref[cb, rows, csl], dn, preferred_element_type=F32)
                s_ref[idx % 2, :, rows] = s
                smax = _lane_groups(s, jnp.maximum)
                mt = smax if mt is None else jnp.maximum(mt, smax)
            if prev is not None:
                p = jnp.exp2(s_ref[(idx - 1) % 2, :, rows] - m_prev)
                acc = acc + jnp.dot(p.astype(BF16), v_ref[pb, rows, psl], preferred_element_type=F32)
        if prev is not None:
            lane = lax.broadcasted_iota(jnp.int32, (tq, LANES), 1)
            l = jnp.sum(jnp.where(lane == _ones_lane(pj), acc, 0.0), axis=-1, keepdims=True)
            oj = jnp.where((lane >= MLA_V) == (pj % 2 == 1), acc, 0.0) / l
            out = oj if out is None else out + oj
            if pj % 2 == 1:
                osl = slice(LANES * (pj // 2), LANES * (pj // 2 + 1))
                o_ref[pb, :, osl] = (out * g_ref[pb, :, osl].astype(F32)).astype(BF16)
                out = None
        if cur is not None:
            m_prev = jnp.max(mt, axis=-1, keepdims=True)


def _mla_attn(q, k, v, gates, bb, tq, pairs):
    B, L, _ = q.shape
    Lk = k.shape[1]
    hp = MLA_HEADS // 2 // pairs
    qspec = pl.BlockSpec((bb, tq, 2 * LANES * pairs), lambda b, h, i: (b, i, h))
    kspec = pl.BlockSpec((bb, Lk, 2 * LANES * pairs), lambda b, h, i: (b, 0, h))
    ospec = pl.BlockSpec((bb, tq, LANES * pairs), lambda b, h, i: (b, i, h))
    tk = min(Lk, MLA_KEY_CHUNK)
    assert Lk % tk == 0
    return pl.pallas_call(
        functools.partial(_mla_attn_kernel, tk),
        grid=(B // bb, hp, L // tq),
        in_specs=[qspec, kspec, kspec, ospec],
        out_specs=ospec,
        out_shape=jax.ShapeDtypeStruct((B, L, MLA_WIDTH), BF16),
        scratch_shapes=[pltpu.VMEM((2, tq, Lk), F32)],
        compiler_params=_params(3),
        name="mla_attn",
    )(q, k, v, gates)


def _ret_kernel(n_chunks, decay_ref, q_ref, k_ref, v_ref, sf_ref, sb_ref, gn_ref, gate_ref,
                o_ref, rf_ref, rb_ref, st_ref, u_ref, r_ref, dmask_ref, qw_ref, kwt_ref, sdec_ref):
    C = RET_CHUNK
    pair = pl.program_id(0)
    bb = q_ref.shape[0]
    lane_hi = lax.broadcasted_iota(jnp.int32, (C, LANES), 1) >= RET_DK

    @pl.when(pl.program_id(1) == 0)
    def _():
        rel = (lax.broadcasted_iota(jnp.int32, (C, C), 0) - lax.broadcasted_iota(jnp.int32, (C, C), 1)).astype(F32)
        qrow = lax.broadcasted_iota(jnp.int32, (C, LANES), 0).astype(F32)
        kcol = lax.broadcasted_iota(jnp.int32, (LANES, C), 1).astype(F32)
        krow_hi = lax.broadcasted_iota(jnp.int32, (LANES, C), 0) >= RET_DK
        srow_hi = lax.broadcasted_iota(jnp.int32, (LANES, 2 * RET_DV), 0) >= RET_DK
        scol_hi = lax.broadcasted_iota(jnp.int32, (LANES, 2 * RET_DV), 1) >= RET_DV
        same_head = srow_hi == scol_hi
        sdec_ref[2] = jnp.where(same_head, 1.0, 0.0)
        log_decay = lambda shape, d, hh: -jnp.exp(jnp.full(shape, decay_ref[d, 2 * pair + hh], F32))
        for hh in range(2):
            fwd = jnp.where(rel >= 0, jnp.exp(jnp.maximum(rel, 0.0) * log_decay((C, C), 0, hh)), 0.0)
            bwd = jnp.where(rel <= 0, jnp.exp(jnp.maximum(-rel, 0.0) * log_decay((C, C), 1, hh)), 0.0)
            dmask_ref[hh] = fwd + bwd
        for d in range(2):
            lg_q = jnp.where(lane_hi, log_decay((C, LANES), d, 1), log_decay((C, LANES), d, 0))
            lg_k = jnp.where(krow_hi, log_decay((LANES, C), d, 1), log_decay((LANES, C), d, 0))
            lg_s = jnp.where(srow_hi, log_decay((LANES, 2 * RET_DV), d, 1), log_decay((LANES, 2 * RET_DV), d, 0))
            if d == 0:
                qw_ref[d] = jnp.exp((qrow + 1.0) * lg_q)
                kwt_ref[d] = jnp.exp((C - 1.0 - kcol) * lg_k)
            else:
                qw_ref[d] = jnp.exp((C - qrow) * lg_q)
                kwt_ref[d] = jnp.exp(kcol * lg_k)
            sdec_ref[d] = jnp.where(same_head, jnp.exp(C * lg_s), 0.0)

    def updates(n):
        r0 = n * C
        for b in range(bb):
            kt = k_ref[b, pl.ds(r0, C), :].astype(F32).T
            vc = v_ref[b, pl.ds(r0, C), :]
            for d in range(2):
                u_ref[b, n, d] = jnp.dot((kt * kwt_ref[d]).astype(BF16), vc, preferred_element_type=F32)

    for n in range(n_chunks):
        updates(n)

    for b in range(bb):
        for d, s_ref in ((0, sf_ref), (1, sb_ref)):
            st_ref[b, d] = jnp.zeros((LANES, 2 * RET_DV), F32)
            for hh in range(2):
                st_ref[b, d, RET_DK * hh:RET_DK * (hh + 1), RET_DV * hh:RET_DV * (hh + 1)] = s_ref[b, hh]

    def scan(n, _):
        for b in range(bb):
            for d in range(2):
                c = n if d == 0 else n_chunks - 1 - n
                state = st_ref[b, d]
                r_ref[b, c, LANES * d:LANES * (d + 1), :] = state.astype(BF16)
                st_ref[b, d] = sdec_ref[d] * state + u_ref[b, c, d] * sdec_ref[2]
        return 0

    lax.fori_loop(0, n_chunks, scan, 0)

    for b in range(bb):
        for hh in range(2):
            rf_ref[b, hh] = st_ref[b, 0, RET_DK * hh:RET_DK * (hh + 1), RET_DV * hh:RET_DV * (hh + 1)]
            rb_ref[b, hh] = st_ref[b, 1, RET_DK * hh:RET_DK * (hh + 1), RET_DV * hh:RET_DV * (hh + 1)]

    dn = (((1,), (1,)), ((), ()))
    ones = jnp.ones((RET_DV, RET_DV), BF16)

    def outputs(n):
        r0 = n * C
        for b in range(bb):
            qc = q_ref[b, pl.ds(r0, C), :].astype(F32)
            kc = k_ref[b, pl.ds(r0, C), :]
            vc = v_ref[b, pl.ds(r0, C), :]
            q2 = jnp.concatenate([jnp.where(lane_hi, 0.0, qc), jnp.where(lane_hi, qc, 0.0)], axis=0)
            s2 = lax.dot_general(q2.astype(BF16), kc, dn, preferred_element_type=F32)
            qw = jnp.concatenate([qc * qw_ref[0], qc * qw_ref[1]], axis=-1).astype(BF16)
            inter = jnp.dot(qw, r_ref[b, n], preferred_element_type=F32)
            for hh in range(2):
                sl = slice(RET_DV * hh, RET_DV * (hh + 1))
                y = jnp.dot((s2[C * hh:C * (hh + 1)] * dmask_ref[hh]).astype(BF16), vc[:, sl],
                            preferred_element_type=F32) + inter[:, sl]
                yc = y - jnp.mean(y, axis=-1, keepdims=True)
                var = jnp.dot((yc * yc).astype(BF16), ones, preferred_element_type=F32) * (1.0 / RET_DV)
                out = yc * lax.rsqrt(var + EPS) * gn_ref[:, sl]
                o_ref[b, pl.ds(r0, C), sl] = (out * gate_ref[b, pl.ds(r0, C), sl].astype(F32)).astype(BF16)

    for n in range(n_chunks):
        outputs(n)


def _retention(rq, rk, rv, sf, sb, decay, ret_norm_g, gates, bb):
    B, L, _ = rq.shape
    C = RET_CHUNK
    n_chunks = L // C
    hp = RET_HEADS // 2
    st_spec = pl.BlockSpec((bb, 2, RET_DK, RET_DV), lambda h, b: (b, h, 0, 0))
    st_shape = jax.ShapeDtypeStruct((B, RET_HEADS, RET_DK, RET_DV), F32)
    qk_spec = pl.BlockSpec((bb, L, LANES), lambda h, b: (b, 0, h))
    v_spec = pl.BlockSpec((bb, L, 2 * RET_DV), lambda h, b: (b, 0, h))
    return pl.pallas_call(
        functools.partial(_ret_kernel, n_chunks),
        grid=(hp, B // bb),
        in_specs=[pl.BlockSpec(memory_space=pltpu.SMEM), qk_spec, qk_spec, v_spec, st_spec, st_spec,
                  pl.BlockSpec((1, 2 * RET_DV), lambda h, b: (0, h)),
                  pl.BlockSpec((bb, L, 2 * RET_DV), lambda h, b: (b, 0, hp + h))],
        out_specs=[v_spec, st_spec, st_spec],
        out_shape=[jax.ShapeDtypeStruct((B, L, RET_WIDTH), BF16), st_shape, st_shape],
        scratch_shapes=[pltpu.VMEM((bb, 2, LANES, 2 * RET_DV), F32),
                        pltpu.VMEM((bb, n_chunks, 2, LANES, 2 * RET_DV), F32),
                        pltpu.VMEM((bb, n_chunks, 2 * LANES, 2 * RET_DV), BF16),
                        pltpu.VMEM((2, C, C), F32),
                        pltpu.VMEM((2, C, LANES), F32),
                        pltpu.VMEM((2, LANES, C), F32),
                        pltpu.VMEM((3, LANES, 2 * RET_DV), F32)],
        compiler_params=_params(2),
        name="retention",
    )(decay, rq, rk, rv, sf, sb, ret_norm_g[None, :], gates)


def _prep_win_kernel(rope, n_res, *refs):
    n_in = 11 if rope else 7
    res_in = refs[n_in:n_in + n_res]
    outs = refs[n_in + n_res:]
    main_out, y_out = (outs[:-1], outs[-1]) if n_res else (outs, None)
    if rope:
        x_ref, mod_ref, ng_ref, win_ref, seg_ref, qg_ref, qgs_ref, kg_ref, kgs_ref, c_ref, s_ref = refs[:n_in]
        q_out, k_out, v_out, gate_out = main_out
    else:
        x_ref, mod_ref, ng_ref, win_ref, seg_ref, qg_ref, kg_ref = refs[:n_in]
        q_out, k_out, v_out, gate_out, kst_out, vst_out = main_out
    o = WIN_OFF
    h = _modulated(_block_input(x_ref, res_in, y_out), mod_ref, ng_ref)
    proj = jnp.dot(h, win_ref[...], preferred_element_type=F32)
    gate_out[...] = _silu(proj[:, o['g']:o['g'] + WIN_WIDTH]).astype(BF16)

    qg2 = jnp.concatenate([qg_ref[...]] * 2, axis=-1)
    kg2 = jnp.concatenate([kg_ref[...]] * 2, axis=-1)
    if rope:
        c2 = jnp.concatenate([c_ref[...]] * 2, axis=-1)
        s2 = jnp.concatenate([s_ref[...]] * 2, axis=-1)
        q_c, q_s = c2 * qg2, s2 * jnp.concatenate([qgs_ref[...]] * 2, axis=-1)
        k_c, k_s = c2 * kg2, s2 * jnp.concatenate([kgs_ref[...]] * 2, axis=-1)
    qscale = WIN_HEAD_DIM ** -0.5 * LOG2E
    for g in range(WIN_WIDTH // 256):
        sl = slice(256 * g, 256 * (g + 1))
        qg = proj[:, sl]
        rn = _seg_rsqrt(qg, seg_ref, WIN_HEAD_DIM) * qscale
        if rope:
            qv = qg * q_c + _rope_partner(qg, WIN_HEAD_DIM // 4) * q_s
        else:
            qv = qg * qg2
        q_out[:, sl] = (qv * rn).astype(BF16)
    kraw = proj[:, o['k']:o['k'] + 256]
    rn = _seg_rsqrt(kraw, seg_ref, WIN_HEAD_DIM)
    if rope:
        kn = (kraw * k_c + _rope_partner(kraw, WIN_HEAD_DIM // 4) * k_s) * rn
    else:
        kn = kraw * kg2 * rn
    v = proj[:, o['v']:o['v'] + 256]
    if not rope:
        kst_out[...] = kn
        vst_out[...] = v
    k_out[...] = _win_kv_slots(kn).astype(BF16)
    v_out[...] = _with_ones_lane(_win_kv_slots(v)).astype(BF16)


def _prep_win(x, mod, norm_g, p, rope, tokens_per_batch, mod_row, tables, tm, res=None):
    T = x.shape[0]
    sw64 = _swap_idx(WIN_HEAD_DIM)
    win = p['w_in'].astype(BF16)
    seg = _seg_matrix(WIN_HEAD_DIM)
    rep = lambda g: jnp.concatenate([g, g])[None, :]
    blocks_per_batch = tokens_per_batch // tm
    tok = lambda w: pl.BlockSpec((tm, w), lambda i: (i, 0))
    if mod_row is None:
        mod_spec = pl.BlockSpec((1, 1, 3 * D_MODEL), lambda i: (i // blocks_per_batch, 0, 0))
    else:
        mod_spec = pl.BlockSpec((1, 1, 3 * D_MODEL), lambda i: (mod_row, 0, 0))
    vec = lambda a: (a, _const_spec(a.shape))
    ins = [(x, tok(D_MODEL)), (mod, mod_spec), vec(norm_g[None, :]), vec(win), vec(seg),
           vec(rep(p['q_head_g']))]
    if rope:
        ins.append(vec(rep(p['q_head_g'][sw64])))
    ins.append(vec(rep(p['k_head_g'])))
    if rope:
        ins.append(vec(rep(p['k_head_g'][sw64])))
        tab = lambda a: (a, pl.BlockSpec((tm, LANES), lambda i: (i % blocks_per_batch, 0)))
        ins += [tab(tables['c64']), tab(tables['s64'])]
    outs = [(1024, BF16), (1024, BF16), (1024, BF16), (1024, BF16)]
    if not rope:
        outs += [(256, F32), (256, F32)]
    n_res = 0
    if res is not None:
        ins.append((res[0], mod_spec))
        for a, w in res[1]:
            ins += [(a, tok(a.shape[1])), vec(w)]
        n_res = 1 + 2 * len(res[1])
        outs.append((D_MODEL, F32))
    return pl.pallas_call(
        functools.partial(_prep_win_kernel, rope, n_res),
        grid=(T // tm,),
        in_specs=[s for _, s in ins],
        out_specs=[tok(w) for w, _ in outs],
        out_shape=[jax.ShapeDtypeStruct((T, w), dt) for w, dt in outs],
        compiler_params=_params(1),
        name="prep_win_rope" if rope else "prep_win",
    )(*[a for a, _ in ins])


def _win_attn_kernel(local, *refs):
    if local:
        sink_ref, q_ref, kf_ref, vf_ref, k_ref, v_ref, g_ref, bias_ref, o_ref, s_ref = refs
    else:
        sink_ref, q_ref, kf_ref, vf_ref, g_ref, o_ref, s_ref = refs
    tq = q_ref.shape[1]
    nf = kf_ref.shape[1]
    kv_per_step = q_ref.shape[2] // (2 * LANES)
    upper = lax.broadcasted_iota(jnp.int32, (2 * tq, 1), 0) >= tq
    lane = lax.broadcasted_iota(jnp.int32, (2 * tq, LANES), 1)
    if local:
        span = bias_ref.shape[2]
        start = jnp.clip(pl.program_id(2) * tq - WINDOW, 0, k_ref.shape[1] - span)
        start = pl.multiple_of(start, WINDOW)
    dn = (((1,), (1,)), ((), ()))
    units = [(b, jj, half) for b in range(q_ref.shape[0]) for jj in range(kv_per_step) for half in range(2)]
    m_prev, sk_prev, out = None, None, None
    for idx in range(len(units) + 1):
        cur = units[idx] if idx < len(units) else None
        prev = units[idx - 1] if idx > 0 else None
        if cur is not None:
            cb, cjj, chalf = cur
            base = 2 * LANES * cjj
            csl = slice(base + LANES * chalf, base + LANES * (chalf + 1))
            q2 = jnp.concatenate([q_ref[cb, :, base:base + LANES],
                                  q_ref[cb, :, base + LANES:base + 2 * LANES]], axis=0)
            head = 4 * (pl.program_id(1) * kv_per_step + cjj) + chalf
            sk = jnp.where(upper, sink_ref[head + 2], sink_ref[head]) * LOG2E
            s_ctx = lax.dot_general(q2, kf_ref[cb, :, csl], dn, preferred_element_type=F32)
            s_ref[idx % 2, :, 0:nf] = s_ctx
            mt = _lane_groups(s_ctx, jnp.maximum)
        if prev is not None:
            pb, pjj, phalf = prev
            pbase = 2 * LANES * pjj
            psl = slice(pbase + LANES * phalf, pbase + LANES * (phalf + 1))
            e_ctx = jnp.exp2(s_ref[(idx - 1) % 2, :, 0:nf] - m_prev)
            acc = jnp.dot(e_ctx.astype(BF16), vf_ref[pb, :, psl], preferred_element_type=F32)
        if local and cur is not None:
            s_loc = lax.dot_general(q2, k_ref[cb, pl.ds(start, span), csl], dn,
                                    preferred_element_type=F32) + bias_ref[0]
            s_ref[idx % 2, :, nf:nf + span] = s_loc
            mt = jnp.maximum(mt, _lane_groups(s_loc, jnp.maximum))
        if local and prev is not None:
            e_loc = jnp.exp2(s_ref[(idx - 1) % 2, :, nf:nf + span] - m_prev)
            acc = acc + jnp.dot(e_loc.astype(BF16), v_ref[pb, pl.ds(start, span), psl],
                                preferred_element_type=F32)
        if prev is not None:
            den = jnp.sum(jnp.where(lane == _ones_lane(phalf), acc, 0.0), axis=-1, keepdims=True)
            den = den + jnp.exp2(sk_prev - m_prev)
            oh = jnp.where((lane >= WIN_HEAD_DIM) == (phalf == 1), acc, 0.0) / den
            out = oh if out is None else out + oh
            if phalf == 1:
                gate = g_ref[pb, :, pbase:pbase + 2 * LANES].astype(F32)
                o_ref[pb, :, pbase:pbase + LANES] = (out[0:tq] * gate[:, 0:LANES]).astype(BF16)
                o_ref[pb, :, pbase + LANES:pbase + 2 * LANES] = (
                    out[tq:2 * tq] * gate[:, LANES:2 * LANES]).astype(BF16)
                out = None
        if cur is not None:
            m_prev = jnp.maximum(jnp.max(mt, axis=-1, keepdims=True), sk)
            sk_prev = sk


def _window_bias(tq):
    span = tq + 2 * WINDOW
    qi = jnp.arange(2 * tq)[:, None] % tq
    kk = jnp.arange(span)[None, :]
    cases = [jnp.abs(kk - off - qi) <= WINDOW for off in (0, WINDOW, 2 * WINDOW)]
    return jnp.where(jnp.stack(cases), 0.0, NEG).astype(F32)


def _win_attn(q, kf, vf, own, gates, sink, bb, tq, kv_per_step):
    B, L, _ = q.shape
    nb = L // tq
    width = 2 * LANES * kv_per_step
    qspec = pl.BlockSpec((bb, tq, width), lambda b, h, i: (b, i, h))
    fspec = pl.BlockSpec((bb, kf.shape[1], width), lambda b, h, i: (b, 0, h))
    in_specs = [pl.BlockSpec(memory_space=pltpu.SMEM), qspec, fspec, fspec]
    args = [sink, q, kf, vf]
    n_keys = kf.shape[1]
    if own is not None:
        assert nb >= 3 and bb == 1
        kspec = pl.BlockSpec((bb, L, width), lambda b, h, i: (b, 0, h), pipeline_mode=pl.Buffered(1))
        in_specs += [kspec, kspec]
        args += list(own)
    in_specs.append(qspec)
    args.append(gates)
    if own is not None:
        bias = _window_bias(tq)
        n_keys += bias.shape[2]
        in_specs.append(pl.BlockSpec(
            (1,) + bias.shape[1:],
            lambda b, h, i: (jnp.where(i == 0, 0, jnp.where(i == nb - 1, 2, 1)), 0, 0)))
        args.append(bias)
    return pl.pallas_call(
        functools.partial(_win_attn_kernel, own is not None),
        grid=(B // bb, WIN_KV_HEADS // kv_per_step, nb),
        in_specs=in_specs,
        out_specs=qspec,
        out_shape=jax.ShapeDtypeStruct((B, L, WIN_WIDTH), BF16),
        scratch_shapes=[pltpu.VMEM((2, 2 * tq, n_keys), F32)],
        compiler_params=_params(3),
        name="win_attn_local" if own is not None else "win_attn",
    )(*args)


def _out_kernel(n_parts, *refs):
    x_ref, mod_ref = refs[0], refs[1]
    parts = refs[2:2 + 2 * n_parts]
    o_ref = refs[2 + 2 * n_parts]
    y = None
    for i in range(n_parts):
        t = jnp.dot(parts[2 * i][...], parts[2 * i + 1][...], preferred_element_type=F32)
        y = t if y is None else y + t
    gate = mod_ref[0, :, 2 * D_MODEL:3 * D_MODEL]
    o_ref[...] = x_ref[...] + gate * y


def _out_proj(x, mod, parts, tokens_per_batch, mod_row, tm):
    T = x.shape[0]
    blocks_per_batch = tokens_per_batch // tm
    tok = lambda w: pl.BlockSpec((tm, w), lambda i: (i, 0))
    if mod_row is None:
        mod_spec = pl.BlockSpec((1, 1, 3 * D_MODEL), lambda i: (i // blocks_per_batch, 0, 0))
    else:
        mod_spec = pl.BlockSpec((1, 1, 3 * D_MODEL), lambda i: (mod_row, 0, 0))
    in_specs = [tok(D_MODEL), mod_spec]
    args = [x, mod]
    for a, w in parts:
        in_specs += [tok(a.shape[1]), _const_spec(w.shape)]
        args += [a, w]
    return pl.pallas_call(
        functools.partial(_out_kernel, len(parts)),
        grid=(T // tm,),
        in_specs=in_specs,
        out_specs=tok(D_MODEL),
        out_shape=jax.ShapeDtypeStruct((T, D_MODEL), F32),
        compiler_params=_params(1),
        name="out_proj",
    )(*args)


def kernel(x_prompt, x_sample, cache_l0_mla_ckv, cache_l0_mla_krope, state_l0_ret_fwd, state_l0_ret_bwd, cache_l1_win_k, cache_l1_win_v, cache_l2_mla_ckv, cache_l2_mla_krope, state_l2_ret_fwd, state_l2_ret_bwd, cache_l3_win_k, cache_l3_win_v, c, c_ctx, ada_w, ada_b, norm_g, ab_w_in, mla_q_norm_g, mla_w_uq, mla_kv_norm_g, mla_w_ukv, mla_q_head_g, mla_k_head_g, ret_decay, ret_norm_g, ab_w_out, win_w_in, win_q_head_g, win_k_head_g, win_sink, win_w_out):
    BP, LP, D = x_prompt.shape
    BS, LS, _ = x_sample.shape
    P = cache_l0_mla_ckv.shape[1]
    ctx_caches = ((cache_l0_mla_ckv, cache_l0_mla_krope, state_l0_ret_fwd, state_l0_ret_bwd),
                  (cache_l1_win_k, cache_l1_win_v),
                  (cache_l2_mla_ckv, cache_l2_mla_krope, state_l2_ret_fwd, state_l2_ret_bwd),
                  (cache_l3_win_k, cache_l3_win_v))

    cond = jnp.concatenate([c, c_ctx[None, :], jnp.zeros((8 - BS - 1, D), F32)], axis=0)
    mod_all = _ada_mod(cond, ada_w, ada_b).reshape(DEPTH, 8, 1, 3 * D)
    ctx_row = BS

    c32, s32 = _rope_tables(LS, MLA_ROPE)
    c64, s64 = _rope_tables(LS, RET_DK)
    ones = jnp.ones((LS, MLA_NOPE), F32)
    zeros = jnp.zeros((LS, MLA_NOPE), F32)
    z32 = jnp.zeros((LS, LANES - MLA_QK), F32)
    tables = dict(cm=jnp.concatenate([ones, c32, z32], -1), sm=jnp.concatenate([zeros, s32, z32], -1),
                  c64=jnp.concatenate([c64, c64], -1), s64=jnp.concatenate([s64, s64], -1))

    y_p = x_prompt.reshape(BP * LP, D)
    y_s = x_sample.reshape(BS * LS, D)
    tm = 512
    new_state = []
    pend_p = pend_s = None

    def take_stream(outs, pending, y):
        return (outs[:-1], outs[-1]) if pending is not None else (outs, y)

    for l in range(DEPTH):
        i = l // 2
        mod = mod_all[l]
        p3 = lambda a: a.reshape(BP, LP, a.shape[-1])
        s3 = lambda a: a.reshape(BS, -1, a.shape[-1])
        flat = lambda a: a.reshape(-1, a.shape[-1])
        if l % 2 == 0:
            p = {'w_in': ab_w_in[i], 'q_norm_g': mla_q_norm_g[i], 'w_uq': mla_w_uq[i],
                 'kv_norm_g': mla_kv_norm_g[i], 'w_ukv': mla_w_ukv[i], 'q_head_g': mla_q_head_g[i],
                 'k_head_g': mla_k_head_g[i]}
            w_out = ab_w_out[i].astype(BF16)
            parts_w = (w_out[:MLA_WIDTH], w_out[MLA_WIDTH:])
            ckv_c, krope_c, sf, sb = ctx_caches[l]

            outs, y_p = take_stream(_prep_ab(y_p, mod, norm_g[l], p, False, LP, ctx_row, None, tm, res=pend_p),
                                    pend_p, y_p)
            q, k, v, gates, rq, rk, rv, ckv, kr = outs
            a_p = _mla_attn(p3(q), p3(k), p3(v), p3(gates), PROMPT_ROWS_PER_STEP, LP, 2)
            zst = jnp.zeros((BP, RET_HEADS, RET_DK, RET_DV), F32)
            r_p, rf, rb = _retention(p3(rq), p3(rk), p3(rv), zst, zst, ret_decay[i], ret_norm_g[i], p3(gates),
                                     PROMPT_ROWS_PER_STEP)
            new_state.append((ckv.reshape(BP, LP, KV_LORA),
                              kr[:, MLA_NOPE:MLA_QK].reshape(BP, LP, MLA_ROPE), rf, rb))
            pend_p = (mod, [(flat(a_p), parts_w[0]), (flat(r_p), parts_w[1])])

            kr_c = jnp.pad(krope_c.reshape(BS * P, MLA_ROPE), ((0, 0), (MLA_NOPE, LANES - MLA_QK)))
            outs, y_s = take_stream(_prep_ab(y_s, mod, norm_g[l], p, True, LS, None, tables, tm,
                                             (ckv_c.reshape(BS * P, KV_LORA), kr_c), res=pend_s), pend_s, y_s)
            q, k, v, gates, rq, rk, rv = outs
            a_s = _mla_attn(s3(q), s3(k), s3(v), s3(gates), 1, 512, 2)
            r_s, _, _ = _retention(s3(rq), s3(rk), s3(rv), sf, sb, ret_decay[i], ret_norm_g[i], s3(gates), 1)
            pend_s = (mod, [(flat(a_s), parts_w[0]), (flat(r_s), parts_w[1])])
        else:
            p = {'w_in': win_w_in[i], 'q_head_g': win_q_head_g[i], 'k_head_g': win_k_head_g[i]}
            w_out = win_w_out[i].astype(BF16)
            sink = win_sink[i]
            kc, vc = ctx_caches[l]

            outs, y_p = take_stream(_prep_win(y_p, mod, norm_g[l], p, False, LP, ctx_row, None, tm, res=pend_p),
                                    pend_p, y_p)
            q, k, v, gates, kst, vst = outs
            o_p = _win_attn(p3(q), p3(k), p3(v), None, p3(gates), sink, PROMPT_ROWS_PER_STEP, LP, 2)
            new_state.append((kst.reshape(BP, LP, WIN_KV_HEADS, WIN_HEAD_DIM),
                              vst.reshape(BP, LP, WIN_KV_HEADS, WIN_HEAD_DIM)))
            pend_p = (mod, [(flat(o_p), w_out)])

            outs, y_s = take_stream(_prep_win(y_s, mod, norm_g[l], p, True, LS, None, tables, tm, res=pend_s),
                                    pend_s, y_s)
            q, k, v, gates = outs

            def slots(a, fill):
                a = a.astype(BF16)
                f = jnp.broadcast_to(fill.astype(BF16), a.shape)
                return jnp.concatenate([a, f, f, a], axis=-1).reshape(BS, P, 4 * 2 * LANES)

            zero = jnp.zeros((WIN_HEAD_DIM,), F32)
            o_s = _win_attn(s3(q), slots(kc, zero), slots(vc, zero.at[0].set(1.0)), (s3(k), s3(v)), s3(gates),
                            sink, 1, 256, 4)
            pend_s = (mod, [(flat(o_s), w_out)])

    y_p = _out_proj(y_p, pend_p[0], pend_p[1], LP, ctx_row, tm)
    y_s = _out_proj(y_s, pend_s[0], pend_s[1], LS, None, tm)

    (l0_ckv, l0_krope, l0_rf, l0_rb), (l1_k, l1_v), (l2_ckv, l2_krope, l2_rf, l2_rb), (l3_k, l3_v) = new_state
    return (y_p.reshape(BP, LP, D), y_s.reshape(BS, LS, D), l0_ckv, l0_krope, l0_rf, l0_rb, l1_k, l1_v,
            l2_ckv, l2_krope, l2_rf, l2_rb, l3_k, l3_v)
```

```python
import functools

import numpy as np
import jax
import jax.numpy as jnp
from jax import lax
from jax.experimental import pallas as pl
from jax.experimental.pallas import tpu as pltpu

D_MODEL = 1024
DEPTH = 4
GRID_W = 64
ROPE_BASE = 10000.0
EPS = 1e-6

MLA_HEADS = 8
MLA_NOPE = 64
MLA_ROPE = 32
MLA_QK = MLA_NOPE + MLA_ROPE
MLA_V = 64
Q_LORA = 384
KV_LORA = 256
MLA_WIDTH = MLA_HEADS * MLA_V

RET_HEADS = 4
RET_DK = 64
RET_DV = 128
RET_CHUNK = 256
RET_WIDTH = RET_HEADS * RET_DV

WIN_HEADS = 16
WIN_KV_HEADS = 4
WIN_HEAD_DIM = 64
WINDOW = 128
WIN_WIDTH = WIN_HEADS * WIN_HEAD_DIM

LANES = 128
F32 = jnp.float32
BF16 = jnp.bfloat16
NEG = -1e30
LOG2E = 1.4426950408889634
VMEM_LIMIT = 52 * 1024 * 1024

TOKEN_BLOCK = 512
PROMPT_ROWS_PER_STEP = 8
MLA_Q_BLOCK = 512
MLA_PAIRS_PER_STEP = 2
MLA_KEY_CHUNK = 512
WIN_Q_BLOCK = 256
WIN_KV_PER_STEP = 4
WIN_KV_PER_STEP_PROMPT = 2

AB_OFF = dict(cq=0, ckv=384, krope=640, ga=672, rq=1184, rk=1440, rv=1696, gb=2208)
WIN_OFF = dict(q=0, k=1024, v=1280, g=1536)


def _offsets(widths):
    out, at = {}, 0
    for name, w in widths:
        out[name] = at
        at += w
    out['end'] = at
    return out


AB_SEGMENTS = (('cq', Q_LORA), ('ckv', KV_LORA), ('kr', LANES), ('g', MLA_WIDTH + RET_WIDTH),
               ('rq', RET_HEADS * RET_DK), ('rk', RET_HEADS * RET_DK), ('rv', RET_WIDTH))
AB_LAYOUT = _offsets(AB_SEGMENTS)


def _params(n_axes):
    return pltpu.CompilerParams(dimension_semantics=("arbitrary",) * n_axes, vmem_limit_bytes=VMEM_LIMIT)


def _const_spec(shape):
    nd = len(shape)
    return pl.BlockSpec(shape, lambda *_: (0,) * nd)


def _swap_idx(dim):
    d2 = dim // 2
    half = d2 // 2
    one = np.concatenate([np.arange(half, d2), np.arange(0, half)])
    return np.concatenate([one, d2 + one])


def _take_cols(w, cols):
    w = w.astype(BF16)
    pieces, i = [], 0
    while i < len(cols):
        j = i + 1
        if cols[i] < 0:
            while j < len(cols) and cols[j] < 0:
                j += 1
            pieces.append(jnp.zeros((w.shape[0], j - i), BF16))
        else:
            while j < len(cols) and cols[j] == cols[j - 1] + 1:
                j += 1
            pieces.append(w[:, cols[i]:cols[i] + j - i])
        i = j
    return jnp.concatenate(pieces, axis=1)


def _rope_tables(n_tokens, dim):
    n_rows = n_tokens // GRID_W
    half = dim // 4
    inv = jnp.power(jnp.float32(ROPE_BASE), -jnp.arange(half, dtype=F32) / half)

    def one(n_pos):
        ang = jnp.arange(n_pos, dtype=F32)[:, None] * inv[None, :]
        c, s = jnp.cos(ang), jnp.sin(ang)
        return jnp.concatenate([c, c], -1), jnp.concatenate([-s, s], -1)

    cr, sr = [jnp.repeat(a, GRID_W, axis=0) for a in one(n_rows)]
    cc, sc = [jnp.tile(a, (n_rows, 1)) for a in one(GRID_W)]
    return jnp.concatenate([cr, cc], -1), jnp.concatenate([sr, sc], -1)


def _ada_kernel(cond_ref, w_ref, b_ref, o_ref):
    c = cond_ref[...]
    sc = (c * jax.nn.sigmoid(c)).astype(BF16)
    o_ref[0] = jnp.dot(sc, w_ref[0].astype(BF16), preferred_element_type=F32) + b_ref[0]


def _ada_mod(cond, ada_w, ada_b):
    tn = 768
    n3 = 3 * D_MODEL
    return pl.pallas_call(
        _ada_kernel,
        grid=(DEPTH, n3 // tn),
        in_specs=[pl.BlockSpec((8, D_MODEL), lambda l, j: (0, 0)),
                  pl.BlockSpec((1, D_MODEL, tn), lambda l, j: (l, 0, j)),
                  pl.BlockSpec((1, 1, tn), lambda l, j: (l, 0, j))],
        out_specs=pl.BlockSpec((1, 8, tn), lambda l, j: (l, 0, j)),
        out_shape=jax.ShapeDtypeStruct((DEPTH, 8, n3), F32),
        compiler_params=_params(2),
        name="ada_mod",
    )(cond, ada_w, ada_b.reshape(DEPTH, 1, n3))


def _rms(x, g):
    return x * lax.rsqrt(jnp.mean(x * x, axis=-1, keepdims=True) + EPS) * g


def _modulated(x, mod_ref, ng_ref):
    shift = mod_ref[0, :, 0:D_MODEL]
    scale = mod_ref[0, :, D_MODEL:2 * D_MODEL]
    return (_rms(x, ng_ref[...]) * (1.0 + scale) + shift).astype(BF16)


def _block_input(x_ref, res_refs, y_out):
    x = x_ref[...]
    if not res_refs:
        return x
    upd = None
    for a_ref, w_ref in zip(res_refs[1::2], res_refs[2::2]):
        t = jnp.dot(a_ref[...], w_ref[...], preferred_element_type=F32)
        upd = t if upd is None else upd + t
    x = x + res_refs[0][0, :, 2 * D_MODEL:3 * D_MODEL] * upd
    y_out[...] = x
    return x


def _seg_rsqrt(x, seg_ref, n_real):
    ssq = jnp.dot((x * x).astype(BF16), seg_ref[...], preferred_element_type=F32)
    return lax.rsqrt(ssq * (1.0 / n_real) + EPS)


def _silu(g):
    return g * jax.nn.sigmoid(g)


def _rope_partner(x, half):
    lane = lax.broadcasted_iota(jnp.int32, (x.shape[0], LANES), 1)
    first = lane % (2 * half) < half
    cols = []
    for c in range(x.shape[1] // LANES):
        xc = x[:, LANES * c:LANES * (c + 1)]
        cols.append(jnp.where(first, pltpu.roll(xc, LANES - half, 1), pltpu.roll(xc, half, 1)))
    return cols[0] if len(cols) == 1 else jnp.concatenate(cols, axis=-1)


def _low_half(rows):
    return lax.broadcasted_iota(jnp.int32, (rows, LANES), 1) < LANES // 2


def _mla_kv_slots(nat):
    lo = _low_half(nat.shape[0])
    k_cols, v_cols = [], []
    for h in range(MLA_HEADS):
        col = nat[:, LANES * h:LANES * (h + 1)]
        k_cols.append(jnp.where(lo, col, 0.0))
        v_cols.append(jnp.where(lo, pltpu.roll(col, LANES // 2, 1), 0.0) if h % 2 == 0
                      else jnp.where(lo, 0.0, col))
    return jnp.concatenate(k_cols, axis=-1), _with_ones_lane(jnp.concatenate(v_cols, axis=-1))


def _win_kv_slots(x):
    lo = _low_half(x.shape[0])
    cols = []
    for c in range(x.shape[1] // LANES):
        xc = x[:, LANES * c:LANES * (c + 1)]
        xr = pltpu.roll(xc, LANES // 2, 1)
        cols += [jnp.where(lo, xc, 0.0), jnp.where(lo, 0.0, xr), jnp.where(lo, xr, 0.0), jnp.where(lo, 0.0, xc)]
    return jnp.concatenate(cols, axis=-1)


def _ones_lane(head):
    return MLA_V if head % 2 == 0 else 0


def _with_ones_lane(v):
    lane = lax.broadcasted_iota(jnp.int32, v.shape, 1)
    odd = (lane // LANES) % 2 == 1
    return jnp.where(lane % LANES == jnp.where(odd, _ones_lane(1), _ones_lane(0)), 1.0, v)


def _ctx_kv(ckv_ref, kr_ref, wkv_ref, seg_ref, kg_ref, k_out, v_out):
    kn, v = _mla_kv_slots(jnp.dot(ckv_ref[...].astype(BF16), wkv_ref[...], preferred_element_type=F32))
    v_out[...] = v.astype(BF16)
    kr = kr_ref[...]
    kg = kg_ref[...]
    kr2 = jnp.concatenate([kr, kr], axis=-1)
    kg2 = jnp.concatenate([kg, kg], axis=-1)
    for g in range(MLA_HEADS // 2):
        sl = slice(2 * LANES * g, 2 * LANES * (g + 1))
        kraw = kn[:, sl] + kr2
        k_out[:, sl] = (kraw * kg2 * _seg_rsqrt(kraw, seg_ref, MLA_QK)).astype(BF16)


def _prep_ab_kernel(rope, ctx_blocks, steps_per_batch, n_res, *refs):
    n_in = 17 if rope else 11
    n_ctx = 2 if ctx_blocks else 0
    main_in = refs[:n_in]
    ctx_in = refs[n_in:n_in + n_ctx]
    res_in = refs[n_in + n_ctx:n_in + n_ctx + n_res]
    outs = refs[n_in + n_ctx + n_res:]
    main_out, y_out = (outs[:-1], outs[-1]) if n_res else (outs, None)

    def own():
        x = _block_input(main_in[0], res_in, y_out)
        _prep_ab_body(rope, x, *main_in[1:], *main_out)

    if not ctx_blocks:
        own()
        return
    wkv_ref, seg_ref, kg_ref = main_in[7], main_in[8], main_in[11]
    is_ctx = pl.program_id(0) % steps_per_batch < ctx_blocks

    @pl.when(is_ctx)
    def _():
        _ctx_kv(*ctx_in, wkv_ref, seg_ref, kg_ref, main_out[1], main_out[2])

    pl.when(jnp.logical_not(is_ctx))(own)


def _prep_ab_body(rope, x, *refs):
    if rope:
        (mod_ref, ng_ref, win_ref, qng_ref, kvng_ref, wq_ref, wkv_ref, seg_ref,
         qg_ref, qgs_ref, kg_ref, kgs_ref, cm_ref, sm_ref, cr_ref, sr_ref,
         q_out, k_out, v_out, gate_out, rq_out, rk_out, rv_out) = refs
    else:
        (mod_ref, ng_ref, win_ref, qng_ref, kvng_ref, wq_ref, wkv_ref, seg_ref,
         qg_ref, kg_ref,
         q_out, k_out, v_out, gate_out, rq_out, rk_out, rv_out, ckv_out, kr_out) = refs
    o_kr, o_g, o_rq, o_rk, o_rv = AB_LAYOUT['kr'], AB_LAYOUT['g'], AB_LAYOUT['rq'], AB_LAYOUT['rk'], AB_LAYOUT['rv']

    h = _modulated(x, mod_ref, ng_ref)
    proj = jnp.dot(h, win_ref[...], preferred_element_type=F32)

    cqn = _rms(proj[:, 0:Q_LORA], qng_ref[...]).astype(BF16)
    ckvn_f = _rms(proj[:, Q_LORA:Q_LORA + KV_LORA], kvng_ref[...])
    ckvn = ckvn_f.astype(BF16)
    kr = proj[:, o_kr:o_kr + LANES]

    gate_out[...] = _silu(proj[:, o_g:o_g + 2 * MLA_WIDTH]).astype(BF16)
    rv_out[...] = proj[:, o_rv:o_rv + RET_WIDTH].astype(BF16)

    qm = jnp.dot(cqn, wq_ref[...], preferred_element_type=F32)
    kn, v = _mla_kv_slots(jnp.dot(ckvn, wkv_ref[...], preferred_element_type=F32))
    v_out[...] = v.astype(BF16)

    qscale = MLA_QK ** -0.5 * LOG2E
    if rope:
        krs = _rope_partner(kr, MLA_ROPE // 4)
        cm, sm = cm_ref[...], sm_ref[...]
        q_c = cm * qg_ref[...]
        q_s = sm * qgs_ref[...]
        kr_rot = kr * (cm * kg_ref[...]) + krs * (sm * kgs_ref[...])
        cr, sr = cr_ref[...], sr_ref[...]
        rq = proj[:, o_rq:o_rq + 256]
        rk = proj[:, o_rk:o_rk + 256]
        rqs = _rope_partner(rq, RET_DK // 4)
        rks = _rope_partner(rk, RET_DK // 4)
        for c in range(2):
            sl = slice(LANES * c, LANES * (c + 1))
            rq_out[:, sl] = ((rq[:, sl] * cr + rqs[:, sl] * sr) * (RET_DK ** -0.5)).astype(BF16)
            rk_out[:, sl] = (rk[:, sl] * cr + rks[:, sl] * sr).astype(BF16)
    else:
        q_c = jnp.broadcast_to(qg_ref[...], (qm.shape[0], LANES))
        kr_rot = kr * kg_ref[...]
        rq_out[...] = (proj[:, o_rq:o_rq + 256] * (RET_DK ** -0.5)).astype(BF16)
        rk_out[...] = proj[:, o_rk:o_rk + 256].astype(BF16)
        ckv_out[...] = ckvn_f
        kr_out[...] = kr

    kg = kg_ref[...]
    q_c2 = jnp.concatenate([q_c, q_c], axis=-1)
    kr2 = jnp.concatenate([kr, kr], axis=-1)
    kr_rot2 = jnp.concatenate([kr_rot, kr_rot], axis=-1)
    kg2 = jnp.concatenate([kg, kg], axis=-1)
    if rope:
        q_s2 = jnp.concatenate([q_s, q_s], axis=-1)
    for g in range(MLA_HEADS // 2):
        sl = slice(2 * LANES * g, 2 * LANES * (g + 1))
        qg = qm[:, sl]
        rq_n = _seg_rsqrt(qg, seg_ref, MLA_QK)
        qv = qg * q_c2
        if rope:
            qv = qv + _rope_partner(qg, MLA_ROPE // 4) * q_s2
        q_out[:, sl] = (qv * (rq_n * qscale)).astype(BF16)
        kgp = kn[:, sl]
        rk_n = _seg_rsqrt(kgp + kr2, seg_ref, MLA_QK)
        k_out[:, sl] = ((kgp * kg2 + kr_rot2) * rk_n).astype(BF16)


def _ab_weights(w_in, w_uq, w_ukv, q_head_g, k_head_g, rope):
    sw32 = _swap_idx(MLA_ROPE)
    z = lambda n: [-1] * n
    o = AB_OFF
    run = lambda start, n: list(range(start, start + n))
    source = dict(cq=run(o['cq'], Q_LORA), ckv=run(o['ckv'], KV_LORA),
                  kr=z(MLA_NOPE) + run(o['krope'], MLA_ROPE) + z(LANES - MLA_QK),
                  g=run(o['ga'], MLA_WIDTH) + run(o['gb'], RET_WIDTH),
                  rq=run(o['rq'], RET_HEADS * RET_DK), rk=run(o['rk'], RET_HEADS * RET_DK),
                  rv=run(o['rv'], RET_WIDTH))
    cols = []
    for name, width in AB_SEGMENTS:
        assert len(source[name]) == width and len(cols) == AB_LAYOUT[name]
        cols += source[name]
    win = _take_cols(w_in, cols).astype(BF16)

    qc = []
    for hh in range(MLA_HEADS):
        qc += list(range(MLA_QK * hh, MLA_QK * (hh + 1))) + z(LANES - MLA_QK)
    wq = _take_cols(w_uq, qc).astype(BF16)
    wkv = w_ukv.astype(BF16)

    pad = lambda g: jnp.concatenate([g, jnp.zeros((LANES - MLA_QK,), F32)])[None, :]
    gsw = lambda g: jnp.concatenate([jnp.zeros((MLA_NOPE,), F32), g[MLA_NOPE + sw32],
                                     jnp.zeros((LANES - MLA_QK,), F32)])[None, :]
    qg, kg = pad(q_head_g), pad(k_head_g)
    qgs, kgs = (gsw(q_head_g), gsw(k_head_g)) if rope else (None, None)
    return win, wq, wkv, qg, qgs, kg, kgs


def _seg_matrix(width):
    idx = np.arange(2 * LANES) // width
    return jnp.asarray((idx[:, None] == idx[None, :]).astype(np.float32), BF16)


def _prep_ab(x, mod, norm_g, p, rope, tokens_per_batch, mod_row, tables, tm, ctx=None, res=None):
    T = x.shape[0]
    win, wq, wkv, qg, qgs, kg, kgs = _ab_weights(
        p['w_in'], p['w_uq'], p['w_ukv'], p['q_head_g'], p['k_head_g'], rope)
    seg = _seg_matrix(LANES)
    blocks_per_batch = max(tokens_per_batch // tm, 1)
    ctx_blocks = 0
    if ctx is not None:
        n_batch = T // tokens_per_batch
        ctx_rows = ctx[0].shape[0] // n_batch
        assert rope and ctx_rows % tm == 0 and tokens_per_batch % tm == 0
        ctx_blocks = ctx_rows // tm
    steps_per_batch = blocks_per_batch + ctx_blocks
    batch = lambda i: i // steps_per_batch
    in_batch = lambda i: jnp.maximum(i % steps_per_batch - ctx_blocks, 0)
    own = lambda i: batch(i) * blocks_per_batch + in_batch(i)
    tok = lambda w: pl.BlockSpec((tm, w), lambda i: (own(i), 0))
    if mod_row is None:
        mod_spec = pl.BlockSpec((1, 1, 3 * D_MODEL), lambda i: (batch(i), 0, 0))
    else:
        mod_spec = pl.BlockSpec((1, 1, 3 * D_MODEL), lambda i: (mod_row, 0, 0))
    vec = lambda a: (a, _const_spec(a.shape))
    ins = [(x, tok(D_MODEL)), (mod, mod_spec), vec(norm_g[None, :]), vec(win),
           vec(p['q_norm_g'][None, :]), vec(p['kv_norm_g'][None, :]), vec(wq),
           vec(wkv), vec(seg), vec(qg)]
    if rope:
        ins.append(vec(qgs))
    ins.append(vec(kg))
    if rope:
        ins.append(vec(kgs))
        tab = lambda a: (a, pl.BlockSpec((tm, LANES), lambda i: (in_batch(i), 0)))
        ins += [tab(tables['cm']), tab(tables['sm']), tab(tables['c64']), tab(tables['s64'])]
    kv_rows, kv_spec = T, tok(1024)
    if ctx_blocks:
        cspec = lambda w: pl.BlockSpec((tm, w), lambda i: (
            batch(i) * ctx_blocks + jnp.minimum(i % steps_per_batch, ctx_blocks - 1), 0))
        ins += [(ctx[0], cspec(KV_LORA)), (ctx[1], cspec(LANES))]
        kv_rows = T + ctx[0].shape[0]
        kv_spec = pl.BlockSpec((tm, 1024), lambda i: (i, 0))
    outs = [(T, 1024, BF16, tok(1024)), (kv_rows, 1024, BF16, kv_spec), (kv_rows, 1024, BF16, kv_spec),
            (T, 1024, BF16, tok(1024)), (T, 256, BF16, tok(256)), (T, 256, BF16, tok(256)),
            (T, 512, BF16, tok(512))]
    if not rope:
        outs += [(T, KV_LORA, F32, tok(KV_LORA)), (T, LANES, F32, tok(LANES))]
    n_res = 0
    if res is not None:
        ins.append((res[0], mod_spec))
        for a, w in res[1]:
            ins += [(a, tok(a.shape[1])), vec(w)]
        n_res = 1 + 2 * len(res[1])
        outs.append((T, D_MODEL, F32, tok(D_MODEL)))
    return pl.pallas_call(
        functools.partial(_prep_ab_kernel, rope, ctx_blocks, steps_per_batch, n_res),
        grid=(kv_rows // tm,),
        in_specs=[s for _, s in ins],
        out_specs=[o[3] for o in outs],
        out_shape=[jax.ShapeDtypeStruct(o[:2], o[2]) for o in outs],
        compiler_params=_params(1),
        name="prep_ab_rope" if rope else "prep_ab",
    )(*[a for a, _ in ins])


def _lane_groups(x, op):
    out = x[:, 0:LANES]
    for t in range(1, x.shape[1] // LANES):
        out = op(out, x[:, LANES * t:LANES * (t + 1)])
    return out


def _mla_attn_kernel(tk, q_ref, k_ref, v_ref, g_ref, o_ref, s_ref):
    dn = (((1,), (1,)), ((), ()))
    tq = q_ref.shape[1]
    nk = k_ref.shape[1] // tk
    heads = [(b, j) for b in range(q_ref.shape[0]) for j in range(q_ref.shape[2] // LANES)]
    m_prev, out = None, None
    for idx in range(len(heads) + 1):
        cur = heads[idx] if idx < len(heads) else None
        prev = heads[idx - 1] if idx > 0 else None
        if cur is not None:
            cb, cj = cur
            csl = slice(LANES * cj, LANES * (cj + 1))
            q = q_ref[cb, :, csl]
            mt = None
        if prev is not None:
            pb, pj = prev
            psl = slice(LANES * pj, LANES * (pj + 1))
            acc = jnp.zeros((tq, LANES), F32)
        for c in range(nk):
            rows = slice(c * tk, (c + 1) * tk)
            if cur is not None:
                s = lax.dot_general(q, k_ref[cb, rows, csl], dn, preferred_element_type=F32)
                s_ref[idx % 2, :, rows] = s
                smax = _lane_groups(s, jnp.maximum)
                mt = smax if mt is None else jnp.maximum(mt, smax)
            if prev is not None:
                p = jnp.exp2(s_ref[(idx - 1) % 2, :, rows] - m_prev)
                acc = acc + jnp.dot(p.astype(BF16), v_ref[pb, rows, psl], preferred_element_type=F32)
        if prev is not None:
            lane = lax.broadcasted_iota(jnp.int32, (tq, LANES), 1)
            l = jnp.sum(jnp.where(lane == _ones_lane(pj), acc, 0.0), axis=-1, keepdims=True)
            oj = jnp.where((lane >= MLA_V) == (pj % 2 == 1), acc, 0.0) / l
            out = oj if out is None else out + oj
            if pj % 2 == 1:
                osl = slice(LANES * (pj // 2), LANES * (pj // 2 + 1))
                o_ref[pb, :, osl] = (out * g_ref[pb, :, osl].astype(F32)).astype(BF16)
                out = None
        if cur is not None:
            m_prev = jnp.max(mt, axis=-1, keepdims=True)


def _mla_attn(q, k, v, gates, bb, tq, pairs):
    B, L, _ = q.shape
    Lk = k.shape[1]
    hp = MLA_HEADS // 2 // pairs
    qspec = pl.BlockSpec((bb, tq, 2 * LANES * pairs), lambda b, h, i: (b, i, h))
    kspec = pl.BlockSpec((bb, Lk, 2 * LANES * pairs), lambda b, h, i: (b, 0, h))
    ospec = pl.BlockSpec((bb, tq, LANES * pairs), lambda b, h, i: (b, i, h))
    tk = min(Lk, MLA_KEY_CHUNK)
    assert Lk % tk == 0
    return pl.pallas_call(
        functools.partial(_mla_attn_kernel, tk),
        grid=(B // bb, hp, L // tq),
        in_specs=[qspec, kspec, kspec, ospec],
        out_specs=ospec,
        out_shape=jax.ShapeDtypeStruct((B, L, MLA_WIDTH), BF16),
        scratch_shapes=[pltpu.VMEM((2, tq, Lk), F32)],
        compiler_params=_params(3),
        name="mla_attn",
    )(q, k, v, gates)


def _ret_kernel(n_chunks, decay_ref, q_ref, k_ref, v_ref, sf_ref, sb_ref, gn_ref, gate_ref,
                o_ref, rf_ref, rb_ref, st_ref, u_ref, r_ref, dmask_ref, qw_ref, kwt_ref, sdec_ref):
    C = RET_CHUNK
    pair = pl.program_id(0)
    bb = q_ref.shape[0]
    lane_hi = lax.broadcasted_iota(jnp.int32, (C, LANES), 1) >= RET_DK

    @pl.when(pl.program_id(1) == 0)
    def _():
        rel = (lax.broadcasted_iota(jnp.int32, (C, C), 0) - lax.broadcasted_iota(jnp.int32, (C, C), 1)).astype(F32)
        qrow = lax.broadcasted_iota(jnp.int32, (C, LANES), 0).astype(F32)
        kcol = lax.broadcasted_iota(jnp.int32, (LANES, C), 1).astype(F32)
        krow_hi = lax.broadcasted_iota(jnp.int32, (LANES, C), 0) >= RET_DK
        srow_hi = lax.broadcasted_iota(jnp.int32, (LANES, 2 * RET_DV), 0) >= RET_DK
        scol_hi = lax.broadcasted_iota(jnp.int32, (LANES, 2 * RET_DV), 1) >= RET_DV
        same_head = srow_hi == scol_hi
        sdec_ref[2] = jnp.where(same_head, 1.0, 0.0)
        log_decay = lambda shape, d, hh: -jnp.exp(jnp.full(shape, decay_ref[d, 2 * pair + hh], F32))
        for hh in range(2):
            fwd = jnp.where(rel >= 0, jnp.exp(jnp.maximum(rel, 0.0) * log_decay((C, C), 0, hh)), 0.0)
            bwd = jnp.where(rel <= 0, jnp.exp(jnp.maximum(-rel, 0.0) * log_decay((C, C), 1, hh)), 0.0)
            dmask_ref[hh] = fwd + bwd
        for d in range(2):
            lg_q = jnp.where(lane_hi, log_decay((C, LANES), d, 1), log_decay((C, LANES), d, 0))
            lg_k = jnp.where(krow_hi, log_decay((LANES, C), d, 1), log_decay((LANES, C), d, 0))
            lg_s = jnp.where(srow_hi, log_decay((LANES, 2 * RET_DV), d, 1), log_decay((LANES, 2 * RET_DV), d, 0))
            if d == 0:
                qw_ref[d] = jnp.exp((qrow + 1.0) * lg_q)
                kwt_ref[d] = jnp.exp((C - 1.0 - kcol) * lg_k)
            else:
                qw_ref[d] = jnp.exp((C - qrow) * lg_q)
                kwt_ref[d] = jnp.exp(kcol * lg_k)
            sdec_ref[d] = jnp.where(same_head, jnp.exp(C * lg_s), 0.0)

    def updates(n):
        r0 = n * C
        for b in range(bb):
            kt = k_ref[b, pl.ds(r0, C), :].astype(F32).T
            vc = v_ref[b, pl.ds(r0, C), :]
            for d in range(2):
                u_ref[b, n, d] = jnp.dot((kt * kwt_ref[d]).astype(BF16), vc, preferred_element_type=F32)

    for n in range(n_chunks):
        updates(n)

    for b in range(bb):
        for d, s_ref in ((0, sf_ref), (1, sb_ref)):
            st_ref[b, d] = jnp.zeros((LANES, 2 * RET_DV), F32)
            for hh in range(2):
                st_ref[b, d, RET_DK * hh:RET_DK * (hh + 1), RET_DV * hh:RET_DV * (hh + 1)] = s_ref[b, hh]

    def scan(n, _):
        for b in range(bb):
            for d in range(2):
                c = n if d == 0 else n_chunks - 1 - n
                state = st_ref[b, d]
                r_ref[b, c, LANES * d:LANES * (d + 1), :] = state.astype(BF16)
                st_ref[b, d] = sdec_ref[d] * state + u_ref[b, c, d] * sdec_ref[2]
        return 0

    lax.fori_loop(0, n_chunks, scan, 0)

    for b in range(bb):
        for hh in range(2):
            rf_ref[b, hh] = st_ref[b, 0, RET_DK * hh:RET_DK * (hh + 1), RET_DV * hh:RET_DV * (hh + 1)]
            rb_ref[b, hh] = st_ref[b, 1, RET_DK * hh:RET_DK * (hh + 1), RET_DV * hh:RET_DV * (hh + 1)]

    dn = (((1,), (1,)), ((), ()))
    ones = jnp.ones((RET_DV, RET_DV), BF16)

    def outputs(n):
        r0 = n * C
        for b in range(bb):
            qc = q_ref[b, pl.ds(r0, C), :].astype(F32)
            kc = k_ref[b, pl.ds(r0, C), :]
            vc = v_ref[b, pl.ds(r0, C), :]
            q2 = jnp.concatenate([jnp.where(lane_hi, 0.0, qc), jnp.where(lane_hi, qc, 0.0)], axis=0)
            s2 = lax.dot_general(q2.astype(BF16), kc, dn, preferred_element_type=F32)
            qw = jnp.concatenate([qc * qw_ref[0], qc * qw_ref[1]], axis=-1).astype(BF16)
            inter = jnp.dot(qw, r_ref[b, n], preferred_element_type=F32)
            for hh in range(2):
                sl = slice(RET_DV * hh, RET_DV * (hh + 1))
                y = jnp.dot((s2[C * hh:C * (hh + 1)] * dmask_ref[hh]).astype(BF16), vc[:, sl],
                            preferred_element_type=F32) + inter[:, sl]
                yc = y - jnp.mean(y, axis=-1, keepdims=True)
                var = jnp.dot((yc * yc).astype(BF16), ones, preferred_element_type=F32) * (1.0 / RET_DV)
                out = yc * lax.rsqrt(var + EPS) * gn_ref[:, sl]
                o_ref[b, pl.ds(r0, C), sl] = (out * gate_ref[b, pl.ds(r0, C), sl].astype(F32)).astype(BF16)

    for n in range(n_chunks):
        outputs(n)


def _retention(rq, rk, rv, sf, sb, decay, ret_norm_g, gates, bb):
    B, L, _ = rq.shape
    C = RET_CHUNK
    n_chunks = L // C
    hp = RET_HEADS // 2
    st_spec = pl.BlockSpec((bb, 2, RET_DK, RET_DV), lambda h, b: (b, h, 0, 0))
    st_shape = jax.ShapeDtypeStruct((B, RET_HEADS, RET_DK, RET_DV), F32)
    qk_spec = pl.BlockSpec((bb, L, LANES), lambda h, b: (b, 0, h))
    v_spec = pl.BlockSpec((bb, L, 2 * RET_DV), lambda h, b: (b, 0, h))
    return pl.pallas_call(
        functools.partial(_ret_kernel, n_chunks),
        grid=(hp, B // bb),
        in_specs=[pl.BlockSpec(memory_space=pltpu.SMEM), qk_spec, qk_spec, v_spec, st_spec, st_spec,
                  pl.BlockSpec((1, 2 * RET_DV), lambda h, b: (0, h)),
                  pl.BlockSpec((bb, L, 2 * RET_DV), lambda h, b: (b, 0, hp + h))],
        out_specs=[v_spec, st_spec, st_spec],
        out_shape=[jax.ShapeDtypeStruct((B, L, RET_WIDTH), BF16), st_shape, st_shape],
        scratch_shapes=[pltpu.VMEM((bb, 2, LANES, 2 * RET_DV), F32),
                        pltpu.VMEM((bb, n_chunks, 2, LANES, 2 * RET_DV), F32),
                        pltpu.VMEM((bb, n_chunks, 2 * LANES, 2 * RET_DV), BF16),
                        pltpu.VMEM((2, C, C), F32),
                        pltpu.VMEM((2, C, LANES), F32),
                        pltpu.VMEM((2, LANES, C), F32),
                        pltpu.VMEM((3, LANES, 2 * RET_DV), F32)],
        compiler_params=_params(2),
        name="retention",
    )(decay, rq, rk, rv, sf, sb, ret_norm_g[None, :], gates)


def _prep_win_kernel(rope, n_res, *refs):
    n_in = 11 if rope else 7
    res_in = refs[n_in:n_in + n_res]
    outs = refs[n_in + n_res:]
    main_out, y_out = (outs[:-1], outs[-1]) if n_res else (outs, None)
    if rope:
        x_ref, mod_ref, ng_ref, win_ref, seg_ref, qg_ref, qgs_ref, kg_ref, kgs_ref, c_ref, s_ref = refs[:n_in]
        q_out, k_out, v_out, gate_out = main_out
    else:
        x_ref, mod_ref, ng_ref, win_ref, seg_ref, qg_ref, kg_ref = refs[:n_in]
        q_out, k_out, v_out, gate_out, kst_out, vst_out = main_out
    o = WIN_OFF
    h = _modulated(_block_input(x_ref, res_in, y_out), mod_ref, ng_ref)
    proj = jnp.dot(h, win_ref[...], preferred_element_type=F32)
    gate_out[...] = _silu(proj[:, o['g']:o['g'] + WIN_WIDTH]).astype(BF16)

    qg2 = jnp.concatenate([qg_ref[...]] * 2, axis=-1)
    kg2 = jnp.concatenate([kg_ref[...]] * 2, axis=-1)
    if rope:
        c2 = jnp.concatenate([c_ref[...]] * 2, axis=-1)
        s2 = jnp.concatenate([s_ref[...]] * 2, axis=-1)
        q_c, q_s = c2 * qg2, s2 * jnp.concatenate([qgs_ref[...]] * 2, axis=-1)
        k_c, k_s = c2 * kg2, s2 * jnp.concatenate([kgs_ref[...]] * 2, axis=-1)
    qscale = WIN_HEAD_DIM ** -0.5 * LOG2E
    for g in range(WIN_WIDTH // 256):
        sl = slice(256 * g, 256 * (g + 1))
        qg = proj[:, sl]
        rn = _seg_rsqrt(qg, seg_ref, WIN_HEAD_DIM) * qscale
        if rope:
            qv = qg * q_c + _rope_partner(qg, WIN_HEAD_DIM // 4) * q_s
        else:
            qv = qg * qg2
        q_out[:, sl] = (qv * rn).astype(BF16)
    kraw = proj[:, o['k']:o['k'] + 256]
    rn = _seg_rsqrt(kraw, seg_ref, WIN_HEAD_DIM)
    if rope:
        kn = (kraw * k_c + _rope_partner(kraw, WIN_HEAD_DIM // 4) * k_s) * rn
    else:
        kn = kraw * kg2 * rn
    v = proj[:, o['v']:o['v'] + 256]
    if not rope:
        kst_out[...] = kn
        vst_out[...] = v
    k_out[...] = _win_kv_slots(kn).astype(BF16)
    v_out[...] = _with_ones_lane(_win_kv_slots(v)).astype(BF16)


def _prep_win(x, mod, norm_g, p, rope, tokens_per_batch, mod_row, tables, tm, res=None):
    T = x.shape[0]
    sw64 = _swap_idx(WIN_HEAD_DIM)
    win = p['w_in'].astype(BF16)
    seg = _seg_matrix(WIN_HEAD_DIM)
    rep = lambda g: jnp.concatenate([g, g])[None, :]
    blocks_per_batch = tokens_per_batch // tm
    tok = lambda w: pl.BlockSpec((tm, w), lambda i: (i, 0))
    if mod_row is None:
        mod_spec = pl.BlockSpec((1, 1, 3 * D_MODEL), lambda i: (i // blocks_per_batch, 0, 0))
    else:
        mod_spec = pl.BlockSpec((1, 1, 3 * D_MODEL), lambda i: (mod_row, 0, 0))
    vec = lambda a: (a, _const_spec(a.shape))
    ins = [(x, tok(D_MODEL)), (mod, mod_spec), vec(norm_g[None, :]), vec(win), vec(seg),
           vec(rep(p['q_head_g']))]
    if rope:
        ins.append(vec(rep(p['q_head_g'][sw64])))
    ins.append(vec(rep(p['k_head_g'])))
    if rope:
        ins.append(vec(rep(p['k_head_g'][sw64])))
        tab = lambda a: (a, pl.BlockSpec((tm, LANES), lambda i: (i % blocks_per_batch, 0)))
        ins += [tab(tables['c64']), tab(tables['s64'])]
    outs = [(1024, BF16), (1024, BF16), (1024, BF16), (1024, BF16)]
    if not rope:
        outs += [(256, F32), (256, F32)]
    n_res = 0
    if res is not None:
        ins.append((res[0], mod_spec))
        for a, w in res[1]:
            ins += [(a, tok(a.shape[1])), vec(w)]
        n_res = 1 + 2 * len(res[1])
        outs.append((D_MODEL, F32))
    return pl.pallas_call(
        functools.partial(_prep_win_kernel, rope, n_res),
        grid=(T // tm,),
        in_specs=[s for _, s in ins],
        out_specs=[tok(w) for w, _ in outs],
        out_shape=[jax.ShapeDtypeStruct((T, w), dt) for w, dt in outs],
        compiler_params=_params(1),
        name="prep_win_rope" if rope else "prep_win",
    )(*[a for a, _ in ins])


def _win_attn_kernel(local, *refs):
    if local:
        sink_ref, q_ref, kf_ref, vf_ref, k_ref, v_ref, g_ref, bias_ref, o_ref, s_ref = refs
    else:
        sink_ref, q_ref, kf_ref, vf_ref, g_ref, o_ref, s_ref = refs
    tq = q_ref.shape[1]
    nf = kf_ref.shape[1]
    kv_per_step = q_ref.shape[2] // (2 * LANES)
    upper = lax.broadcasted_iota(jnp.int32, (2 * tq, 1), 0) >= tq
    lane = lax.broadcasted_iota(jnp.int32, (2 * tq, LANES), 1)
    if local:
        span = bias_ref.shape[2]
        start = jnp.clip(pl.program_id(2) * tq - WINDOW, 0, k_ref.shape[1] - span)
        start = pl.multiple_of(start, WINDOW)
    dn = (((1,), (1,)), ((), ()))
    units = [(b, jj, half) for b in range(q_ref.shape[0]) for jj in range(kv_per_step) for half in range(2)]
    m_prev, sk_prev, out = None, None, None
    for idx in range(len(units) + 1):
        cur = units[idx] if idx < len(units) else None
        prev = units[idx - 1] if idx > 0 else None
        if cur is not None:
            cb, cjj, chalf = cur
            base = 2 * LANES * cjj
            csl = slice(base + LANES * chalf, base + LANES * (chalf + 1))
            q2 = jnp.concatenate([q_ref[cb, :, base:base + LANES],
                                  q_ref[cb, :, base + LANES:base + 2 * LANES]], axis=0)
            head = 4 * (pl.program_id(1) * kv_per_step + cjj) + chalf
            sk = jnp.where(upper, sink_ref[head + 2], sink_ref[head]) * LOG2E
            s_ctx = lax.dot_general(q2, kf_ref[cb, :, csl], dn, preferred_element_type=F32)
            s_ref[idx % 2, :, 0:nf] = s_ctx
            mt = _lane_groups(s_ctx, jnp.maximum)
        if prev is not None:
            pb, pjj, phalf = prev
            pbase = 2 * LANES * pjj
            psl = slice(pbase + LANES * phalf, pbase + LANES * (phalf + 1))
            e_ctx = jnp.exp2(s_ref[(idx - 1) % 2, :, 0:nf] - m_prev)
            acc = jnp.dot(e_ctx.astype(BF16), vf_ref[pb, :, psl], preferred_element_type=F32)
        if local and cur is not None:
            s_loc = lax.dot_general(q2, k_ref[cb, pl.ds(start, span), csl], dn,
                                    preferred_element_type=F32) + bias_ref[0]
            s_ref[idx % 2, :, nf:nf + span] = s_loc
            mt = jnp.maximum(mt, _lane_groups(s_loc, jnp.maximum))
        if local and prev is not None:
            e_loc = jnp.exp2(s_ref[(idx - 1) % 2, :, nf:nf + span] - m_prev)
            acc = acc + jnp.dot(e_loc.astype(BF16), v_ref[pb, pl.ds(start, span), psl],
                                preferred_element_type=F32)
        if prev is not None:
            den = jnp.sum(jnp.where(lane == _ones_lane(phalf), acc, 0.0), axis=-1, keepdims=True)
            den = den + jnp.exp2(sk_prev - m_prev)
            oh = jnp.where((lane >= WIN_HEAD_DIM) == (phalf == 1), acc, 0.0) / den
            out = oh if out is None else out + oh
            if phalf == 1:
                gate = g_ref[pb, :, pbase:pbase + 2 * LANES].astype(F32)
                o_ref[pb, :, pbase:pbase + LANES] = (out[0:tq] * gate[:, 0:LANES]).astype(BF16)
                o_ref[pb, :, pbase + LANES:pbase + 2 * LANES] = (
                    out[tq:2 * tq] * gate[:, LANES:2 * LANES]).astype(BF16)
                out = None
        if cur is not None:
            m_prev = jnp.maximum(jnp.max(mt, axis=-1, keepdims=True), sk)
            sk_prev = sk


def _window_bias(tq):
    span = tq + 2 * WINDOW
    qi = jnp.arange(2 * tq)[:, None] % tq
    kk = jnp.arange(span)[None, :]
    cases = [jnp.abs(kk - off - qi) <= WINDOW for off in (0, WINDOW, 2 * WINDOW)]
    return jnp.where(jnp.stack(cases), 0.0, NEG).astype(F32)


def _win_attn(q, kf, vf, own, gates, sink, bb, tq, kv_per_step):
    B, L, _ = q.shape
    nb = L // tq
    width = 2 * LANES * kv_per_step
    qspec = pl.BlockSpec((bb, tq, width), lambda b, h, i: (b, i, h))
    fspec = pl.BlockSpec((bb, kf.shape[1], width), lambda b, h, i: (b, 0, h))
    in_specs = [pl.BlockSpec(memory_space=pltpu.SMEM), qspec, fspec, fspec]
    args = [sink, q, kf, vf]
    n_keys = kf.shape[1]
    if own is not None:
        assert nb >= 3 and bb == 1
        kspec = pl.BlockSpec((bb, L, width), lambda b, h, i: (b, 0, h), pipeline_mode=pl.Buffered(1))
        in_specs += [kspec, kspec]
        args += list(own)
    in_specs.append(qspec)
    args.append(gates)
    if own is not None:
        bias = _window_bias(tq)
        n_keys += bias.shape[2]
        in_specs.append(pl.BlockSpec(
            (1,) + bias.shape[1:],
            lambda b, h, i: (jnp.where(i == 0, 0, jnp.where(i == nb - 1, 2, 1)), 0, 0)))
        args.append(bias)
    return pl.pallas_call(
        functools.partial(_win_attn_kernel, own is not None),
        grid=(B // bb, WIN_KV_HEADS // kv_per_step, nb),
        in_specs=in_specs,
        out_specs=qspec,
        out_shape=jax.ShapeDtypeStruct((B, L, WIN_WIDTH), BF16),
        scratch_shapes=[pltpu.VMEM((2, 2 * tq, n_keys), F32)],
        compiler_params=_params(3),
        name="win_attn_local" if own is not None else "win_attn",
    )(*args)


def _out_kernel(n_parts, *refs):
    x_ref, mod_ref = refs[0], refs[1]
    parts = refs[2:2 + 2 * n_parts]
    o_ref = refs[2 + 2 * n_parts]
    y = None
    for i in range(n_parts):
        t = jnp.dot(parts[2 * i][...], parts[2 * i + 1][...], preferred_element_type=F32)
        y = t if y is None else y + t
    gate = mod_ref[0, :, 2 * D_MODEL:3 * D_MODEL]
    o_ref[...] = x_ref[...] + gate * y


def _out_proj(x, mod, parts, tokens_per_batch, mod_row, tm):
    T = x.shape[0]
    blocks_per_batch = tokens_per_batch // tm
    tok = lambda w: pl.BlockSpec((tm, w), lambda i: (i, 0))
    if mod_row is None:
        mod_spec = pl.BlockSpec((1, 1, 3 * D_MODEL), lambda i: (i // blocks_per_batch, 0, 0))
    else:
        mod_spec = pl.BlockSpec((1, 1, 3 * D_MODEL), lambda i: (mod_row, 0, 0))
    in_specs = [tok(D_MODEL), mod_spec]
    args = [x, mod]
    for a, w in parts:
        in_specs += [tok(a.shape[1]), _const_spec(w.shape)]
        args += [a, w]
    return pl.pallas_call(
        functools.partial(_out_kernel, len(parts)),
        grid=(T // tm,),
        in_specs=in_specs,
        out_specs=tok(D_MODEL),
        out_shape=jax.ShapeDtypeStruct((T, D_MODEL), F32),
        compiler_params=_params(1),
        name="out_proj",
    )(*args)


def kernel(x_prompt, x_sample, cache_l0_mla_ckv, cache_l0_mla_krope, state_l0_ret_fwd, state_l0_ret_bwd, cache_l1_win_k, cache_l1_win_v, cache_l2_mla_ckv, cache_l2_mla_krope, state_l2_ret_fwd, state_l2_ret_bwd, cache_l3_win_k, cache_l3_win_v, c, c_ctx, ada_w, ada_b, norm_g, ab_w_in, mla_q_norm_g, mla_w_uq, mla_kv_norm_g, mla_w_ukv, mla_q_head_g, mla_k_head_g, ret_decay, ret_norm_g, ab_w_out, win_w_in, win_q_head_g, win_k_head_g, win_sink, win_w_out):
    BP, LP, D = x_prompt.shape
    BS, LS, _ = x_sample.shape
    P = cache_l0_mla_ckv.shape[1]
    ctx_caches = ((cache_l0_mla_ckv, cache_l0_mla_krope, state_l0_ret_fwd, state_l0_ret_bwd),
                  (cache_l1_win_k, cache_l1_win_v),
                  (cache_l2_mla_ckv, cache_l2_mla_krope, state_l2_ret_fwd, state_l2_ret_bwd),
                  (cache_l3_win_k, cache_l3_win_v))

    cond = jnp.concatenate([c, c_ctx[None, :], jnp.zeros((8 - BS - 1, D), F32)], axis=0)
    mod_all = _ada_mod(cond, ada_w, ada_b).reshape(DEPTH, 8, 1, 3 * D)
    ctx_row = BS

    c32, s32 = _rope_tables(LS, MLA_ROPE)
    c64, s64 = _rope_tables(LS, RET_DK)
    ones = jnp.ones((LS, MLA_NOPE), F32)
    zeros = jnp.zeros((LS, MLA_NOPE), F32)
    z32 = jnp.zeros((LS, LANES - MLA_QK), F32)
    tables = dict(cm=jnp.concatenate([ones, c32, z32], -1), sm=jnp.concatenate([zeros, s32, z32], -1),
                  c64=jnp.concatenate([c64, c64], -1), s64=jnp.concatenate([s64, s64], -1))

    y_p = x_prompt.reshape(BP * LP, D)
    y_s = x_sample.reshape(BS * LS, D)
    tm = TOKEN_BLOCK
    new_state = []
    pend_p = pend_s = None

    def take_stream(outs, pending, y):
        return (outs[:-1], outs[-1]) if pending is not None else (outs, y)

    for l in range(DEPTH):
        i = l // 2
        mod = mod_all[l]
        p3 = lambda a: a.reshape(BP, LP, a.shape[-1])
        s3 = lambda a: a.reshape(BS, -1, a.shape[-1])
        flat = lambda a: a.reshape(-1, a.shape[-1])
        if l % 2 == 0:
            p = {'w_in': ab_w_in[i], 'q_norm_g': mla_q_norm_g[i], 'w_uq': mla_w_uq[i],
                 'kv_norm_g': mla_kv_norm_g[i], 'w_ukv': mla_w_ukv[i], 'q_head_g': mla_q_head_g[i],
                 'k_head_g': mla_k_head_g[i]}
            w_out = ab_w_out[i].astype(BF16)
            parts_w = (w_out[:MLA_WIDTH], w_out[MLA_WIDTH:])
            ckv_c, krope_c, sf, sb = ctx_caches[l]

            outs, y_p = take_stream(_prep_ab(y_p, mod, norm_g[l], p, False, LP, ctx_row, None, tm, res=pend_p),
                                    pend_p, y_p)
            q, k, v, gates, rq, rk, rv, ckv, kr = outs
            a_p = _mla_attn(p3(q), p3(k), p3(v), p3(gates), PROMPT_ROWS_PER_STEP, LP, MLA_PAIRS_PER_STEP)
            zst = jnp.zeros((BP, RET_HEADS, RET_DK, RET_DV), F32)
            r_p, rf, rb = _retention(p3(rq), p3(rk), p3(rv), zst, zst, ret_decay[i], ret_norm_g[i], p3(gates),
                                     PROMPT_ROWS_PER_STEP)
            new_state.append((ckv.reshape(BP, LP, KV_LORA),
                              kr[:, MLA_NOPE:MLA_QK].reshape(BP, LP, MLA_ROPE), rf, rb))
            pend_p = (mod, [(flat(a_p), parts_w[0]), (flat(r_p), parts_w[1])])

            kr_c = jnp.pad(krope_c.reshape(BS * P, MLA_ROPE), ((0, 0), (MLA_NOPE, LANES - MLA_QK)))
            outs, y_s = take_stream(_prep_ab(y_s, mod, norm_g[l], p, True, LS, None, tables, tm,
                                             (ckv_c.reshape(BS * P, KV_LORA), kr_c), res=pend_s), pend_s, y_s)
            q, k, v, gates, rq, rk, rv = outs
            a_s = _mla_attn(s3(q), s3(k), s3(v), s3(gates), 1, MLA_Q_BLOCK, MLA_PAIRS_PER_STEP)
            r_s, _, _ = _retention(s3(rq), s3(rk), s3(rv), sf, sb, ret_decay[i], ret_norm_g[i], s3(gates), 1)
            pend_s = (mod, [(flat(a_s), parts_w[0]), (flat(r_s), parts_w[1])])
        else:
            p = {'w_in': win_w_in[i], 'q_head_g': win_q_head_g[i], 'k_head_g': win_k_head_g[i]}
            w_out = win_w_out[i].astype(BF16)
            sink = win_sink[i]
            kc, vc = ctx_caches[l]

            outs, y_p = take_stream(_prep_win(y_p, mod, norm_g[l], p, False, LP, ctx_row, None, tm, res=pend_p),
                                    pend_p, y_p)
            q, k, v, gates, kst, vst = outs
            o_p = _win_attn(p3(q), p3(k), p3(v), None, p3(gates), sink, PROMPT_ROWS_PER_STEP, LP,
                            WIN_KV_PER_STEP_PROMPT)
            new_state.append((kst.reshape(BP, LP, WIN_KV_HEADS, WIN_HEAD_DIM),
                              vst.reshape(BP, LP, WIN_KV_HEADS, WIN_HEAD_DIM)))
            pend_p = (mod, [(flat(o_p), w_out)])

            outs, y_s = take_stream(_prep_win(y_s, mod, norm_g[l], p, True, LS, None, tables, tm, res=pend_s),
                                    pend_s, y_s)
            q, k, v, gates = outs

            def slots(a, fill):
                a = a.astype(BF16)
                f = jnp.broadcast_to(fill.astype(BF16), a.shape)
                return jnp.concatenate([a, f, f, a], axis=-1).reshape(BS, P, 4 * 2 * LANES)

            zero = jnp.zeros((WIN_HEAD_DIM,), F32)
            o_s = _win_attn(s3(q), slots(kc, zero), slots(vc, zero.at[0].set(1.0)), (s3(k), s3(v)), s3(gates),
                            sink, 1, WIN_Q_BLOCK, WIN_KV_PER_STEP)
            pend_s = (mod, [(flat(o_s), w_out)])

    y_p = _out_proj(y_p, pend_p[0], pend_p[1], LP, ctx_row, tm)
    y_s = _out_proj(y_s, pend_s[0], pend_s[1], LS, None, tm)

    (l0_ckv, l0_krope, l0_rf, l0_rb), (l1_k, l1_v), (l2_ckv, l2_krope, l2_rf, l2_rb), (l3_k, l3_v) = new_state
    return (y_p.reshape(BP, LP, D), y_s.reshape(BS, LS, D), l0_ckv, l0_krope, l0_rf, l0_rb, l1_k, l1_v,
            l2_ckv, l2_krope, l2_rf, l2_rb, l3_k, l3_v)
```

```python
import functools

import numpy as np
import jax
import jax.numpy as jnp
from jax import lax
from jax.experimental import pallas as pl
from jax.experimental.pallas import tpu as pltpu

D_MODEL = 1024
DEPTH = 4
GRID_W = 64
ROPE_BASE = 10000.0
EPS = 1e-6

MLA_HEADS = 8
MLA_NOPE = 64
MLA_ROPE = 32
MLA_QK = MLA_NOPE + MLA_ROPE
MLA_V = 64
Q_LORA = 384
KV_LORA = 256
MLA_WIDTH = MLA_HEADS * MLA_V

RET_HEADS = 4
RET_DK = 64
RET_DV = 128
RET_CHUNK = 256
RET_WIDTH = RET_HEADS * RET_DV

WIN_HEADS = 16
WIN_KV_HEADS = 4
WIN_HEAD_DIM = 64
WINDOW = 128
WIN_WIDTH = WIN_HEADS * WIN_HEAD_DIM

LANES = 128
F32 = jnp.float32
BF16 = jnp.bfloat16
NEG = -1e30
LOG2E = 1.4426950408889634
VMEM_LIMIT = 52 * 1024 * 1024

TOKEN_BLOCK = 512
PROMPT_ROWS_PER_STEP = 8
MLA_Q_BLOCK = 512
MLA_PAIRS_PER_STEP = 2
MLA_KEY_CHUNK = 512
WIN_Q_BLOCK = 256
WIN_KV_PER_STEP = 4
WIN_KV_PER_STEP_PROMPT = 2

AB_OFF = dict(cq=0, ckv=384, krope=640, ga=672, rq=1184, rk=1440, rv=1696, gb=2208)
WIN_OFF = dict(q=0, k=1024, v=1280, g=1536)


def _offsets(widths):
    out, at = {}, 0
    for name, w in widths:
        out[name] = at
        at += w
    out['end'] = at
    return out


AB_SEGMENTS = (('cq', Q_LORA), ('ckv', KV_LORA), ('kr', LANES), ('g', MLA_WIDTH + RET_WIDTH),
               ('rq', RET_HEADS * RET_DK), ('rk', RET_HEADS * RET_DK), ('rv', RET_WIDTH))
AB_LAYOUT = _offsets(AB_SEGMENTS)


def _params(n_axes):
    return pltpu.CompilerParams(dimension_semantics=("arbitrary",) * n_axes, vmem_limit_bytes=VMEM_LIMIT)


def _const_spec(shape):
    nd = len(shape)
    return pl.BlockSpec(shape, lambda *_: (0,) * nd)


def _swap_idx(dim):
    d2 = dim // 2
    half = d2 // 2
    one = np.concatenate([np.arange(half, d2), np.arange(0, half)])
    return np.concatenate([one, d2 + one])


def _take_cols(w, cols):
    w = w.astype(BF16)
    pieces, i = [], 0
    while i < len(cols):
        j = i + 1
        if cols[i] < 0:
            while j < len(cols) and cols[j] < 0:
                j += 1
            pieces.append(jnp.zeros((w.shape[0], j - i), BF16))
        else:
            while j < len(cols) and cols[j] == cols[j - 1] + 1:
                j += 1
            pieces.append(w[:, cols[i]:cols[i] + j - i])
        i = j
    return jnp.concatenate(pieces, axis=1)


def _rope_tables(n_tokens, dim):
    n_rows = n_tokens // GRID_W
    half = dim // 4
    inv = jnp.power(jnp.float32(ROPE_BASE), -jnp.arange(half, dtype=F32) / half)

    def one(n_pos):
        ang = jnp.arange(n_pos, dtype=F32)[:, None] * inv[None, :]
        c, s = jnp.cos(ang), jnp.sin(ang)
        return jnp.concatenate([c, c], -1), jnp.concatenate([-s, s], -1)

    cr, sr = [jnp.repeat(a, GRID_W, axis=0) for a in one(n_rows)]
    cc, sc = [jnp.tile(a, (n_rows, 1)) for a in one(GRID_W)]
    return jnp.concatenate([cr, cc], -1), jnp.concatenate([sr, sc], -1)


def _ada_kernel(cond_ref, w_ref, b_ref, o_ref):
    c = cond_ref[...]
    sc = (c * jax.nn.sigmoid(c)).astype(BF16)
    o_ref[0] = jnp.dot(sc, w_ref[0].astype(BF16), preferred_element_type=F32) + b_ref[0]


def _ada_mod(cond, ada_w, ada_b):
    tn = 768
    n3 = 3 * D_MODEL
    return pl.pallas_call(
        _ada_kernel,
        grid=(DEPTH, n3 // tn),
        in_specs=[pl.BlockSpec((8, D_MODEL), lambda l, j: (0, 0)),
                  pl.BlockSpec((1, D_MODEL, tn), lambda l, j: (l, 0, j)),
                  pl.BlockSpec((1, 1, tn), lambda l, j: (l, 0, j))],
        out_specs=pl.BlockSpec((1, 8, tn), lambda l, j: (l, 0, j)),
        out_shape=jax.ShapeDtypeStruct((DEPTH, 8, n3), F32),
        compiler_params=_params(2),
        name="ada_mod",
    )(cond, ada_w, ada_b.reshape(DEPTH, 1, n3))


def _rms(x, g):
    return x * lax.rsqrt(jnp.mean(x * x, axis=-1, keepdims=True) + EPS) * g


def _modulated(x, mod_ref, ng_ref):
    shift = mod_ref[0, :, 0:D_MODEL]
    scale = mod_ref[0, :, D_MODEL:2 * D_MODEL]
    return (_rms(x, ng_ref[...]) * (1.0 + scale) + shift).astype(BF16)


def _block_input(x_ref, res_refs, y_out):
    x = x_ref[...]
    if not res_refs:
        return x
    upd = None
    for a_ref, w_ref in zip(res_refs[1::2], res_refs[2::2]):
        t = jnp.dot(a_ref[...], w_ref[...], preferred_element_type=F32)
        upd = t if upd is None else upd + t
    x = x + res_refs[0][0, :, 2 * D_MODEL:3 * D_MODEL] * upd
    y_out[...] = x
    return x


def _seg_rsqrt(x, seg_ref, n_real):
    ssq = jnp.dot((x * x).astype(BF16), seg_ref[...], preferred_element_type=F32)
    return lax.rsqrt(ssq * (1.0 / n_real) + EPS)


def _half_rsqrt(x, n_real):
    lo = _low_half(x.shape[0])
    cols = []
    for c in range(x.shape[1] // LANES):
        sq = x[:, LANES * c:LANES * (c + 1)]
        sq = sq * sq
        cols.append(jnp.where(lo, jnp.sum(jnp.where(lo, sq, 0.0), axis=-1, keepdims=True),
                              jnp.sum(jnp.where(lo, 0.0, sq), axis=-1, keepdims=True)))
    return lax.rsqrt(jnp.concatenate(cols, axis=-1) * (1.0 / n_real) + EPS)


def _silu(g):
    return g * jax.nn.sigmoid(g)


def _rope_partner(x, half):
    lane = lax.broadcasted_iota(jnp.int32, (x.shape[0], LANES), 1)
    first = lane % (2 * half) < half
    cols = []
    for c in range(x.shape[1] // LANES):
        xc = x[:, LANES * c:LANES * (c + 1)]
        cols.append(jnp.where(first, pltpu.roll(xc, LANES - half, 1), pltpu.roll(xc, half, 1)))
    return cols[0] if len(cols) == 1 else jnp.concatenate(cols, axis=-1)


def _low_half(rows):
    return lax.broadcasted_iota(jnp.int32, (rows, LANES), 1) < LANES // 2


def _mla_kv_slots(nat):
    lo = _low_half(nat.shape[0])
    k_cols, v_cols = [], []
    for h in range(MLA_HEADS):
        col = nat[:, LANES * h:LANES * (h + 1)]
        k_cols.append(jnp.where(lo, col, 0.0))
        v_cols.append(jnp.where(lo, pltpu.roll(col, LANES // 2, 1), 0.0) if h % 2 == 0
                      else jnp.where(lo, 0.0, col))
    return jnp.concatenate(k_cols, axis=-1), _with_ones_lane(jnp.concatenate(v_cols, axis=-1))


def _win_kv_slots(x):
    lo = _low_half(x.shape[0])
    cols = []
    for c in range(x.shape[1] // LANES):
        xc = x[:, LANES * c:LANES * (c + 1)]
        xr = pltpu.roll(xc, LANES // 2, 1)
        cols += [jnp.where(lo, xc, 0.0), jnp.where(lo, 0.0, xr), jnp.where(lo, xr, 0.0), jnp.where(lo, 0.0, xc)]
    return jnp.concatenate(cols, axis=-1)


def _ones_lane(head):
    return MLA_V if head % 2 == 0 else 0


def _with_ones_lane(v):
    lane = lax.broadcasted_iota(jnp.int32, v.shape, 1)
    odd = (lane // LANES) % 2 == 1
    return jnp.where(lane % LANES == jnp.where(odd, _ones_lane(1), _ones_lane(0)), 1.0, v)


def _ctx_kv(ckv_ref, kr_ref, wkv_ref, seg_ref, kg_ref, k_out, v_out):
    kn, v = _mla_kv_slots(jnp.dot(ckv_ref[...].astype(BF16), wkv_ref[...], preferred_element_type=F32))
    v_out[...] = v.astype(BF16)
    kr = kr_ref[...]
    kg = kg_ref[...]
    kr2 = jnp.concatenate([kr, kr], axis=-1)
    kg2 = jnp.concatenate([kg, kg], axis=-1)
    for g in range(MLA_HEADS // 2):
        sl = slice(2 * LANES * g, 2 * LANES * (g + 1))
        kraw = kn[:, sl] + kr2
        k_out[:, sl] = (kraw * kg2 * _seg_rsqrt(kraw, seg_ref, MLA_QK)).astype(BF16)


def _prep_ab_kernel(rope, ctx_blocks, steps_per_batch, n_res, *refs):
    n_in = 17 if rope else 11
    n_ctx = 2 if ctx_blocks else 0
    main_in = refs[:n_in]
    ctx_in = refs[n_in:n_in + n_ctx]
    res_in = refs[n_in + n_ctx:n_in + n_ctx + n_res]
    outs = refs[n_in + n_ctx + n_res:]
    main_out, y_out = (outs[:-1], outs[-1]) if n_res else (outs, None)

    def own():
        x = _block_input(main_in[0], res_in, y_out)
        _prep_ab_body(rope, x, *main_in[1:], *main_out)

    if not ctx_blocks:
        own()
        return
    wkv_ref, seg_ref, kg_ref = main_in[7], main_in[8], main_in[11]
    is_ctx = pl.program_id(0) % steps_per_batch < ctx_blocks

    @pl.when(is_ctx)
    def _():
        _ctx_kv(*ctx_in, wkv_ref, seg_ref, kg_ref, main_out[1], main_out[2])

    pl.when(jnp.logical_not(is_ctx))(own)


def _prep_ab_body(rope, x, *refs):
    if rope:
        (mod_ref, ng_ref, win_ref, qng_ref, kvng_ref, wq_ref, wkv_ref, seg_ref,
         qg_ref, qgs_ref, kg_ref, kgs_ref, cm_ref, sm_ref, cr_ref, sr_ref,
         q_out, k_out, v_out, gate_out, rq_out, rk_out, rv_out) = refs
    else:
        (mod_ref, ng_ref, win_ref, qng_ref, kvng_ref, wq_ref, wkv_ref, seg_ref,
         qg_ref, kg_ref,
         q_out, k_out, v_out, gate_out, rq_out, rk_out, rv_out, ckv_out, kr_out) = refs
    o_kr, o_g, o_rq, o_rk, o_rv = AB_LAYOUT['kr'], AB_LAYOUT['g'], AB_LAYOUT['rq'], AB_LAYOUT['rk'], AB_LAYOUT['rv']

    h = _modulated(x, mod_ref, ng_ref)
    proj = jnp.dot(h, win_ref[...], preferred_element_type=F32)

    cqn = _rms(proj[:, 0:Q_LORA], qng_ref[...]).astype(BF16)
    ckvn_f = _rms(proj[:, Q_LORA:Q_LORA + KV_LORA], kvng_ref[...])
    ckvn = ckvn_f.astype(BF16)
    kr = proj[:, o_kr:o_kr + LANES]

    gate_out[...] = _silu(proj[:, o_g:o_g + 2 * MLA_WIDTH]).astype(BF16)
    rv_out[...] = proj[:, o_rv:o_rv + RET_WIDTH].astype(BF16)

    qm = jnp.dot(cqn, wq_ref[...], preferred_element_type=F32)
    kn, v = _mla_kv_slots(jnp.dot(ckvn, wkv_ref[...], preferred_element_type=F32))
    v_out[...] = v.astype(BF16)

    qscale = MLA_QK ** -0.5 * LOG2E
    if rope:
        krs = _rope_partner(kr, MLA_ROPE // 4)
        cm, sm = cm_ref[...], sm_ref[...]
        q_c = cm * qg_ref[...]
        q_s = sm * qgs_ref[...]
        kr_rot = kr * (cm * kg_ref[...]) + krs * (sm * kgs_ref[...])
        cr, sr = cr_ref[...], sr_ref[...]
        rq = proj[:, o_rq:o_rq + 256]
        rk = proj[:, o_rk:o_rk + 256]
        rqs = _rope_partner(rq, RET_DK // 4)
        rks = _rope_partner(rk, RET_DK // 4)
        for c in range(2):
            sl = slice(LANES * c, LANES * (c + 1))
            rq_out[:, sl] = ((rq[:, sl] * cr + rqs[:, sl] * sr) * (RET_DK ** -0.5)).astype(BF16)
            rk_out[:, sl] = (rk[:, sl] * cr + rks[:, sl] * sr).astype(BF16)
    else:
        q_c = jnp.broadcast_to(qg_ref[...], (qm.shape[0], LANES))
        kr_rot = kr * kg_ref[...]
        rq_out[...] = (proj[:, o_rq:o_rq + 256] * (RET_DK ** -0.5)).astype(BF16)
        rk_out[...] = proj[:, o_rk:o_rk + 256].astype(BF16)
        ckv_out[...] = ckvn_f
        kr_out[...] = kr

    kg = kg_ref[...]
    q_c2 = jnp.concatenate([q_c, q_c], axis=-1)
    kr2 = jnp.concatenate([kr, kr], axis=-1)
    kr_rot2 = jnp.concatenate([kr_rot, kr_rot], axis=-1)
    kg2 = jnp.concatenate([kg, kg], axis=-1)
    if rope:
        q_s2 = jnp.concatenate([q_s, q_s], axis=-1)
    for g in range(MLA_HEADS // 2):
        sl = slice(2 * LANES * g, 2 * LANES * (g + 1))
        qg = qm[:, sl]
        rq_n = _seg_rsqrt(qg, seg_ref, MLA_QK)
        qv = qg * q_c2
        if rope:
            qv = qv + _rope_partner(qg, MLA_ROPE // 4) * q_s2
        q_out[:, sl] = (qv * (rq_n * qscale)).astype(BF16)
        kgp = kn[:, sl]
        rk_n = _seg_rsqrt(kgp + kr2, seg_ref, MLA_QK)
        k_out[:, sl] = ((kgp * kg2 + kr_rot2) * rk_n).astype(BF16)


def _ab_weights(w_in, w_uq, w_ukv, q_head_g, k_head_g, rope):
    sw32 = _swap_idx(MLA_ROPE)
    z = lambda n: [-1] * n
    o = AB_OFF
    run = lambda start, n: list(range(start, start + n))
    source = dict(cq=run(o['cq'], Q_LORA), ckv=run(o['ckv'], KV_LORA),
                  kr=z(MLA_NOPE) + run(o['krope'], MLA_ROPE) + z(LANES - MLA_QK),
                  g=run(o['ga'], MLA_WIDTH) + run(o['gb'], RET_WIDTH),
                  rq=run(o['rq'], RET_HEADS * RET_DK), rk=run(o['rk'], RET_HEADS * RET_DK),
                  rv=run(o['rv'], RET_WIDTH))
    cols = []
    for name, width in AB_SEGMENTS:
        assert len(source[name]) == width and len(cols) == AB_LAYOUT[name]
        cols += source[name]
    win = _take_cols(w_in, cols).astype(BF16)

    qc = []
    for hh in range(MLA_HEADS):
        qc += list(range(MLA_QK * hh, MLA_QK * (hh + 1))) + z(LANES - MLA_QK)
    wq = _take_cols(w_uq, qc).astype(BF16)
    wkv = w_ukv.astype(BF16)

    pad = lambda g: jnp.concatenate([g, jnp.zeros((LANES - MLA_QK,), F32)])[None, :]
    gsw = lambda g: jnp.concatenate([jnp.zeros((MLA_NOPE,), F32), g[MLA_NOPE + sw32],
                                     jnp.zeros((LANES - MLA_QK,), F32)])[None, :]
    qg, kg = pad(q_head_g), pad(k_head_g)
    qgs, kgs = (gsw(q_head_g), gsw(k_head_g)) if rope else (None, None)
    return win, wq, wkv, qg, qgs, kg, kgs


def _seg_matrix(width):
    idx = np.arange(2 * LANES) // width
    return jnp.asarray((idx[:, None] == idx[None, :]).astype(np.float32), BF16)


def _prep_ab(x, mod, norm_g, p, rope, tokens_per_batch, mod_row, tables, tm, ctx=None, res=None):
    T = x.shape[0]
    win, wq, wkv, qg, qgs, kg, kgs = _ab_weights(
        p['w_in'], p['w_uq'], p['w_ukv'], p['q_head_g'], p['k_head_g'], rope)
    seg = _seg_matrix(LANES)
    blocks_per_batch = max(tokens_per_batch // tm, 1)
    ctx_blocks = 0
    if ctx is not None:
        n_batch = T // tokens_per_batch
        ctx_rows = ctx[0].shape[0] // n_batch
        assert rope and ctx_rows % tm == 0 and tokens_per_batch % tm == 0
        ctx_blocks = ctx_rows // tm
    steps_per_batch = blocks_per_batch + ctx_blocks
    batch = lambda i: i // steps_per_batch
    in_batch = lambda i: jnp.maximum(i % steps_per_batch - ctx_blocks, 0)
    own = lambda i: batch(i) * blocks_per_batch + in_batch(i)
    tok = lambda w: pl.BlockSpec((tm, w), lambda i: (own(i), 0))
    if mod_row is None:
        mod_spec = pl.BlockSpec((1, 1, 3 * D_MODEL), lambda i: (batch(i), 0, 0))
    else:
        mod_spec = pl.BlockSpec((1, 1, 3 * D_MODEL), lambda i: (mod_row, 0, 0))
    vec = lambda a: (a, _const_spec(a.shape))
    ins = [(x, tok(D_MODEL)), (mod, mod_spec), vec(norm_g[None, :]), vec(win),
           vec(p['q_norm_g'][None, :]), vec(p['kv_norm_g'][None, :]), vec(wq),
           vec(wkv), vec(seg), vec(qg)]
    if rope:
        ins.append(vec(qgs))
    ins.append(vec(kg))
    if rope:
        ins.append(vec(kgs))
        tab = lambda a: (a, pl.BlockSpec((tm, LANES), lambda i: (in_batch(i), 0)))
        ins += [tab(tables['cm']), tab(tables['sm']), tab(tables['c64']), tab(tables['s64'])]
    kv_rows, kv_spec = T, tok(1024)
    if ctx_blocks:
        cspec = lambda w: pl.BlockSpec((tm, w), lambda i: (
            batch(i) * ctx_blocks + jnp.minimum(i % steps_per_batch, ctx_blocks - 1), 0))
        ins += [(ctx[0], cspec(KV_LORA)), (ctx[1], cspec(LANES))]
        kv_rows = T + ctx[0].shape[0]
        kv_spec = pl.BlockSpec((tm, 1024), lambda i: (i, 0))
    outs = [(T, 1024, BF16, tok(1024)), (kv_rows, 1024, BF16, kv_spec), (kv_rows, 1024, BF16, kv_spec),
            (T, 1024, BF16, tok(1024)), (T, 256, BF16, tok(256)), (T, 256, BF16, tok(256)),
            (T, 512, BF16, tok(512))]
    if not rope:
        outs += [(T, KV_LORA, F32, tok(KV_LORA)), (T, LANES, F32, tok(LANES))]
    n_res = 0
    if res is not None:
        ins.append((res[0], mod_spec))
        for a, w in res[1]:
            ins += [(a, tok(a.shape[1])), vec(w)]
        n_res = 1 + 2 * len(res[1])
        outs.append((T, D_MODEL, F32, tok(D_MODEL)))
    return pl.pallas_call(
        functools.partial(_prep_ab_kernel, rope, ctx_blocks, steps_per_batch, n_res),
        grid=(kv_rows // tm,),
        in_specs=[s for _, s in ins],
        out_specs=[o[3] for o in outs],
        out_shape=[jax.ShapeDtypeStruct(o[:2], o[2]) for o in outs],
        compiler_params=_params(1),
        name="prep_ab_rope" if rope else "prep_ab",
    )(*[a for a, _ in ins])


def _lane_groups(x, op):
    out = x[:, 0:LANES]
    for t in range(1, x.shape[1] // LANES):
        out = op(out, x[:, LANES * t:LANES * (t + 1)])
    return out


def _mla_attn_kernel(tk, q_ref, k_ref, v_ref, g_ref, o_ref, s_ref):
    dn = (((1,), (1,)), ((), ()))
    tq = q_ref.shape[1]
    nk = k_ref.shape[1] // tk
    heads = [(b, j) for b in range(q_ref.shape[0]) for j in range(q_ref.shape[2] // LANES)]
    m_prev, out = None, None
    for idx in range(len(heads) + 1):
        cur = heads[idx] if idx < len(heads) else None
        prev = heads[idx - 1] if idx > 0 else None
        if cur is not None:
            cb, cj = cur
            csl = slice(LANES * cj, LANES * (cj + 1))
            q = q_ref[cb, :, csl]
            mt = None
        if prev is not None:
            pb, pj = prev
            psl = slice(LANES * pj, LANES * (pj + 1))
            acc = jnp.zeros((tq, LANES), F32)
        for c in range(nk):
            rows = slice(c * tk, (c + 1) * tk)
            if cur is not None:
                s = lax.dot_general(q, k_ref[cb, rows, csl], dn, preferred_element_type=F32)
                s_ref[idx % 2, :, rows] = s
                smax = _lane_groups(s, jnp.maximum)
                mt = smax if mt is None else jnp.maximum(mt, smax)
            if prev is not None:
                p = jnp.exp2(s_ref[(idx - 1) % 2, :, rows] - m_prev)
                acc = acc + jnp.dot(p.astype(BF16), v_ref[pb, rows, psl], preferred_element_type=F32)
        if prev is not None:
            lane = lax.broadcasted_iota(jnp.int32, (tq, LANES), 1)
            l = jnp.sum(jnp.where(lane == _ones_lane(pj), acc, 0.0), axis=-1, keepdims=True)
            oj = jnp.where((lane >= MLA_V) == (pj % 2 == 1), acc, 0.0) / l
            out = oj if out is None else out + oj
            if pj % 2 == 1:
                osl = slice(LANES * (pj // 2), LANES * (pj // 2 + 1))
                o_ref[pb, :, osl] = (out * g_ref[pb, :, osl].astype(F32)).astype(BF16)
                out = None
        if cur is not None:
            m_prev = jnp.max(mt, axis=-1, keepdims=True)


def _mla_attn(q, k, v, gates, bb, tq, pairs):
    B, L, _ = q.shape
    Lk = k.shape[1]
    hp = MLA_HEADS // 2 // pairs
    qspec = pl.BlockSpec((bb, tq, 2 * LANES * pairs), lambda b, h, i: (b, i, h))
    kspec = pl.BlockSpec((bb, Lk, 2 * LANES * pairs), lambda b, h, i: (b, 0, h))
    ospec = pl.BlockSpec((bb, tq, LANES * pairs), lambda b, h, i: (b, i, h))
    tk = min(Lk, MLA_KEY_CHUNK)
    assert Lk % tk == 0
    return pl.pallas_call(
        functools.partial(_mla_attn_kernel, tk),
        grid=(B // bb, hp, L // tq),
        in_specs=[qspec, kspec, kspec, ospec],
        out_specs=ospec,
        out_shape=jax.ShapeDtypeStruct((B, L, MLA_WIDTH), BF16),
        scratch_shapes=[pltpu.VMEM((2, tq, Lk), F32)],
        compiler_params=_params(3),
        name="mla_attn",
    )(q, k, v, gates)


def _ret_kernel(n_chunks, decay_ref, q_ref, k_ref, v_ref, sf_ref, sb_ref, gn_ref, gate_ref,
                o_ref, rf_ref, rb_ref, st_ref, u_ref, r_ref, dmask_ref, qw_ref, kwt_ref, sdec_ref):
    C = RET_CHUNK
    pair = pl.program_id(0)
    bb = q_ref.shape[0]
    lane_hi = lax.broadcasted_iota(jnp.int32, (C, LANES), 1) >= RET_DK

    @pl.when(pl.program_id(1) == 0)
    def _():
        rel = (lax.broadcasted_iota(jnp.int32, (C, C), 0) - lax.broadcasted_iota(jnp.int32, (C, C), 1)).astype(F32)
        qrow = lax.broadcasted_iota(jnp.int32, (C, LANES), 0).astype(F32)
        kcol = lax.broadcasted_iota(jnp.int32, (LANES, C), 1).astype(F32)
        krow_hi = lax.broadcasted_iota(jnp.int32, (LANES, C), 0) >= RET_DK
        srow_hi = lax.broadcasted_iota(jnp.int32, (LANES, 2 * RET_DV), 0) >= RET_DK
        scol_hi = lax.broadcasted_iota(jnp.int32, (LANES, 2 * RET_DV), 1) >= RET_DV
        same_head = srow_hi == scol_hi
        sdec_ref[2] = jnp.where(same_head, 1.0, 0.0)
        log_decay = lambda shape, d, hh: -jnp.exp(jnp.full(shape, decay_ref[d, 2 * pair + hh], F32))
        for hh in range(2):
            fwd = jnp.where(rel >= 0, jnp.exp(jnp.maximum(rel, 0.0) * log_decay((C, C), 0, hh)), 0.0)
            bwd = jnp.where(rel <= 0, jnp.exp(jnp.maximum(-rel, 0.0) * log_decay((C, C), 1, hh)), 0.0)
            dmask_ref[hh] = fwd + bwd
        for d in range(2):
            lg_q = jnp.where(lane_hi, log_decay((C, LANES), d, 1), log_decay((C, LANES), d, 0))
            lg_k = jnp.where(krow_hi, log_decay((LANES, C), d, 1), log_decay((LANES, C), d, 0))
            lg_s = jnp.where(srow_hi, log_decay((LANES, 2 * RET_DV), d, 1), log_decay((LANES, 2 * RET_DV), d, 0))
            if d == 0:
                qw_ref[d] = jnp.exp((qrow + 1.0) * lg_q)
                kwt_ref[d] = jnp.exp((C - 1.0 - kcol) * lg_k)
            else:
                qw_ref[d] = jnp.exp((C - qrow) * lg_q)
                kwt_ref[d] = jnp.exp(kcol * lg_k)
            sdec_ref[d] = jnp.where(same_head, jnp.exp(C * lg_s), 0.0)

    def updates(n):
        r0 = n * C
        for b in range(bb):
            kt = k_ref[b, pl.ds(r0, C), :].astype(F32).T
            vc = v_ref[b, pl.ds(r0, C), :]
            for d in range(2):
                u_ref[b, n, d] = jnp.dot((kt * kwt_ref[d]).astype(BF16), vc, preferred_element_type=F32)

    for n in range(n_chunks):
        updates(n)

    for b in range(bb):
        for d, s_ref in ((0, sf_ref), (1, sb_ref)):
            st_ref[b, d] = jnp.zeros((LANES, 2 * RET_DV), F32)
            for hh in range(2):
                st_ref[b, d, RET_DK * hh:RET_DK * (hh + 1), RET_DV * hh:RET_DV * (hh + 1)] = s_ref[b, hh]

    def scan(n, _):
        for b in range(bb):
            for d in range(2):
                c = n if d == 0 else n_chunks - 1 - n
                state = st_ref[b, d]
                r_ref[b, c, LANES * d:LANES * (d + 1), :] = state.astype(BF16)
                st_ref[b, d] = sdec_ref[d] * state + u_ref[b, c, d] * sdec_ref[2]
        return 0

    lax.fori_loop(0, n_chunks, scan, 0)

    for b in range(bb):
        for hh in range(2):
            rf_ref[b, hh] = st_ref[b, 0, RET_DK * hh:RET_DK * (hh + 1), RET_DV * hh:RET_DV * (hh + 1)]
            rb_ref[b, hh] = st_ref[b, 1, RET_DK * hh:RET_DK * (hh + 1), RET_DV * hh:RET_DV * (hh + 1)]

    dn = (((1,), (1,)), ((), ()))
    ones = jnp.ones((RET_DV, RET_DV), BF16)

    def outputs(n):
        r0 = n * C
        for b in range(bb):
            qc = q_ref[b, pl.ds(r0, C), :].astype(F32)
            kc = k_ref[b, pl.ds(r0, C), :]
            vc = v_ref[b, pl.ds(r0, C), :]
            q2 = jnp.concatenate([jnp.where(lane_hi, 0.0, qc), jnp.where(lane_hi, qc, 0.0)], axis=0)
            s2 = lax.dot_general(q2.astype(BF16), kc, dn, preferred_element_type=F32)
            qw = jnp.concatenate([qc * qw_ref[0], qc * qw_ref[1]], axis=-1).astype(BF16)
            inter = jnp.dot(qw, r_ref[b, n], preferred_element_type=F32)
            for hh in range(2):
                sl = slice(RET_DV * hh, RET_DV * (hh + 1))
                y = jnp.dot((s2[C * hh:C * (hh + 1)] * dmask_ref[hh]).astype(BF16), vc[:, sl],
                            preferred_element_type=F32) + inter[:, sl]
                yc = y - jnp.mean(y, axis=-1, keepdims=True)
                var = jnp.dot((yc * yc).astype(BF16), ones, preferred_element_type=F32) * (1.0 / RET_DV)
                out = yc * lax.rsqrt(var + EPS) * gn_ref[:, sl]
                o_ref[b, pl.ds(r0, C), sl] = (out * gate_ref[b, pl.ds(r0, C), sl].astype(F32)).astype(BF16)

    for n in range(n_chunks):
        outputs(n)


def _retention(rq, rk, rv, sf, sb, decay, ret_norm_g, gates, bb):
    B, L, _ = rq.shape
    C = RET_CHUNK
    n_chunks = L // C
    hp = RET_HEADS // 2
    st_spec = pl.BlockSpec((bb, 2, RET_DK, RET_DV), lambda h, b: (b, h, 0, 0))
    st_shape = jax.ShapeDtypeStruct((B, RET_HEADS, RET_DK, RET_DV), F32)
    qk_spec = pl.BlockSpec((bb, L, LANES), lambda h, b: (b, 0, h))
    v_spec = pl.BlockSpec((bb, L, 2 * RET_DV), lambda h, b: (b, 0, h))
    return pl.pallas_call(
        functools.partial(_ret_kernel, n_chunks),
        grid=(hp, B // bb),
        in_specs=[pl.BlockSpec(memory_space=pltpu.SMEM), qk_spec, qk_spec, v_spec, st_spec, st_spec,
                  pl.BlockSpec((1, 2 * RET_DV), lambda h, b: (0, h)),
                  pl.BlockSpec((bb, L, 2 * RET_DV), lambda h, b: (b, 0, hp + h))],
        out_specs=[v_spec, st_spec, st_spec],
        out_shape=[jax.ShapeDtypeStruct((B, L, RET_WIDTH), BF16), st_shape, st_shape],
        scratch_shapes=[pltpu.VMEM((bb, 2, LANES, 2 * RET_DV), F32),
                        pltpu.VMEM((bb, n_chunks, 2, LANES, 2 * RET_DV), F32),
                        pltpu.VMEM((bb, n_chunks, 2 * LANES, 2 * RET_DV), BF16),
                        pltpu.VMEM((2, C, C), F32),
                        pltpu.VMEM((2, C, LANES), F32),
                        pltpu.VMEM((2, LANES, C), F32),
                        pltpu.VMEM((3, LANES, 2 * RET_DV), F32)],
        compiler_params=_params(2),
        name="retention",
    )(decay, rq, rk, rv, sf, sb, ret_norm_g[None, :], gates)


def _prep_win_kernel(rope, n_res, *refs):
    n_in = 11 if rope else 7
    res_in = refs[n_in:n_in + n_res]
    outs = refs[n_in + n_res:]
    main_out, y_out = (outs[:-1], outs[-1]) if n_res else (outs, None)
    if rope:
        x_ref, mod_ref, ng_ref, win_ref, seg_ref, qg_ref, qgs_ref, kg_ref, kgs_ref, c_ref, s_ref = refs[:n_in]
        q_out, k_out, v_out, gate_out = main_out
    else:
        x_ref, mod_ref, ng_ref, win_ref, seg_ref, qg_ref, kg_ref = refs[:n_in]
        q_out, k_out, v_out, gate_out, kst_out, vst_out = main_out
    o = WIN_OFF
    h = _modulated(_block_input(x_ref, res_in, y_out), mod_ref, ng_ref)
    proj = jnp.dot(h, win_ref[...], preferred_element_type=F32)
    gate_out[...] = _silu(proj[:, o['g']:o['g'] + WIN_WIDTH]).astype(BF16)

    qg2 = jnp.concatenate([qg_ref[...]] * 2, axis=-1)
    kg2 = jnp.concatenate([kg_ref[...]] * 2, axis=-1)
    if rope:
        c2 = jnp.concatenate([c_ref[...]] * 2, axis=-1)
        s2 = jnp.concatenate([s_ref[...]] * 2, axis=-1)
        q_c, q_s = c2 * qg2, s2 * jnp.concatenate([qgs_ref[...]] * 2, axis=-1)
        k_c, k_s = c2 * kg2, s2 * jnp.concatenate([kgs_ref[...]] * 2, axis=-1)
    qscale = WIN_HEAD_DIM ** -0.5 * LOG2E
    if rope:
        head_rsqrt = lambda t: _seg_rsqrt(t, seg_ref, WIN_HEAD_DIM)
    else:
        head_rsqrt = lambda t: _half_rsqrt(t, WIN_HEAD_DIM)
    for g in range(WIN_WIDTH // 256):
        sl = slice(256 * g, 256 * (g + 1))
        qg = proj[:, sl]
        rn = head_rsqrt(qg) * qscale
        if rope:
            qv = qg * q_c + _rope_partner(qg, WIN_HEAD_DIM // 4) * q_s
        else:
            qv = qg * qg2
        q_out[:, sl] = (qv * rn).astype(BF16)
    kraw = proj[:, o['k']:o['k'] + 256]
    rn = head_rsqrt(kraw)
    if rope:
        kn = (kraw * k_c + _rope_partner(kraw, WIN_HEAD_DIM // 4) * k_s) * rn
    else:
        kn = kraw * kg2 * rn
    v = proj[:, o['v']:o['v'] + 256]
    if not rope:
        kst_out[...] = kn
        vst_out[...] = v
    k_out[...] = _win_kv_slots(kn).astype(BF16)
    v_out[...] = _with_ones_lane(_win_kv_slots(v)).astype(BF16)


def _prep_win(x, mod, norm_g, p, rope, tokens_per_batch, mod_row, tables, tm, res=None):
    T = x.shape[0]
    sw64 = _swap_idx(WIN_HEAD_DIM)
    win = p['w_in'].astype(BF16)
    seg = _seg_matrix(WIN_HEAD_DIM)
    rep = lambda g: jnp.concatenate([g, g])[None, :]
    blocks_per_batch = tokens_per_batch // tm
    tok = lambda w: pl.BlockSpec((tm, w), lambda i: (i, 0))
    if mod_row is None:
        mod_spec = pl.BlockSpec((1, 1, 3 * D_MODEL), lambda i: (i // blocks_per_batch, 0, 0))
    else:
        mod_spec = pl.BlockSpec((1, 1, 3 * D_MODEL), lambda i: (mod_row, 0, 0))
    vec = lambda a: (a, _const_spec(a.shape))
    ins = [(x, tok(D_MODEL)), (mod, mod_spec), vec(norm_g[None, :]), vec(win), vec(seg),
           vec(rep(p['q_head_g']))]
    if rope:
        ins.append(vec(rep(p['q_head_g'][sw64])))
    ins.append(vec(rep(p['k_head_g'])))
    if rope:
        ins.append(vec(rep(p['k_head_g'][sw64])))
        tab = lambda a: (a, pl.BlockSpec((tm, LANES), lambda i: (i % blocks_per_batch, 0)))
        ins += [tab(tables['c64']), tab(tables['s64'])]
    outs = [(1024, BF16), (1024, BF16), (1024, BF16), (1024, BF16)]
    if not rope:
        outs += [(256, F32), (256, F32)]
    n_res = 0
    if res is not None:
        ins.append((res[0], mod_spec))
        for a, w in res[1]:
            ins += [(a, tok(a.shape[1])), vec(w)]
        n_res = 1 + 2 * len(res[1])
        outs.append((D_MODEL, F32))
    return pl.pallas_call(
        functools.partial(_prep_win_kernel, rope, n_res),
        grid=(T // tm,),
        in_specs=[s for _, s in ins],
        out_specs=[tok(w) for w, _ in outs],
        out_shape=[jax.ShapeDtypeStruct((T, w), dt) for w, dt in outs],
        compiler_params=_params(1),
        name="prep_win_rope" if rope else "prep_win",
    )(*[a for a, _ in ins])


def _win_attn_kernel(local, *refs):
    if local:
        sink_ref, q_ref, kf_ref, vf_ref, k_ref, v_ref, g_ref, bias_ref, o_ref, s_ref = refs
    else:
        sink_ref, q_ref, kf_ref, vf_ref, g_ref, o_ref, s_ref = refs
    tq = q_ref.shape[1]
    nf = kf_ref.shape[1]
    kv_per_step = q_ref.shape[2] // (2 * LANES)
    upper = lax.broadcasted_iota(jnp.int32, (2 * tq, 1), 0) >= tq
    lane = lax.broadcasted_iota(jnp.int32, (2 * tq, LANES), 1)
    if local:
        span = bias_ref.shape[2]
        start = jnp.clip(pl.program_id(2) * tq - WINDOW, 0, k_ref.shape[1] - span)
        start = pl.multiple_of(start, WINDOW)
    dn = (((1,), (1,)), ((), ()))
    units = [(b, jj, half) for b in range(q_ref.shape[0]) for jj in range(kv_per_step) for half in range(2)]
    m_prev, sk_prev, out = None, None, None
    for idx in range(len(units) + 1):
        cur = units[idx] if idx < len(units) else None
        prev = units[idx - 1] if idx > 0 else None
        if cur is not None:
            cb, cjj, chalf = cur
            base = 2 * LANES * cjj
            csl = slice(base + LANES * chalf, base + LANES * (chalf + 1))
            q2 = jnp.concatenate([q_ref[cb, :, base:base + LANES],
                                  q_ref[cb, :, base + LANES:base + 2 * LANES]], axis=0)
            head = 4 * (pl.program_id(1) * kv_per_step + cjj) + chalf
            sk = jnp.where(upper, sink_ref[head + 2], sink_ref[head]) * LOG2E
            s_ctx = lax.dot_general(q2, kf_ref[cb, :, csl], dn, preferred_element_type=F32)
            s_ref[idx % 2, :, 0:nf] = s_ctx
            mt = _lane_groups(s_ctx, jnp.maximum)
        if prev is not None:
            pb, pjj, phalf = prev
            pbase = 2 * LANES * pjj
            psl = slice(pbase + LANES * phalf, pbase + LANES * (phalf + 1))
            e_ctx = jnp.exp2(s_ref[(idx - 1) % 2, :, 0:nf] - m_prev)
            acc = jnp.dot(e_ctx.astype(BF16), vf_ref[pb, :, psl], preferred_element_type=F32)
        if local and cur is not None:
            s_loc = lax.dot_general(q2, k_ref[cb, pl.ds(start, span), csl], dn,
                                    preferred_element_type=F32) + bias_ref[0]
            s_ref[idx % 2, :, nf:nf + span] = s_loc
            mt = jnp.maximum(mt, _lane_groups(s_loc, jnp.maximum))
        if local and prev is not None:
            e_loc = jnp.exp2(s_ref[(idx - 1) % 2, :, nf:nf + span] - m_prev)
            acc = acc + jnp.dot(e_loc.astype(BF16), v_ref[pb, pl.ds(start, span), psl],
                                preferred_element_type=F32)
        if prev is not None:
            den = jnp.sum(jnp.where(lane == _ones_lane(phalf), acc, 0.0), axis=-1, keepdims=True)
            den = den + jnp.exp2(sk_prev - m_prev)
            oh = jnp.where((lane >= WIN_HEAD_DIM) == (phalf == 1), acc, 0.0) / den
            out = oh if out is None else out + oh
            if phalf == 1:
                gate = g_ref[pb, :, pbase:pbase + 2 * LANES].astype(F32)
                o_ref[pb, :, pbase:pbase + LANES] = (out[0:tq] * gate[:, 0:LANES]).astype(BF16)
                o_ref[pb, :, pbase + LANES:pbase + 2 * LANES] = (
                    out[tq:2 * tq] * gate[:, LANES:2 * LANES]).astype(BF16)
                out = None
        if cur is not None:
            m_prev = jnp.maximum(jnp.max(mt, axis=-1, keepdims=True), sk)
            sk_prev = sk


def _window_bias(tq):
    span = tq + 2 * WINDOW
    qi = jnp.arange(2 * tq)[:, None] % tq
    kk = jnp.arange(span)[None, :]
    cases = [jnp.abs(kk - off - qi) <= WINDOW for off in (0, WINDOW, 2 * WINDOW)]
    return jnp.where(jnp.stack(cases), 0.0, NEG).astype(F32)


def _win_attn(q, kf, vf, own, gates, sink, bb, tq, kv_per_step):
    B, L, _ = q.shape
    nb = L // tq
    width = 2 * LANES * kv_per_step
    qspec = pl.BlockSpec((bb, tq, width), lambda b, h, i: (b, i, h))
    fspec = pl.BlockSpec((bb, kf.shape[1], width), lambda b, h, i: (b, 0, h))
    in_specs = [pl.BlockSpec(memory_space=pltpu.SMEM), qspec, fspec, fspec]
    args = [sink, q, kf, vf]
    n_keys = kf.shape[1]
    if own is not None:
        assert nb >= 3 and bb == 1
        kspec = pl.BlockSpec((bb, L, width), lambda b, h, i: (b, 0, h), pipeline_mode=pl.Buffered(1))
        in_specs += [kspec, kspec]
        args += list(own)
    in_specs.append(qspec)
    args.append(gates)
    if own is not None:
        bias = _window_bias(tq)
        n_keys += bias.shape[2]
        in_specs.append(pl.BlockSpec(
            (1,) + bias.shape[1:],
            lambda b, h, i: (jnp.where(i == 0, 0, jnp.where(i == nb - 1, 2, 1)), 0, 0)))
        args.append(bias)
    return pl.pallas_call(
        functools.partial(_win_attn_kernel, own is not None),
        grid=(B // bb, WIN_KV_HEADS // kv_per_step, nb),
        in_specs=in_specs,
        out_specs=qspec,
        out_shape=jax.ShapeDtypeStruct((B, L, WIN_WIDTH), BF16),
        scratch_shapes=[pltpu.VMEM((2, 2 * tq, n_keys), F32)],
        compiler_params=_params(3),
        name="win_attn_local" if own is not None else "win_attn",
    )(*args)


def _out_kernel(n_parts, *refs):
    x_ref, mod_ref = refs[0], refs[1]
    parts = refs[2:2 + 2 * n_parts]
    o_ref = refs[2 + 2 * n_parts]
    y = None
    for i in range(n_parts):
        t = jnp.dot(parts[2 * i][...], parts[2 * i + 1][...], preferred_element_type=F32)
        y = t if y is None else y + t
    gate = mod_ref[0, :, 2 * D_MODEL:3 * D_MODEL]
    o_ref[...] = x_ref[...] + gate * y


def _out_proj(x, mod, parts, tokens_per_batch, mod_row, tm):
    T = x.shape[0]
    blocks_per_batch = tokens_per_batch // tm
    tok = lambda w: pl.BlockSpec((tm, w), lambda i: (i, 0))
    if mod_row is None:
        mod_spec = pl.BlockSpec((1, 1, 3 * D_MODEL), lambda i: (i // blocks_per_batch, 0, 0))
    else:
        mod_spec = pl.BlockSpec((1, 1, 3 * D_MODEL), lambda i: (mod_row, 0, 0))
    in_specs = [tok(D_MODEL), mod_spec]
    args = [x, mod]
    for a, w in parts:
        in_specs += [tok(a.shape[1]), _const_spec(w.shape)]
        args += [a, w]
    return pl.pallas_call(
        functools.partial(_out_kernel, len(parts)),
        grid=(T // tm,),
        in_specs=in_specs,
        out_specs=tok(D_MODEL),
        out_shape=jax.ShapeDtypeStruct((T, D_MODEL), F32),
        compiler_params=_params(1),
        name="out_proj",
    )(*args)


def kernel(x_prompt, x_sample, cache_l0_mla_ckv, cache_l0_mla_krope, state_l0_ret_fwd, state_l0_ret_bwd, cache_l1_win_k, cache_l1_win_v, cache_l2_mla_ckv, cache_l2_mla_krope, state_l2_ret_fwd, state_l2_ret_bwd, cache_l3_win_k, cache_l3_win_v, c, c_ctx, ada_w, ada_b, norm_g, ab_w_in, mla_q_norm_g, mla_w_uq, mla_kv_norm_g, mla_w_ukv, mla_q_head_g, mla_k_head_g, ret_decay, ret_norm_g, ab_w_out, win_w_in, win_q_head_g, win_k_head_g, win_sink, win_w_out):
    BP, LP, D = x_prompt.shape
    BS, LS, _ = x_sample.shape
    P = cache_l0_mla_ckv.shape[1]
    ctx_caches = ((cache_l0_mla_ckv, cache_l0_mla_krope, state_l0_ret_fwd, state_l0_ret_bwd),
                  (cache_l1_win_k, cache_l1_win_v),
                  (cache_l2_mla_ckv, cache_l2_mla_krope, state_l2_ret_fwd, state_l2_ret_bwd),
                  (cache_l3_win_k, cache_l3_win_v))

    cond = jnp.concatenate([c, c_ctx[None, :], jnp.zeros((8 - BS - 1, D), F32)], axis=0)
    mod_all = _ada_mod(cond, ada_w, ada_b).reshape(DEPTH, 8, 1, 3 * D)
    ctx_row = BS

    c32, s32 = _rope_tables(LS, MLA_ROPE)
    c64, s64 = _rope_tables(LS, RET_DK)
    ones = jnp.ones((LS, MLA_NOPE), F32)
    zeros = jnp.zeros((LS, MLA_NOPE), F32)
    z32 = jnp.zeros((LS, LANES - MLA_QK), F32)
    tables = dict(cm=jnp.concatenate([ones, c32, z32], -1), sm=jnp.concatenate([zeros, s32, z32], -1),
                  c64=jnp.concatenate([c64, c64], -1), s64=jnp.concatenate([s64, s64], -1))

    y_p = x_prompt.reshape(BP * LP, D)
    y_s = x_sample.reshape(BS * LS, D)
    tm = TOKEN_BLOCK
    new_state = []
    pend_p = pend_s = None

    def take_stream(outs, pending, y):
        return (outs[:-1], outs[-1]) if pending is not None else (outs, y)

    for l in range(DEPTH):
        i = l // 2
        mod = mod_all[l]
        p3 = lambda a: a.reshape(BP, LP, a.shape[-1])
        s3 = lambda a: a.reshape(BS, -1, a.shape[-1])
        flat = lambda a: a.reshape(-1, a.shape[-1])
        if l % 2 == 0:
            p = {'w_in': ab_w_in[i], 'q_norm_g': mla_q_norm_g[i], 'w_uq': mla_w_uq[i],
                 'kv_norm_g': mla_kv_norm_g[i], 'w_ukv': mla_w_ukv[i], 'q_head_g': mla_q_head_g[i],
                 'k_head_g': mla_k_head_g[i]}
            w_out = ab_w_out[i].astype(BF16)
            parts_w = (w_out[:MLA_WIDTH], w_out[MLA_WIDTH:])
            ckv_c, krope_c, sf, sb = ctx_caches[l]

            outs, y_p = take_stream(_prep_ab(y_p, mod, norm_g[l], p, False, LP, ctx_row, None, tm, res=pend_p),
                                    pend_p, y_p)
            q, k, v, gates, rq, rk, rv, ckv, kr = outs
            a_p = _mla_attn(p3(q), p3(k), p3(v), p3(gates), PROMPT_ROWS_PER_STEP, LP, MLA_PAIRS_PER_STEP)
            zst = jnp.zeros((BP, RET_HEADS, RET_DK, RET_DV), F32)
            r_p, rf, rb = _retention(p3(rq), p3(rk), p3(rv), zst, zst, ret_decay[i], ret_norm_g[i], p3(gates),
                                     PROMPT_ROWS_PER_STEP)
            new_state.append((ckv.reshape(BP, LP, KV_LORA),
                              kr[:, MLA_NOPE:MLA_QK].reshape(BP, LP, MLA_ROPE), rf, rb))
            pend_p = (mod, [(flat(a_p), parts_w[0]), (flat(r_p), parts_w[1])])

            kr_c = jnp.pad(krope_c.reshape(BS * P, MLA_ROPE), ((0, 0), (MLA_NOPE, LANES - MLA_QK)))
            outs, y_s = take_stream(_prep_ab(y_s, mod, norm_g[l], p, True, LS, None, tables, tm,
                                             (ckv_c.reshape(BS * P, KV_LORA), kr_c), res=pend_s), pend_s, y_s)
            q, k, v, gates, rq, rk, rv = outs
            a_s = _mla_attn(s3(q), s3(k), s3(v), s3(gates), 1, MLA_Q_BLOCK, MLA_PAIRS_PER_STEP)
            r_s, _, _ = _retention(s3(rq), s3(rk), s3(rv), sf, sb, ret_decay[i], ret_norm_g[i], s3(gates), 1)
            pend_s = (mod, [(flat(a_s), parts_w[0]), (flat(r_s), parts_w[1])])
        else:
            p = {'w_in': win_w_in[i], 'q_head_g': win_q_head_g[i], 'k_head_g': win_k_head_g[i]}
            w_out = win_w_out[i].astype(BF16)
            sink = win_sink[i]
            kc, vc = ctx_caches[l]

            outs, y_p = take_stream(_prep_win(y_p, mod, norm_g[l], p, False, LP, ctx_row, None, tm, res=pend_p),
                                    pend_p, y_p)
            q, k, v, gates, kst, vst = outs
            o_p = _win_attn(p3(q), p3(k), p3(v), None, p3(gates), sink, PROMPT_ROWS_PER_STEP, LP,
                            WIN_KV_PER_STEP_PROMPT)
            new_state.append((kst.reshape(BP, LP, WIN_KV_HEADS, WIN_HEAD_DIM),
                              vst.reshape(BP, LP, WIN_KV_HEADS, WIN_HEAD_DIM)))
            pend_p = (mod, [(flat(o_p), w_out)])

            outs, y_s = take_stream(_prep_win(y_s, mod, norm_g[l], p, True, LS, None, tables, tm, res=pend_s),
                                    pend_s, y_s)
            q, k, v, gates = outs

            def slots(a, fill):
                a = a.astype(BF16)
                f = jnp.broadcast_to(fill.astype(BF16), a.shape)
                return jnp.concatenate([a, f, f, a], axis=-1).reshape(BS, P, 4 * 2 * LANES)

            zero = jnp.zeros((WIN_HEAD_DIM,), F32)
            o_s = _win_attn(s3(q), slots(kc, zero), slots(vc, zero.at[0].set(1.0)), (s3(k), s3(v)), s3(gates),
                            sink, 1, WIN_Q_BLOCK, WIN_KV_PER_STEP)
            pend_s = (mod, [(flat(o_s), w_out)])

    y_p = _out_proj(y_p, pend_p[0], pend_p[1], LP, ctx_row, tm)
    y_s = _out_proj(y_s, pend_s[0], pend_s[1], LS, None, tm)

    (l0_ckv, l0_krope, l0_rf, l0_rb), (l1_k, l1_v), (l2_ckv, l2_krope, l2_rf, l2_rb), (l3_k, l3_v) = new_state
    return (y_p.reshape(BP, LP, D), y_s.reshape(BS, LS, D), l0_ckv, l0_krope, l0_rf, l0_rb, l1_k, l1_v,
            l2_ckv, l2_krope, l2_rf, l2_rb, l3_k, l3_v)
```
